```python
import math
import jax, jax.numpy as jnp
from jax import lax
import numpy as np


D_MODEL = 1024
BATCH = 1
SEQ = 16384
DEPTH = 4

CHUNK = 64

SSD_EXPAND = 2
SSD_INNER = SSD_EXPAND * D_MODEL
SSD_HEADDIM = 64
SSD_HEADS = SSD_INNER // SSD_HEADDIM
SSD_GROUPS = 4
SSD_STATE = 128
SSD_CONV = 4
SSD_CONV_DIM = SSD_INNER + 2 * SSD_GROUPS * SSD_STATE
DT_MIN = 0.001
DT_MAX = 0.1

SC_WIDTH = D_MODEL
SC_CONV = 3

N_BRANCHES = 2

_S0 = SSD_INNER
_S1 = _S0 + SSD_CONV_DIM
_S2 = _S1 + SSD_HEADS
_S3 = _S2 + SC_WIDTH
_S4 = _S3 + SC_WIDTH
_S5 = _S4 + SC_WIDTH
IN_DIM = _S5 + N_BRANCHES * D_MODEL
IN_SPLITS = (_S0, _S1, _S2, _S3, _S4, _S5)

N_EXPERTS = 32
TOP_K = 4
EXPERT_FF = D_MODEL
SWIGLU_LIMIT = 7.0
SWIGLU_ALPHA = 1.702

DN_ALPHA = (2.0 * DEPTH) ** 0.25
DN_BETA = (8.0 * DEPTH) ** -0.25

LN_EPS = 1e-5
RMS_EPS = 1e-5
N_ADA = 6

kernel_name = 'hybrid_ssd_shortconv_moe_deepnorm'


def layer_norm(x, g=None, b=None):
    xf = x.astype(jnp.float32)
    mu = jnp.mean(xf, axis=-1, keepdims=True)
    var = jnp.mean(jnp.square(xf - mu), axis=-1, keepdims=True)
    y = ((xf - mu) * lax.rsqrt(var + LN_EPS)).astype(x.dtype)
    if g is not None:
        y = y * g + b
    return y


def rms_norm(x, g):
    xf = x.astype(jnp.float32)
    y = xf * lax.rsqrt(jnp.mean(jnp.square(xf), axis=-1, keepdims=True) + RMS_EPS)
    return y.astype(x.dtype) * g


def causal_dwconv(u, w):
    k, ch = w.shape
    return lax.conv_general_dilated(
        u, w[:, None, :], window_strides=(1,), padding=[(k - 1, 0)],
        dimension_numbers=('NWC', 'WIO', 'NWC'), feature_group_count=ch)


def ssd_chunked(x, dt, a, bm, cm):
    b, s, h, p = x.shape
    g, n = bm.shape[-2:]
    k = h // g
    nc = s // CHUNK
    dtype = x.dtype
    xd = (x * dt[..., None]).reshape(b, nc, CHUNK, g, k, p)
    la = (dt * a).astype(jnp.float32).reshape(b, nc, CHUNK, g, k)
    la = jnp.moveaxis(la, 2, -1)
    a_cum = jnp.cumsum(la, axis=-1)
    bc = bm.reshape(b, nc, CHUNK, g, n)
    cc = cm.reshape(b, nc, CHUNK, g, n)
    causal = jnp.tril(jnp.ones((CHUNK, CHUNK), dtype=bool))
    seg = a_cum[..., :, None] - a_cum[..., None, :]
    l_dec = jnp.exp(jnp.where(causal, seg, -jnp.inf)).astype(dtype)
    cb = jnp.einsum('bclgn,bcsgn->bcgls', cc, bc)
    y_diag = jnp.einsum('bcgls,bcgkls,bcsgkp->bclgkp', cb, l_dec, xd)
    decay_states = jnp.exp(a_cum[..., -1:] - a_cum).astype(dtype)
    states = jnp.einsum('bcsgn,bcgks,bcsgkp->bcgkpn', bc, decay_states, xd)
    chunk_decay = jnp.exp(a_cum[..., -1]).astype(dtype)

    def step(carry, inp):
        st, dec = inp
        return carry * dec[..., None, None] + st, carry

    init = jnp.zeros((b, g, k, p, n), dtype=states.dtype)
    _, prev = lax.scan(step, init, (jnp.moveaxis(states, 1, 0), jnp.moveaxis(chunk_decay, 1, 0)))
    prev = jnp.moveaxis(prev, 0, 1)
    y_off = jnp.einsum('bclgn,bcgkpn,bcgkl->bclgkp', cc, prev, jnp.exp(a_cum).astype(dtype))
    return (y_diag + y_off).reshape(b, s, h, p)


def hybrid_mixer(h, w_in, conv_ssd_w, conv_ssd_b, dt_bias, a_log, d_skip, ssd_norm_w,
                 w_ssd_out, conv_short_w, w_short_out, b_gate, w_o):
    b, s, _ = h.shape
    u = h @ w_in
    z, xbc, dt_raw, sc_b, sc_c, sc_x, gate_logits = jnp.split(u, IN_SPLITS, axis=-1)
    xbc = jax.nn.silu(causal_dwconv(xbc, conv_ssd_w) + conv_ssd_b)
    xs, bm, cm = jnp.split(xbc, (SSD_INNER, SSD_INNER + SSD_GROUPS * SSD_STATE), axis=-1)
    xs = xs.reshape(b, s, SSD_HEADS, SSD_HEADDIM)
    bm = bm.reshape(b, s, SSD_GROUPS, SSD_STATE)
    cm = cm.reshape(b, s, SSD_GROUPS, SSD_STATE)
    dt = jax.nn.softplus(dt_raw + dt_bias)
    a = -jnp.exp(a_log)
    y = ssd_chunked(xs, dt, a, bm, cm) + xs * d_skip[:, None]
    y = rms_norm(y.reshape(b, s, SSD_INNER) * jax.nn.silu(z), ssd_norm_w)
    u_a = y @ w_ssd_out
    v = causal_dwconv(sc_c * sc_x, conv_short_w)
    u_b = (sc_b * v) @ w_short_out
    g_a, g_b = jnp.split(jax.nn.sigmoid(gate_logits + b_gate), N_BRANCHES, axis=-1)
    return (g_a * u_a + g_b * u_b) @ w_o


def moe_ffn(h, w_router, b_router, w_gu, b_gu, w_down, b_down):
    b, s, d = h.shape
    t = h.reshape(b * s, d)
    logits = (t @ w_router + b_router).astype(jnp.float32)
    top_vals, top_idx = lax.top_k(logits, TOP_K)
    probs = jax.nn.softmax(top_vals, axis=-1).astype(h.dtype)
    flat_e = top_idx.reshape(-1)
    order = jnp.argsort(flat_e)
    e_sorted = flat_e[order]
    tok_sorted = order // TOP_K
    group_sizes = jnp.bincount(flat_e, length=N_EXPERTS).astype(jnp.int32)
    xs = t[tok_sorted]
    gu = lax.ragged_dot(xs, w_gu, group_sizes) + b_gu[e_sorted]
    gate, up = jnp.split(gu, 2, axis=-1)
    gate = jnp.minimum(gate, SWIGLU_LIMIT)
    up = jnp.clip(up, -SWIGLU_LIMIT, SWIGLU_LIMIT)
    act = (up + 1.0) * gate * jax.nn.sigmoid(SWIGLU_ALPHA * gate)
    out = lax.ragged_dot(act, w_down, group_sizes) + b_down[e_sorted]
    slots = jnp.zeros_like(out).at[order].set(out).reshape(b * s, TOP_K, d)
    return jnp.einsum('tkd,tk->td', slots, probs).reshape(b, s, d)


def modulate(x, shift, scale):
    return layer_norm(x) * (1.0 + scale) + shift


def setup_inputs(seed: int = 0) -> dict:
    key = jax.random.key(seed)
    ks = jax.random.split(key, 26)
    L, D = DEPTH, D_MODEL

    def nrm(k, shape, scale):
        return jax.random.normal(k, shape, jnp.float32) * scale

    dt0 = jnp.exp(jax.random.uniform(ks[7], (L, SSD_HEADS), jnp.float32,
                                     minval=math.log(DT_MIN), maxval=math.log(DT_MAX)))
    return {
        'x': nrm(ks[0], (BATCH, SEQ, D), 1.0),
        'c': nrm(ks[1], (BATCH, D), 1.0),
        'w_ada': nrm(ks[2], (L, D, N_ADA * D), D ** -0.5),
        'b_ada': nrm(ks[3], (L, N_ADA * D), 0.02),
        'w_in': nrm(ks[4], (L, D, IN_DIM), D ** -0.5),
        'conv_ssd_w': nrm(ks[5], (L, SSD_CONV, SSD_CONV_DIM), SSD_CONV ** -0.5),
        'conv_ssd_b': nrm(ks[6], (L, SSD_CONV_DIM), 0.02),
        'dt_bias': dt0 + jnp.log(-jnp.expm1(-dt0)),
        'a_log': jnp.log(jax.random.uniform(ks[8], (L, SSD_HEADS), jnp.float32, minval=1.0, maxval=16.0)),
        'd_skip': 1.0 + nrm(ks[9], (L, SSD_HEADS), 0.02),
        'ssd_norm_w': 1.0 + nrm(ks[10], (L, SSD_INNER), 0.02),
        'w_ssd_out': nrm(ks[11], (L, SSD_INNER, D), DN_BETA * SSD_INNER ** -0.5),
        'conv_short_w': nrm(ks[12], (L, SC_CONV, SC_WIDTH), SC_CONV ** -0.5),
        'w_short_out': nrm(ks[13], (L, SC_WIDTH, D), DN_BETA * SC_WIDTH ** -0.5),
        'b_gate': nrm(ks[14], (L, N_BRANCHES * D), 0.02),
        'w_o': nrm(ks[15], (L, D, D), DN_BETA * D ** -0.5),
        'ln1_g': 1.0 + nrm(ks[16], (L, D), 0.02),
        'ln1_b': nrm(ks[17], (L, D), 0.02),
        'w_router': nrm(ks[18], (L, D, N_EXPERTS), D ** -0.5),
        'b_router': nrm(ks[19], (L, N_EXPERTS), 0.01),
        'w_gu': nrm(ks[20], (L, N_EXPERTS, D, 2 * EXPERT_FF), D ** -0.5),
        'b_gu': nrm(ks[21], (L, N_EXPERTS, 2 * EXPERT_FF), 0.02),
        'w_down': nrm(ks[22], (L, N_EXPERTS, EXPERT_FF, D), DN_BETA * EXPERT_FF ** -0.5),
        'b_down': nrm(ks[23], (L, N_EXPERTS, D), 0.02),
        'ln2_g': 1.0 + nrm(ks[24], (L, D), 0.02),
        'ln2_b': nrm(ks[25], (L, D), 0.02),
    }


def reference(x, c, w_ada, b_ada, w_in, conv_ssd_w, conv_ssd_b, dt_bias, a_log, d_skip,
              ssd_norm_w, w_ssd_out, conv_short_w, w_short_out, b_gate, w_o, ln1_g, ln1_b,
              w_router, b_router, w_gu, b_gu, w_down, b_down, ln2_g, ln2_b):
    for l in range(DEPTH):
        mod = jax.nn.silu(c) @ w_ada[l] + b_ada[l]
        sh1, sc1, g1, sh2, sc2, g2 = jnp.split(mod[:, None, :], N_ADA, axis=-1)
        h = modulate(x, sh1, sc1)
        mix = hybrid_mixer(h, w_in[l], conv_ssd_w[l], conv_ssd_b[l], dt_bias[l], a_log[l],
                           d_skip[l], ssd_norm_w[l], w_ssd_out[l], conv_short_w[l],
                           w_short_out[l], b_gate[l], w_o[l])
        x = layer_norm(DN_ALPHA * x + g1 * mix, ln1_g[l], ln1_b[l])
        h = modulate(x, sh2, sc2)
        ffn = moe_ffn(h, w_router[l], b_router[l], w_gu[l], b_gu[l], w_down[l], b_down[l])
        x = layer_norm(DN_ALPHA * x + g2 * ffn, ln2_g[l], ln2_b[l])
    return x
```

```python
import functools

import jax
import jax.numpy as jnp
from jax import lax
from jax.experimental import pallas as pl
from jax.experimental.pallas import tpu as pltpu

F32 = jnp.float32
BF16 = jnp.bfloat16

D_MODEL = 1024
SSD_INNER = 2048
SSD_HEADS = 32
SSD_HEADDIM = 64
SSD_GROUPS = 4
SSD_STATE = 128
SSD_CONV = 4
SC_CONV = 3
N_EXPERTS = 32
TOP_K = 4
EXPERT_FF = 1024
SWIGLU_LIMIT = 7.0
SWIGLU_ALPHA = 1.702
DEPTH = 4
DN_ALPHA = (2.0 * DEPTH) ** 0.25
LN_EPS = 1e-5
RMS_EPS = 1e-5
N_ADA = 6

LANES = 128
SUBLANES = 8
U_MAIN = 10240
COL_Z = 0
COL_XS = 2048
COL_B = 4096
COL_C = 4608
COL_SCB = 5120
COL_SCC = 6144
COL_SCX = 7168
COL_GATE = 8192
NEG_BIG = -1e30

VMEM_LIMIT = 56 * 1024 * 1024


def _sigmoid(v):
    return 1.0 / (1.0 + jnp.exp(-v))


def _softplus(v):
    return jnp.maximum(v, 0.0) + jnp.log(1.0 + jnp.exp(-jnp.abs(v)))


def _layer_norm(v):
    mu = jnp.mean(v, axis=-1, keepdims=True)
    vc = v - mu
    var = jnp.mean(vc * vc, axis=-1, keepdims=True)
    return vc * lax.rsqrt(var + LN_EPS)


def _split3(v):
    hi = v.astype(BF16)
    r1 = v - hi.astype(F32)
    mid = r1.astype(BF16)
    lo = (r1 - mid.astype(F32)).astype(BF16)
    return hi, mid, lo


def _dot(a, b):
    return jnp.dot(a, b, preferred_element_type=F32)


def _dot_exact_lhs(a_bf16, v):
    hi, mid, lo = _split3(v)
    return _dot(a_bf16, hi) + _dot(a_bf16, mid) + _dot(a_bf16, lo)


def _ada_kernel(c_ref, w_ref, b_ref, o_ref):
    c = c_ref[...]
    s = c * _sigmoid(c)
    o_ref[0] = jnp.sum(w_ref[0] * s, axis=0, keepdims=True) + b_ref[0]


def _ada_mod(c, w_ada, b_ada):
    depth, d, n = w_ada.shape
    tn = 1024
    return pl.pallas_call(
        _ada_kernel,
        grid=(depth, n // tn),
        in_specs=[
            pl.BlockSpec((d, 1), lambda l, j: (0, 0)),
            pl.BlockSpec((1, d, tn), lambda l, j: (l, 0, j)),
            pl.BlockSpec((1, 1, tn), lambda l, j: (l, 0, j)),
        ],
        out_specs=pl.BlockSpec((1, 1, tn), lambda l, j: (l, 0, j)),
        out_shape=jax.ShapeDtypeStruct((depth, 1, n), F32),
        compiler_params=pltpu.CompilerParams(
            dimension_semantics=("arbitrary", "arbitrary"), vmem_limit_bytes=VMEM_LIMIT),
        name="ada_mod",
    )(c.reshape(d, 1), w_ada, b_ada.reshape(depth, 1, n))


def _inproj_kernel(x_ref, sh_ref, sc_ref, w_ref, wdt_ref, u_ref, dt_ref, h_scr):
    @pl.when(pl.program_id(1) == 0)
    def _():
        h = _layer_norm(x_ref[...]) * (1.0 + sc_ref[...]) + sh_ref[...]
        h_scr[...] = h.astype(BF16)
        h3 = _split3(h)
        w3 = _split3(wdt_ref[...])
        dt_ref[...] = (_dot(h3[0], w3[0]) + _dot(h3[0], w3[1]) + _dot(h3[1], w3[0])
                       + _dot(h3[1], w3[1]) + _dot(h3[0], w3[2]) + _dot(h3[2], w3[0]))

    u_ref[...] = _dot(h_scr[...], w_ref[...]).astype(BF16)


def _in_proj(x, sh, sc, w_main, w_dt, tm, tn):
    t, d = x.shape
    n = w_main.shape[1]
    return pl.pallas_call(
        _inproj_kernel,
        grid=(t // tm, n // tn),
        in_specs=[
            pl.BlockSpec((tm, d), lambda i, j: (i, 0)),
            pl.BlockSpec((1, d), lambda i, j: (0, 0)),
            pl.BlockSpec((1, d), lambda i, j: (0, 0)),
            pl.BlockSpec((d, tn), lambda i, j: (0, j)),
            pl.BlockSpec((d, LANES), lambda i, j: (0, 0)),
        ],
        out_specs=[
            pl.BlockSpec((tm, tn), lambda i, j: (i, j)),
            pl.BlockSpec((tm, LANES), lambda i, j: (i, 0)),
        ],
        out_shape=[
            jax.ShapeDtypeStruct((t, n), BF16),
            jax.ShapeDtypeStruct((t, LANES), F32),
        ],
        scratch_shapes=[pltpu.VMEM((tm, d), BF16)],
        compiler_params=pltpu.CompilerParams(
            dimension_semantics=("arbitrary", "arbitrary"), vmem_limit_bytes=VMEM_LIMIT),
        name="in_proj",
    )(x, sh, sc, w_main, w_dt)


def _ssd_kernel(xs_ref, b_ref, c_ref, dt_ref, cwx_ref, cwb_ref, cwc_ref, cbx_ref, cbb_ref,
                cbc_ref, dtb_ref, alog_ref, dskip_ref, y_ref,
                xbuf, bbuf, cbuf, xcs, bcs, ccs, state, *, chunk, rows):
    @pl.when(pl.program_id(0) == 0)
    def _():
        xbuf[0:SUBLANES, :] = jnp.zeros((SUBLANES, xbuf.shape[1]), F32)
        bbuf[0:SUBLANES, :] = jnp.zeros((SUBLANES, bbuf.shape[1]), F32)
        cbuf[0:SUBLANES, :] = jnp.zeros((SUBLANES, cbuf.shape[1]), F32)
        state[...] = jnp.zeros(state.shape, F32)

    def conv_silu(in_ref, buf, w_ref, bias_ref, out_scr):
        buf[SUBLANES:SUBLANES + rows, :] = in_ref[...].astype(F32)
        acc = bias_ref[...]
        for j in range(SSD_CONV):
            off = SUBLANES - (SSD_CONV - 1) + j
            acc = acc + w_ref[j:j + 1, :] * buf[off:off + rows, :]
        out_scr[...] = acc * _sigmoid(acc)
        buf[0:SUBLANES, :] = buf[rows:rows + SUBLANES, :]

    conv_silu(xs_ref, xbuf, cwx_ref, cbx_ref, xcs)
    conv_silu(b_ref, bbuf, cwb_ref, cbb_ref, bcs)
    conv_silu(c_ref, cbuf, cwc_ref, cbc_ref, ccs)

    li = lax.broadcasted_iota(jnp.int32, (chunk, chunk), 0)
    si = lax.broadcasted_iota(jnp.int32, (chunk, chunk), 1)
    causal = li >= si
    tri = jnp.where(causal, 1.0, 0.0).astype(BF16)
    first_half = lax.broadcasted_iota(jnp.int32, (1, LANES), 1) < SSD_HEADDIM
    a_row = -jnp.exp(alog_ref[...])
    heads_per_group = SSD_HEADS // SSD_GROUPS

    def chunk_body(ci, carry):
        r0 = pl.multiple_of(ci * chunk, chunk)
        dt = _softplus(dt_ref[pl.ds(r0, chunk), :] + dtb_ref[...])
        la = dt * a_row
        acum = _dot_exact_lhs(tri, la)
        exp_acum = jnp.exp(acum)
        acum_t = acum.T
        dt_t = dt.T
        last_t = acum_t[:, chunk - 1:chunk]
        w_t = jnp.exp(last_t - acum_t) * dt_t
        cdec_t = jnp.exp(last_t)

        cb = []
        bt = []
        cg = []
        for g in range(SSD_GROUPS):
            bg = bcs[pl.ds(r0, chunk), g * SSD_STATE:(g + 1) * SSD_STATE]
            cgv = ccs[pl.ds(r0, chunk), g * SSD_STATE:(g + 1) * SSD_STATE]
            btg = bg.T
            bt.append(btg)
            cg.append(cgv)
            cb.append(_dot(cgv.astype(BF16), btg.astype(BF16)))

        for pr in range(SSD_HEADS // 2):
            lo = pr * LANES
            xs_pair = xcs[pl.ds(r0, chunk), lo:lo + LANES]
            xs_bf = xs_pair.astype(BF16)
            prev = state[:, lo:lo + LANES]
            lhs_m, lhs_c, lhs_b, cd = [], [], [], []
            for h in (2 * pr, 2 * pr + 1):
                g = h // heads_per_group
                col = acum[:, h:h + 1]
                row = acum_t[h:h + 1, :]
                dec = jnp.exp(jnp.where(causal, col - row, NEG_BIG))
                lhs_m.append((cb[g] * dec * dt_t[h:h + 1, :]).astype(BF16))
                lhs_c.append((cg[g] * exp_acum[:, h:h + 1]).astype(BF16))
                lhs_b.append((bt[g] * w_t[h:h + 1, :]).astype(BF16))
                cd.append(cdec_t[h:h + 1, :])
            out = (_dot(jnp.concatenate(lhs_m, axis=0), xs_bf)
                   + _dot(jnp.concatenate(lhs_c, axis=0), prev.astype(BF16)))
            y_pair = jnp.where(first_half, out[0:chunk, :], out[chunk:2 * chunk, :])
            y_ref[pl.ds(r0, chunk), lo:lo + LANES] = (
                y_pair + xs_pair * dskip_ref[:, lo:lo + LANES]).astype(y_ref.dtype)
            st = _dot(jnp.concatenate(lhs_b, axis=0), xs_bf)
            cd_pair = jnp.where(first_half, cd[0], cd[1])
            state[:, lo:lo + LANES] = prev * cd_pair + jnp.where(
                first_half, st[0:SSD_STATE, :], st[SSD_STATE:2 * SSD_STATE, :])
        return carry

    lax.fori_loop(0, rows // chunk, chunk_body, 0)


def _ssd(u, dt_raw, cw, cb, dt_bias, a_log, d_skip, rows, chunk):
    t = u.shape[0]
    gn = SSD_GROUPS * SSD_STATE
    pad = LANES - SSD_HEADS
    kern = functools.partial(_ssd_kernel, chunk=chunk, rows=rows)
    full = lambda shape: pl.BlockSpec(shape, lambda i: (0, 0))
    return pl.pallas_call(
        kern,
        grid=(t // rows,),
        in_specs=[
            pl.BlockSpec((rows, SSD_INNER), lambda i: (i, COL_XS // SSD_INNER)),
            pl.BlockSpec((rows, gn), lambda i: (i, COL_B // gn)),
            pl.BlockSpec((rows, gn), lambda i: (i, COL_C // gn)),
            pl.BlockSpec((rows, LANES), lambda i: (i, 0)),
            full((SSD_CONV, SSD_INNER)), full((SSD_CONV, gn)), full((SSD_CONV, gn)),
            full((1, SSD_INNER)), full((1, gn)), full((1, gn)),
            full((1, LANES)), full((1, LANES)), full((1, SSD_INNER)),
        ],
        out_specs=pl.BlockSpec((rows, SSD_INNER), lambda i: (i, 0)),
        out_shape=jax.ShapeDtypeStruct((t, SSD_INNER), BF16),
        scratch_shapes=[
            pltpu.VMEM((rows + SUBLANES, SSD_INNER), F32),
            pltpu.VMEM((rows + SUBLANES, gn), F32),
            pltpu.VMEM((rows + SUBLANES, gn), F32),
            pltpu.VMEM((rows, SSD_INNER), F32),
            pltpu.VMEM((rows, gn), F32),
            pltpu.VMEM((rows, gn), F32),
            pltpu.VMEM((SSD_STATE, SSD_INNER), F32),
        ],
        compiler_params=pltpu.CompilerParams(
            dimension_semantics=("arbitrary",), vmem_limit_bytes=VMEM_LIMIT),
        name="ssd",
    )(u, u, u, dt_raw,
      cw[:, :SSD_INNER], cw[:, SSD_INNER:SSD_INNER + gn], cw[:, SSD_INNER + gn:],
      cb[None, :SSD_INNER], cb[None, SSD_INNER:SSD_INNER + gn], cb[None, SSD_INNER + gn:],
      jnp.pad(dt_bias, (0, pad))[None, :], jnp.pad(a_log, (0, pad))[None, :],
      jnp.repeat(d_skip, SSD_HEADDIM)[None, :])


def _post_kernel(y_ref, z_ref, scb_ref, scc_ref, scx_ref, gate_ref, x_ref,
                 nw_ref, wa_ref, csw_ref, wb_ref, bg_ref, wo_ref, g1_ref, lng_ref, lnb_ref,
                 sh2_ref, sc2_ref, wr_ref, br_ref,
                 x1_ref, h2_ref, idx_ref, prob_ref, rank_ref, cnt_ref,
                 sbuf, run, *, rows):
    @pl.when(pl.program_id(0) == 0)
    def _():
        sbuf[0:SUBLANES, :] = jnp.zeros((SUBLANES, sbuf.shape[1]), F32)
        run[...] = jnp.zeros(run.shape, F32)

    z = z_ref[...].astype(F32)
    yg = y_ref[...].astype(F32) * (z * _sigmoid(z))
    ms = jnp.mean(yg * yg, axis=-1, keepdims=True)
    yn = yg * lax.rsqrt(ms + RMS_EPS) * nw_ref[...]
    u_a = _dot(yn.astype(BF16), wa_ref[...])

    sbuf[SUBLANES:SUBLANES + rows, :] = scc_ref[...].astype(F32) * scx_ref[...].astype(F32)
    v = jnp.zeros((rows, D_MODEL), F32)
    for j in range(SC_CONV):
        off = SUBLANES - (SC_CONV - 1) + j
        v = v + csw_ref[j:j + 1, :] * sbuf[off:off + rows, :]
    sbuf[0:SUBLANES, :] = sbuf[rows:rows + SUBLANES, :]
    u_b = _dot((scb_ref[...].astype(F32) * v).astype(BF16), wb_ref[...])

    gl = gate_ref[...].astype(F32) + bg_ref[...]
    merged = _sigmoid(gl[:, :D_MODEL]) * u_a + _sigmoid(gl[:, D_MODEL:]) * u_b
    mix = _dot(merged.astype(BF16), wo_ref[...])
    x1 = _layer_norm(DN_ALPHA * x_ref[...] + g1_ref[...] * mix) * lng_ref[...] + lnb_ref[...]
    x1_ref[...] = x1
    h2 = _layer_norm(x1) * (1.0 + sc2_ref[...]) + sh2_ref[...]
    h2_ref[...] = h2

    h3 = _split3(h2)
    w3 = _split3(wr_ref[...])
    logits = (_dot(h3[0], w3[0]) + _dot(h3[0], w3[1]) + _dot(h3[1], w3[0])
              + _dot(h3[1], w3[1]) + _dot(h3[0], w3[2]) + _dot(h3[2], w3[0])) + br_ref[...]
    lane = lax.broadcasted_iota(jnp.int32, (rows, LANES), 1).astype(F32)
    work = logits
    onehots, vals = [], []
    idx_out = jnp.zeros((rows, LANES), F32)
    for k in range(TOP_K):
        m = jnp.max(work, axis=-1, keepdims=True)
        ik = jnp.min(jnp.where(work == m, lane, float(LANES)), axis=-1, keepdims=True)
        oh = lane == ik
        onehots.append(oh)
        vals.append(m)
        idx_out = jnp.where(lane == float(k), ik, idx_out)
        work = jnp.where(oh, -jnp.inf, work)
    es = [jnp.exp(vk - vals[0]) for vk in vals]
    denom = es[0] + es[1] + es[2] + es[3]
    prob_out = jnp.zeros((rows, LANES), F32)
    for k in range(TOP_K):
        prob_out = jnp.where(lane == float(k), es[k] / denom, prob_out)

    sel = jnp.zeros((rows, LANES), F32)
    for oh in onehots:
        sel = sel + jnp.where(oh, 1.0, 0.0)
    ri = lax.broadcasted_iota(jnp.int32, (rows, rows), 0)
    rj = lax.broadcasted_iota(jnp.int32, (rows, rows), 1)
    strict = jnp.where(ri > rj, 1.0, 0.0).astype(BF16)
    base = _dot(strict, sel.astype(BF16)) + run[...]
    rank_out = jnp.zeros((rows, LANES), F32)
    for k in range(TOP_K):
        rk = jnp.sum(jnp.where(onehots[k], base, 0.0), axis=-1, keepdims=True)
        rank_out = jnp.where(lane == float(k), rk, rank_out)
    run[...] = run[...] + jnp.sum(sel, axis=0, keepdims=True)

    idx_ref[...] = idx_out.astype(jnp.int32)
    prob_ref[...] = prob_out
    rank_ref[...] = rank_out.astype(jnp.int32)
    cnt_ref[...] = run[...].astype(jnp.int32)


def _post(y, u, x, vecs, mats, rows):
    t = x.shape[0]
    kern = functools.partial(_post_kernel, rows=rows)
    row = lambda w: pl.BlockSpec((1, w), lambda i: (0, 0))
    mat = lambda a, b: pl.BlockSpec((a, b), lambda i: (0, 0))
    ublk = lambda w, col: pl.BlockSpec((rows, w), lambda i: (i, col // w))
    nw, csw, bg, g1, lng, lnb, sh2, sc2, br = vecs
    wa, wb, wo, wr = mats
    tok = pl.BlockSpec((rows, LANES), lambda i: (i, 0))
    return pl.pallas_call(
        kern,
        grid=(t // rows,),
        in_specs=[
            pl.BlockSpec((rows, SSD_INNER), lambda i: (i, 0)),
            ublk(SSD_INNER, COL_Z), ublk(D_MODEL, COL_SCB), ublk(D_MODEL, COL_SCC),
            ublk(D_MODEL, COL_SCX), ublk(2 * D_MODEL, COL_GATE),
            pl.BlockSpec((rows, D_MODEL), lambda i: (i, 0)),
            row(SSD_INNER), mat(SSD_INNER, D_MODEL), mat(SC_CONV, D_MODEL), mat(D_MODEL, D_MODEL),
            row(2 * D_MODEL), mat(D_MODEL, D_MODEL), row(D_MODEL), row(D_MODEL), row(D_MODEL),
            row(D_MODEL), row(D_MODEL), mat(D_MODEL, LANES), row(LANES),
        ],
        out_specs=[
            pl.BlockSpec((rows, D_MODEL), lambda i: (i, 0)),
            pl.BlockSpec((rows, D_MODEL), lambda i: (i, 0)),
            tok, tok, tok,
            pl.BlockSpec((1, LANES), lambda i: (0, 0)),
        ],
        out_shape=[
            jax.ShapeDtypeStruct((t, D_MODEL), F32),
            jax.ShapeDtypeStruct((t, D_MODEL), F32),
            jax.ShapeDtypeStruct((t, LANES), jnp.int32),
            jax.ShapeDtypeStruct((t, LANES), F32),
            jax.ShapeDtypeStruct((t, LANES), jnp.int32),
            jax.ShapeDtypeStruct((1, LANES), jnp.int32),
        ],
        scratch_shapes=[
            pltpu.VMEM((rows + SUBLANES, D_MODEL), F32),
            pltpu.VMEM((1, LANES), F32),
        ],
        compiler_params=pltpu.CompilerParams(
            dimension_semantics=("arbitrary",), vmem_limit_bytes=VMEM_LIMIT),
        name="post_mix",
    )(y, u, u, u, u, u, x, nw, wa, csw, wb, bg, wo, g1, lng, lnb, sh2, sc2, wr, br)


def _dispatch_kernel(pos_ref, h_ref, xs_in_ref, xs_ref, sem, *, rows):
    del xs_in_ref

    def row_copy(r, p):
        return pltpu.make_async_copy(h_ref.at[pl.ds(r, 1)], xs_ref.at[pl.ds(p, 1)], sem)

    def issue(r, carry):
        for k in range(TOP_K):
            row_copy(r, pos_ref[r * TOP_K + k]).start()
        return carry

    def drain(r, carry):
        for k in range(TOP_K):
            row_copy(r, pos_ref[r * TOP_K + k]).wait()
        return carry

    lax.fori_loop(0, rows, issue, 0)
    lax.fori_loop(0, rows, drain, 0)


def _dispatch(pos_flat, h2, n_rows, rows):
    t = h2.shape[0]
    kern = functools.partial(_dispatch_kernel, rows=rows)
    return pl.pallas_call(
        kern,
        grid=(t // rows,),
        in_specs=[
            pl.BlockSpec((rows * TOP_K,), lambda i: (i,), memory_space=pltpu.SMEM),
            pl.BlockSpec((rows, D_MODEL), lambda i: (i, 0)),
            pl.BlockSpec(memory_space=pl.ANY),
        ],
        out_specs=pl.BlockSpec(memory_space=pl.ANY),
        out_shape=jax.ShapeDtypeStruct((n_rows, D_MODEL), F32),
        scratch_shapes=[pltpu.SemaphoreType.DMA(())],
        input_output_aliases={2: 0},
        compiler_params=pltpu.CompilerParams(
            dimension_semantics=("arbitrary",), vmem_limit_bytes=VMEM_LIMIT),
        name="moe_dispatch",
    )(pos_flat, h2, jnp.zeros((n_rows, D_MODEL), F32))


def _experts_kernel(te_ref, na_ref, x_ref, wgu_ref, bgu_ref, wd_ref, bd_ref, o_ref,
                    wgu_bf, wd_bf):
    i = pl.program_id(0)
    active = i < na_ref[0]
    prev_e = te_ref[jnp.maximum(i - 1, 0)]
    fresh = jnp.logical_or(i == 0, te_ref[i] != prev_e)

    @pl.when(jnp.logical_and(active, fresh))
    def _():
        wgu_bf[...] = wgu_ref[...].astype(BF16)
        wd_bf[...] = wd_ref[...].astype(BF16)

    @pl.when(active)
    def _():
        gu = _dot(x_ref[...].astype(BF16), wgu_bf[...]) + bgu_ref[...]
        gate = jnp.minimum(gu[:, :EXPERT_FF], SWIGLU_LIMIT)
        up = jnp.clip(gu[:, EXPERT_FF:], -SWIGLU_LIMIT, SWIGLU_LIMIT)
        act = (up + 1.0) * gate * _sigmoid(SWIGLU_ALPHA * gate)
        o_ref[...] = _dot(act.astype(BF16), wd_bf[...]) + bd_ref[...]

    @pl.when(jnp.logical_not(active))
    def _():
        o_ref[...] = jnp.zeros(o_ref.shape, F32)


def _experts(tile_expert, n_active, xs, w_gu, b_gu, w_down, b_down, layer, tm):
    n_rows = xs.shape[0]
    n_tiles = n_rows // tm
    ff2 = 2 * EXPERT_FF
    return pl.pallas_call(
        _experts_kernel,
        grid_spec=pltpu.PrefetchScalarGridSpec(
            num_scalar_prefetch=2,
            grid=(n_tiles,),
            in_specs=[
                pl.BlockSpec((tm, D_MODEL), lambda i, te, na: (i, 0)),
                pl.BlockSpec((None, None, D_MODEL, ff2), lambda i, te, na: (layer, te[i], 0, 0)),
                pl.BlockSpec((None, None, 1, ff2), lambda i, te, na: (layer, te[i], 0, 0)),
                pl.BlockSpec((None, None, EXPERT_FF, D_MODEL), lambda i, te, na: (layer, te[i], 0, 0)),
                pl.BlockSpec((None, None, 1, D_MODEL), lambda i, te, na: (layer, te[i], 0, 0)),
            ],
            out_specs=pl.BlockSpec((tm, D_MODEL), lambda i, te, na: (i, 0)),
            scratch_shapes=[
                pltpu.VMEM((D_MODEL, ff2), BF16),
                pltpu.VMEM((EXPERT_FF, D_MODEL), BF16),
            ],
        ),
        out_shape=jax.ShapeDtypeStruct((n_rows, D_MODEL), F32),
        compiler_params=pltpu.CompilerParams(
            dimension_semantics=("arbitrary",), vmem_limit_bytes=VMEM_LIMIT),
        name="moe_experts",
    )(tile_expert, n_active, xs, w_gu, b_gu[:, :, None, :], w_down, b_down[:, :, None, :])


def _combine_kernel(pos_ref, prob_ref, x1_ref, os_ref, g2_ref, lng_ref, lnb_ref, x2_ref,
                    gbuf, sem, *, rows):
    def row_copy(r, k, p):
        return pltpu.make_async_copy(os_ref.at[pl.ds(p, 1)], gbuf.at[k, pl.ds(r, 1)], sem)

    def issue(r, carry):
        for k in range(TOP_K):
            row_copy(r, k, pos_ref[r * TOP_K + k]).start()
        return carry

    def drain(r, carry):
        for k in range(TOP_K):
            row_copy(r, k, pos_ref[r * TOP_K + k]).wait()
        return carry

    lax.fori_loop(0, rows, issue, 0)
    lax.fori_loop(0, rows, drain, 0)

    prob = prob_ref[...]
    ffn = jnp.zeros((rows, D_MODEL), F32)
    for k in range(TOP_K):
        ffn = ffn + prob[:, k:k + 1] * gbuf[k]
    x2_ref[...] = (_layer_norm(DN_ALPHA * x1_ref[...] + g2_ref[...] * ffn) * lng_ref[...]
                   + lnb_ref[...])


def _combine(pos_flat, probs, x1, out_sorted, g2, lng, lnb, rows):
    t = x1.shape[0]
    kern = functools.partial(_combine_kernel, rows=rows)
    row = pl.BlockSpec((1, D_MODEL), lambda i: (0, 0))
    return pl.pallas_call(
        kern,
        grid=(t // rows,),
        in_specs=[
            pl.BlockSpec((rows * TOP_K,), lambda i: (i,), memory_space=pltpu.SMEM),
            pl.BlockSpec((rows, LANES), lambda i: (i, 0)),
            pl.BlockSpec((rows, D_MODEL), lambda i: (i, 0)),
            pl.BlockSpec(memory_space=pl.ANY),
            row, row, row,
        ],
        out_specs=pl.BlockSpec((rows, D_MODEL), lambda i: (i, 0)),
        out_shape=jax.ShapeDtypeStruct((t, D_MODEL), F32),
        scratch_shapes=[pltpu.VMEM((TOP_K, rows, D_MODEL), F32), pltpu.SemaphoreType.DMA(())],
        compiler_params=pltpu.CompilerParams(
            dimension_semantics=("arbitrary",), vmem_limit_bytes=VMEM_LIMIT),
        name="moe_combine",
    )(pos_flat, probs, x1, out_sorted, g2, lng, lnb)


def _tile(t, pref):
    return pref if t % pref == 0 else t


def kernel(x, c, w_ada, b_ada, w_in, conv_ssd_w, conv_ssd_b, dt_bias, a_log, d_skip, ssd_norm_w,
           w_ssd_out, conv_short_w, w_short_out, b_gate, w_o, ln1_g, ln1_b, w_router, b_router,
           w_gu, b_gu, w_down, b_down, ln2_g, ln2_b):
    batch, seq, d = x.shape
    assert batch == 1 and d == D_MODEL
    depth = w_in.shape[0]
    t = seq
    xt = x.reshape(t, d)

    tm_in = _tile(t, 1024)
    rows_ssd = _tile(t, 512)
    chunk = 128
    rows_post = _tile(t, 256)
    rows_moe = _tile(t, 256)
    tm_e = 256
    n_rows = t * TOP_K + N_EXPERTS * tm_e

    mods = _ada_mod(c, w_ada, b_ada)

    s0 = SSD_INNER
    s1 = s0 + SSD_INNER + 2 * SSD_GROUPS * SSD_STATE
    s2 = s1 + SSD_HEADS
    w_main = jnp.concatenate([w_in[:, :, :s1], w_in[:, :, s2:]], axis=-1).astype(BF16)
    w_dt = jnp.pad(w_in[:, :, s1:s2], ((0, 0), (0, 0), (0, LANES - SSD_HEADS)))
    w_r = jnp.pad(w_router, ((0, 0), (0, 0), (0, LANES - N_EXPERTS)))
    b_r = jnp.pad(b_router, ((0, 0), (0, LANES - N_EXPERTS)), constant_values=NEG_BIG)
    wa_bf = w_ssd_out.astype(BF16)
    wb_bf = w_short_out.astype(BF16)
    wo_bf = w_o.astype(BF16)

    for l in range(depth):
        m = mods[l]
        sh1, sc1, g1, sh2, sc2, g2 = [m[:, k * d:(k + 1) * d] for k in range(N_ADA)]
        u, dt_raw = _in_proj(xt, sh1, sc1, w_main[l], w_dt[l], tm_in, 1024)
        y = _ssd(u, dt_raw, conv_ssd_w[l], conv_ssd_b[l], dt_bias[l], a_log[l], d_skip[l],
                 rows_ssd, chunk)
        vecs = (ssd_norm_w[l][None, :], conv_short_w[l], b_gate[l][None, :], g1,
                ln1_g[l][None, :], ln1_b[l][None, :], sh2, sc2, b_r[l][None, :])
        x1, h2, idx, probs, rank, cnt = _post(
            y, u, xt, vecs, (wa_bf[l], wb_bf[l], wo_bf[l], w_r[l]), rows_post)

        counts = cnt[0, :N_EXPERTS]
        tiles_e = (counts + tm_e - 1) // tm_e
        tile_end = jnp.cumsum(tiles_e)
        starts = (tile_end - tiles_e) * tm_e
        n_active = tile_end[-1:].astype(jnp.int32)
        n_tiles = n_rows // tm_e
        tile_ids = jnp.minimum(jnp.arange(n_tiles, dtype=jnp.int32), n_active[0] - 1)
        tile_expert = jnp.minimum(
            jnp.searchsorted(tile_end, tile_ids, side="right"), N_EXPERTS - 1).astype(jnp.int32)
        pos = starts[idx[:, :TOP_K]] + rank[:, :TOP_K]
        pos_flat = pos.reshape(-1).astype(jnp.int32)

        xs = _dispatch(pos_flat, h2, n_rows, rows_moe)
        out_sorted = _experts(tile_expert, n_active, xs, w_gu, b_gu, w_down, b_down, l, tm_e)
        xt = _combine(pos_flat, probs, x1, out_sorted, g2, ln2_g[l][None, :], ln2_b[l][None, :],
                      rows_moe)

    return xt.reshape(batch, seq, d)
```

```python
import functools

import jax
import jax.numpy as jnp
from jax import lax
from jax.experimental import pallas as pl
from jax.experimental.pallas import tpu as pltpu

F32 = jnp.float32
BF16 = jnp.bfloat16

D_MODEL = 1024
SSD_INNER = 2048
SSD_HEADS = 32
SSD_HEADDIM = 64
SSD_GROUPS = 4
SSD_STATE = 128
SSD_CONV = 4
SC_CONV = 3
N_EXPERTS = 32
TOP_K = 4
EXPERT_FF = 1024
SWIGLU_LIMIT = 7.0
SWIGLU_ALPHA = 1.702
DEPTH = 4
DN_ALPHA = (2.0 * DEPTH) ** 0.25
LN_EPS = 1e-5
RMS_EPS = 1e-5
N_ADA = 6

LANES = 128
SUBLANES = 8
U_MAIN = 10240
COL_Z = 0
COL_XS = 2048
COL_B = 4096
COL_C = 4608
COL_SCB = 5120
COL_SCC = 6144
COL_SCX = 7168
COL_GATE = 8192
NEG_BIG = -1e30

VMEM_LIMIT = 56 * 1024 * 1024


def _sigmoid(v):
    return 1.0 / (1.0 + jnp.exp(-v))


def _softplus(v):
    return jnp.maximum(v, 0.0) + jnp.log(1.0 + jnp.exp(-jnp.abs(v)))


def _layer_norm(v):
    mu = jnp.mean(v, axis=-1, keepdims=True)
    vc = v - mu
    var = jnp.mean(vc * vc, axis=-1, keepdims=True)
    return vc * lax.rsqrt(var + LN_EPS)


def _split3(v):
    hi = v.astype(BF16)
    r1 = v - hi.astype(F32)
    mid = r1.astype(BF16)
    lo = (r1 - mid.astype(F32)).astype(BF16)
    return hi, mid, lo


def _dot(a, b):
    return jnp.dot(a, b, preferred_element_type=F32)


def _dot_exact_lhs(a_bf16, v):
    hi, mid, lo = _split3(v)
    return _dot(a_bf16, hi) + _dot(a_bf16, mid) + _dot(a_bf16, lo)


def _ada_kernel(c_ref, w_ref, b_ref, o_ref):
    c = c_ref[...]
    s = c * _sigmoid(c)
    o_ref[0] = jnp.sum(w_ref[0] * s, axis=0, keepdims=True) + b_ref[0]


def _ada_mod(c, w_ada, b_ada):
    depth, d, n = w_ada.shape
    tn = 1024
    return pl.pallas_call(
        _ada_kernel,
        grid=(depth, n // tn),
        in_specs=[
            pl.BlockSpec((d, 1), lambda l, j: (0, 0)),
            pl.BlockSpec((1, d, tn), lambda l, j: (l, 0, j)),
            pl.BlockSpec((1, 1, tn), lambda l, j: (l, 0, j)),
        ],
        out_specs=pl.BlockSpec((1, 1, tn), lambda l, j: (l, 0, j)),
        out_shape=jax.ShapeDtypeStruct((depth, 1, n), F32),
        compiler_params=pltpu.CompilerParams(
            dimension_semantics=("arbitrary", "arbitrary"), vmem_limit_bytes=VMEM_LIMIT),
        name="ada_mod",
    )(c.reshape(d, 1), w_ada, b_ada.reshape(depth, 1, n))


def _inproj_kernel(x_ref, sh_ref, sc_ref, w_ref, wdt_ref, u_ref, dt_ref, h_scr):
    @pl.when(pl.program_id(1) == 0)
    def _():
        h = _layer_norm(x_ref[...]) * (1.0 + sc_ref[...]) + sh_ref[...]
        h_scr[...] = h.astype(BF16)
        h3 = _split3(h)
        w3 = _split3(wdt_ref[...])
        dt_ref[...] = (_dot(h3[0], w3[0]) + _dot(h3[0], w3[1]) + _dot(h3[1], w3[0])
                       + _dot(h3[1], w3[1]) + _dot(h3[0], w3[2]) + _dot(h3[2], w3[0]))

    u_ref[...] = _dot(h_scr[...], w_ref[...]).astype(BF16)


def _in_proj(x, sh, sc, w_main, w_dt, tm, tn):
    t, d = x.shape
    n = w_main.shape[1]
    return pl.pallas_call(
        _inproj_kernel,
        grid=(t // tm, n // tn),
        in_specs=[
            pl.BlockSpec((tm, d), lambda i, j: (i, 0)),
            pl.BlockSpec((1, d), lambda i, j: (0, 0)),
            pl.BlockSpec((1, d), lambda i, j: (0, 0)),
            pl.BlockSpec((d, tn), lambda i, j: (0, j)),
            pl.BlockSpec((d, LANES), lambda i, j: (0, 0)),
        ],
        out_specs=[
            pl.BlockSpec((tm, tn), lambda i, j: (i, j)),
            pl.BlockSpec((tm, LANES), lambda i, j: (i, 0)),
        ],
        out_shape=[
            jax.ShapeDtypeStruct((t, n), BF16),
            jax.ShapeDtypeStruct((t, LANES), F32),
        ],
        scratch_shapes=[pltpu.VMEM((tm, d), BF16)],
        compiler_params=pltpu.CompilerParams(
            dimension_semantics=("arbitrary", "arbitrary"), vmem_limit_bytes=VMEM_LIMIT),
        name="in_proj",
    )(x, sh, sc, w_main, w_dt)


def _ssd_kernel(xs_ref, b_ref, c_ref, dt_ref, cwx_ref, cwb_ref, cwc_ref, cbx_ref, cbb_ref,
                cbc_ref, dtb_ref, alog_ref, dskip_ref, y_ref,
                xbuf, bbuf, cbuf, xcs, bcs, ccs, state, *, chunk, rows):
    @pl.when(pl.program_id(0) == 0)
    def _():
        xbuf[0:SUBLANES, :] = jnp.zeros((SUBLANES, xbuf.shape[1]), F32)
        bbuf[0:SUBLANES, :] = jnp.zeros((SUBLANES, bbuf.shape[1]), F32)
        cbuf[0:SUBLANES, :] = jnp.zeros((SUBLANES, cbuf.shape[1]), F32)
        state[...] = jnp.zeros(state.shape, F32)

    def conv_silu(in_ref, buf, w_ref, bias_ref, out_scr):
        buf[SUBLANES:SUBLANES + rows, :] = in_ref[...].astype(F32)
        acc = bias_ref[...]
        for j in range(SSD_CONV):
            off = SUBLANES - (SSD_CONV - 1) + j
            acc = acc + w_ref[j:j + 1, :] * buf[off:off + rows, :]
        out_scr[...] = acc * _sigmoid(acc)
        buf[0:SUBLANES, :] = buf[rows:rows + SUBLANES, :]

    conv_silu(xs_ref, xbuf, cwx_ref, cbx_ref, xcs)
    conv_silu(b_ref, bbuf, cwb_ref, cbb_ref, bcs)
    conv_silu(c_ref, cbuf, cwc_ref, cbc_ref, ccs)

    li = lax.broadcasted_iota(jnp.int32, (chunk, chunk), 0)
    si = lax.broadcasted_iota(jnp.int32, (chunk, chunk), 1)
    causal = li >= si
    tri = jnp.where(causal, 1.0, 0.0).astype(BF16)
    first_half = lax.broadcasted_iota(jnp.int32, (1, LANES), 1) < SSD_HEADDIM
    a_row = -jnp.exp(alog_ref[...])
    heads_per_group = SSD_HEADS // SSD_GROUPS

    def chunk_body(ci, carry):
        r0 = pl.multiple_of(ci * chunk, chunk)
        dt = _softplus(dt_ref[pl.ds(r0, chunk), :] + dtb_ref[...])
        la = dt * a_row
        acum = _dot_exact_lhs(tri, la)
        exp_acum = jnp.exp(acum)
        acum_t = acum.T
        dt_t = dt.T
        last_t = acum_t[:, chunk - 1:chunk]
        w_t = jnp.exp(last_t - acum_t) * dt_t
        cdec_t = jnp.exp(last_t)

        cb = []
        bt = []
        cg = []
        for g in range(SSD_GROUPS):
            bg = bcs[pl.ds(r0, chunk), g * SSD_STATE:(g + 1) * SSD_STATE]
            cgv = ccs[pl.ds(r0, chunk), g * SSD_STATE:(g + 1) * SSD_STATE]
            btg = bg.T
            bt.append(btg)
            cg.append(cgv)
            cb.append(_dot(cgv.astype(BF16), btg.astype(BF16)))

        for pr in range(SSD_HEADS // 2):
            lo = pr * LANES
            xs_pair = xcs[pl.ds(r0, chunk), lo:lo + LANES]
            xs_bf = xs_pair.astype(BF16)
            prev = state[:, lo:lo + LANES]
            lhs_m, lhs_c, lhs_b, cd = [], [], [], []
            for h in (2 * pr, 2 * pr + 1):
                g = h // heads_per_group
                col = acum[:, h:h + 1]
                row = acum_t[h:h + 1, :]
                dec = jnp.exp(jnp.where(causal, col - row, NEG_BIG))
                lhs_m.append((cb[g] * dec * dt_t[h:h + 1, :]).astype(BF16))
                lhs_c.append((cg[g] * exp_acum[:, h:h + 1]).astype(BF16))
                lhs_b.append((bt[g] * w_t[h:h + 1, :]).astype(BF16))
                cd.append(cdec_t[h:h + 1, :])
            out = (_dot(jnp.concatenate(lhs_m, axis=0), xs_bf)
                   + _dot(jnp.concatenate(lhs_c, axis=0), prev.astype(BF16)))
            y_pair = jnp.where(first_half, out[0:chunk, :], out[chunk:2 * chunk, :])
            y_ref[pl.ds(r0, chunk), lo:lo + LANES] = (
                y_pair + xs_pair * dskip_ref[:, lo:lo + LANES]).astype(y_ref.dtype)
            st = _dot(jnp.concatenate(lhs_b, axis=0), xs_bf)
            cd_pair = jnp.where(first_half, cd[0], cd[1])
            state[:, lo:lo + LANES] = prev * cd_pair + jnp.where(
                first_half, st[0:SSD_STATE, :], st[SSD_STATE:2 * SSD_STATE, :])
        return carry

    lax.fori_loop(0, rows // chunk, chunk_body, 0)


def _ssd(u, dt_raw, cw, cb, dt_bias, a_log, d_skip, rows, chunk):
    t = u.shape[0]
    gn = SSD_GROUPS * SSD_STATE
    pad = LANES - SSD_HEADS
    kern = functools.partial(_ssd_kernel, chunk=chunk, rows=rows)
    full = lambda shape: pl.BlockSpec(shape, lambda i: (0, 0))
    return pl.pallas_call(
        kern,
        grid=(t // rows,),
        in_specs=[
            pl.BlockSpec((rows, SSD_INNER), lambda i: (i, COL_XS // SSD_INNER)),
            pl.BlockSpec((rows, gn), lambda i: (i, COL_B // gn)),
            pl.BlockSpec((rows, gn), lambda i: (i, COL_C // gn)),
            pl.BlockSpec((rows, LANES), lambda i: (i, 0)),
            full((SSD_CONV, SSD_INNER)), full((SSD_CONV, gn)), full((SSD_CONV, gn)),
            full((1, SSD_INNER)), full((1, gn)), full((1, gn)),
            full((1, LANES)), full((1, LANES)), full((1, SSD_INNER)),
        ],
        out_specs=pl.BlockSpec((rows, SSD_INNER), lambda i: (i, 0)),
        out_shape=jax.ShapeDtypeStruct((t, SSD_INNER), BF16),
        scratch_shapes=[
            pltpu.VMEM((rows + SUBLANES, SSD_INNER), F32),
            pltpu.VMEM((rows + SUBLANES, gn), F32),
            pltpu.VMEM((rows + SUBLANES, gn), F32),
            pltpu.VMEM((rows, SSD_INNER), F32),
            pltpu.VMEM((rows, gn), F32),
            pltpu.VMEM((rows, gn), F32),
            pltpu.VMEM((SSD_STATE, SSD_INNER), F32),
        ],
        compiler_params=pltpu.CompilerParams(
            dimension_semantics=("arbitrary",), vmem_limit_bytes=VMEM_LIMIT),
        name="ssd",
    )(u, u, u, dt_raw,
      cw[:, :SSD_INNER], cw[:, SSD_INNER:SSD_INNER + gn], cw[:, SSD_INNER + gn:],
      cb[None, :SSD_INNER], cb[None, SSD_INNER:SSD_INNER + gn], cb[None, SSD_INNER + gn:],
      jnp.pad(dt_bias, (0, pad))[None, :], jnp.pad(a_log, (0, pad))[None, :],
      jnp.repeat(d_skip, SSD_HEADDIM)[None, :])


def _post_kernel(y_ref, z_ref, scb_ref, scc_ref, scx_ref, gate_ref, x_ref,
                 nw_ref, wa_ref, csw_ref, wb_ref, bg_ref, wo_ref, g1_ref, lng_ref, lnb_ref,
                 sh2_ref, sc2_ref, wr_ref, br_ref,
                 x1_ref, h2_ref, prob_ref, cst_ref, sidt_ref, runtab_ref, cnt_ref,
                 sbuf, run, *, rows):
    @pl.when(pl.program_id(0) == 0)
    def _():
        sbuf[0:SUBLANES, :] = jnp.zeros((SUBLANES, sbuf.shape[1]), F32)
        run[...] = jnp.zeros(run.shape, F32)

    z = z_ref[...].astype(F32)
    yg = y_ref[...].astype(F32) * (z * _sigmoid(z))
    ms = jnp.mean(yg * yg, axis=-1, keepdims=True)
    yn = yg * lax.rsqrt(ms + RMS_EPS) * nw_ref[...]
    u_a = _dot(yn.astype(BF16), wa_ref[...])

    sbuf[SUBLANES:SUBLANES + rows, :] = scc_ref[...].astype(F32) * scx_ref[...].astype(F32)
    v = jnp.zeros((rows, D_MODEL), F32)
    for j in range(SC_CONV):
        off = SUBLANES - (SC_CONV - 1) + j
        v = v + csw_ref[j:j + 1, :] * sbuf[off:off + rows, :]
    sbuf[0:SUBLANES, :] = sbuf[rows:rows + SUBLANES, :]
    u_b = _dot((scb_ref[...].astype(F32) * v).astype(BF16), wb_ref[...])

    gl = gate_ref[...].astype(F32) + bg_ref[...]
    merged = _sigmoid(gl[:, :D_MODEL]) * u_a + _sigmoid(gl[:, D_MODEL:]) * u_b
    mix = _dot(merged.astype(BF16), wo_ref[...])
    x1 = _layer_norm(DN_ALPHA * x_ref[...] + g1_ref[...] * mix) * lng_ref[...] + lnb_ref[...]
    x1_ref[...] = x1
    h2 = _layer_norm(x1) * (1.0 + sc2_ref[...]) + sh2_ref[...]
    h2_ref[...] = h2

    h3 = _split3(h2)
    w3 = _split3(wr_ref[...])
    logits = (_dot(h3[0], w3[0]) + _dot(h3[0], w3[1]) + _dot(h3[1], w3[0])
              + _dot(h3[1], w3[1]) + _dot(h3[0], w3[2]) + _dot(h3[2], w3[0])) + br_ref[...]
    lane = lax.broadcasted_iota(jnp.int32, (rows, LANES), 1).astype(F32)
    work = logits
    onehots, vals = [], []
    for k in range(TOP_K):
        m = jnp.max(work, axis=-1, keepdims=True)
        ik = jnp.min(jnp.where(work == m, lane, float(LANES)), axis=-1, keepdims=True)
        oh = lane == ik
        onehots.append(oh)
        vals.append(m)
        work = jnp.where(oh, -jnp.inf, work)
    es = [jnp.exp(vk - vals[0]) for vk in vals]
    denom = es[0] + es[1] + es[2] + es[3]
    prob_out = jnp.zeros((rows, LANES), F32)
    for k in range(TOP_K):
        prob_out = jnp.where(lane == float(k), es[k] / denom, prob_out)

    sel = jnp.zeros((rows, LANES), F32)
    kk = jnp.zeros((rows, LANES), F32)
    for k, oh in enumerate(onehots):
        sel = sel + jnp.where(oh, 1.0, 0.0)
        kk = kk + jnp.where(oh, float(k), 0.0)
    ri = lax.broadcasted_iota(jnp.int32, (rows, rows), 0)
    rj = lax.broadcasted_iota(jnp.int32, (rows, rows), 1)
    strict = jnp.where(ri > rj, 1.0, 0.0).astype(BF16)
    base = _dot(strict, sel.astype(BF16)) + run[...]
    tok = (pl.program_id(0) * rows
           + lax.broadcasted_iota(jnp.int32, (rows, LANES), 0)).astype(F32)
    picked = sel > 0.0
    cs = jnp.where(picked, base + 1.0, 0.0)
    sid = jnp.where(picked, tok * float(TOP_K) + kk, 0.0)
    cst_ref[...] = cs.T[:N_EXPERTS, :]
    sidt_ref[...] = sid.T[:N_EXPERTS, :]
    runtab_ref[0] = run[...].astype(jnp.int32)
    run[...] = run[...] + jnp.sum(sel, axis=0, keepdims=True)
    prob_ref[...] = prob_out
    cnt_ref[...] = run[...].astype(jnp.int32)


def _post(y, u, x, vecs, mats, rows):
    t = x.shape[0]
    kern = functools.partial(_post_kernel, rows=rows)
    row = lambda w: pl.BlockSpec((1, w), lambda i: (0, 0))
    mat = lambda a, b: pl.BlockSpec((a, b), lambda i: (0, 0))
    ublk = lambda w, col: pl.BlockSpec((rows, w), lambda i: (i, col // w))
    nw, csw, bg, g1, lng, lnb, sh2, sc2, br = vecs
    wa, wb, wo, wr = mats
    tok = pl.BlockSpec((rows, LANES), lambda i: (i, 0))
    return pl.pallas_call(
        kern,
        grid=(t // rows,),
        in_specs=[
            pl.BlockSpec((rows, SSD_INNER), lambda i: (i, 0)),
            ublk(SSD_INNER, COL_Z), ublk(D_MODEL, COL_SCB), ublk(D_MODEL, COL_SCC),
            ublk(D_MODEL, COL_SCX), ublk(2 * D_MODEL, COL_GATE),
            pl.BlockSpec((rows, D_MODEL), lambda i: (i, 0)),
            row(SSD_INNER), mat(SSD_INNER, D_MODEL), mat(SC_CONV, D_MODEL), mat(D_MODEL, D_MODEL),
            row(2 * D_MODEL), mat(D_MODEL, D_MODEL), row(D_MODEL), row(D_MODEL), row(D_MODEL),
            row(D_MODEL), row(D_MODEL), mat(D_MODEL, LANES), row(LANES),
        ],
        out_specs=[
            pl.BlockSpec((rows, D_MODEL), lambda i: (i, 0)),
            pl.BlockSpec((rows, D_MODEL), lambda i: (i, 0)),
            tok,
            pl.BlockSpec((N_EXPERTS, rows), lambda i: (0, i)),
            pl.BlockSpec((N_EXPERTS, rows), lambda i: (0, i)),
            pl.BlockSpec((1, 1, LANES), lambda i: (i, 0, 0)),
            pl.BlockSpec((1, LANES), lambda i: (0, 0)),
        ],
        out_shape=[
            jax.ShapeDtypeStruct((t, D_MODEL), F32),
            jax.ShapeDtypeStruct((t, D_MODEL), F32),
            jax.ShapeDtypeStruct((t, LANES), F32),
            jax.ShapeDtypeStruct((N_EXPERTS, t), F32),
            jax.ShapeDtypeStruct((N_EXPERTS, t), F32),
            jax.ShapeDtypeStruct((t // rows, 1, LANES), jnp.int32),
            jax.ShapeDtypeStruct((1, LANES), jnp.int32),
        ],
        scratch_shapes=[
            pltpu.VMEM((rows + SUBLANES, D_MODEL), F32),
            pltpu.VMEM((1, LANES), F32),
        ],
        compiler_params=pltpu.CompilerParams(
            dimension_semantics=("arbitrary",), vmem_limit_bytes=VMEM_LIMIT),
        name="post_mix",
    )(y, u, u, u, u, u, x, nw, wa, csw, wb, bg, wo, g1, lng, lnb, sh2, sc2, wr, br)


def _invert_kernel(vt_ref, ve_ref, nv_ref, gs_ref, run_ref, cst_ref, sidt_ref, o_ref, ptr,
                   *, tile, tok_tile, n_tok_tiles):
    v = pl.program_id(0)
    i = vt_ref[v]
    e = ve_ref[v]
    vp = jnp.maximum(v - 1, 0)
    new_tile = jnp.logical_or(v == 0, vt_ref[vp] != i)
    new_exp = jnp.logical_or(v == 0, ve_ref[vp] != e)

    @pl.when(new_exp)
    def _():
        ptr[0] = 0

    @pl.when(new_tile)
    def _():
        o_ref[...] = jnp.zeros(o_ref.shape, jnp.int32)

    @pl.when(v < nv_ref[0])
    def _():
        g0 = gs_ref[e]
        row0 = i * tile
        ra = jnp.maximum(g0, row0) - g0
        rb = jnp.minimum(gs_ref[e + 1], row0 + tile) - g0
        b_lo = lax.while_loop(lambda b: run_ref[(b + 1) * N_EXPERTS + e] <= ra,
                              lambda b: b + 1, ptr[0])
        ptr[0] = b_lo
        b_hi = lax.while_loop(
            lambda b: jnp.logical_and(b < n_tok_tiles, run_ref[b * N_EXPERTS + e] < rb),
            lambda b: b + 1, b_lo)
        want = (row0 - g0 + 1
                + lax.broadcasted_iota(jnp.int32, (tile, 1), 0)).astype(F32)
        sub = lax.broadcasted_iota(jnp.int32, (SUBLANES, tok_tile), 0)

        def body(b, acc):
            c0 = pl.multiple_of(b * tok_tile, tok_tile)
            cs_row = cst_ref[pl.ds(e, 1), pl.ds(c0, tok_tile)]
            sid_row = sidt_ref[pl.ds(e, 1), pl.ds(c0, tok_tile)]
            eq_t = jnp.where(cs_row == want, 1.0, 0.0).astype(BF16)
            hi = sid_row.astype(BF16).astype(F32)
            mid = (sid_row - hi).astype(BF16).astype(F32)
            lo = sid_row - hi - mid
            parts = jnp.where(sub == 0, hi, jnp.where(sub == 1, mid, jnp.where(sub == 2, lo, 0.0)))
            return acc + lax.dot_general(parts.astype(BF16), eq_t, (((1,), (1,)), ((), ())),
                                         preferred_element_type=F32)

        acc = lax.fori_loop(b_lo, b_hi, body, jnp.zeros((SUBLANES, tile), F32))
        contrib = acc[0:1, :] + acc[1:2, :] + acc[2:3, :]
        o_ref[0] = o_ref[0] + contrib.astype(jnp.int32)


def _invert(v_tile, v_exp, n_visits, gs, runflat, cst, sidt, n_tiles, tile, tok_tile):
    t = cst.shape[1]
    n_vis = v_tile.shape[0]
    kern = functools.partial(_invert_kernel, tile=tile, tok_tile=tok_tile,
                             n_tok_tiles=t // tok_tile)
    return pl.pallas_call(
        kern,
        grid_spec=pltpu.PrefetchScalarGridSpec(
            num_scalar_prefetch=5,
            grid=(n_vis,),
            in_specs=[
                pl.BlockSpec((N_EXPERTS, t), lambda v, *_: (0, 0)),
                pl.BlockSpec((N_EXPERTS, t), lambda v, *_: (0, 0)),
            ],
            out_specs=pl.BlockSpec((1, 1, tile), lambda v, vt, *_: (vt[v], 0, 0)),
            scratch_shapes=[pltpu.SMEM((1,), jnp.int32)],
        ),
        out_shape=jax.ShapeDtypeStruct((n_tiles, 1, tile), jnp.int32),
        compiler_params=pltpu.CompilerParams(
            dimension_semantics=("arbitrary",), vmem_limit_bytes=VMEM_LIMIT),
        name="moe_invert",
    )(v_tile, v_exp, n_visits, gs, runflat, cst, sidt)


N_FF_BLK = 4
FF_BLK = EXPERT_FF // N_FF_BLK


def _experts_kernel(vt_ref, ve_ref, nv_ref, gs_ref,
                    ord_prev, ord_cur, ord_next, h_hbm, wgu_ref, bgu_ref, wd_ref, bd_ref,
                    slots_hbm, xbuf, obuf, act, wgu_bf, wd_bf, gsem, ssem, *, tile):
    v = pl.program_id(0)
    nv = nv_ref[0]
    i = vt_ref[v]
    e = ve_ref[v]
    vp = jnp.maximum(v - 1, 0)
    active = v < nv
    new_tile = jnp.logical_or(v == 0, vt_ref[vp] != i)
    new_exp = jnp.logical_or(v == 0, ve_ref[vp] != e)
    slot = lax.rem(i, 2)
    other = 1 - slot

    def gather_row(order_ref, r, s):
        tok = lax.shift_right_logical(order_ref[r], 2)
        return pltpu.make_async_copy(h_hbm.at[pl.ds(tok, 1)], xbuf.at[s, pl.ds(r, 1)], gsem.at[s])

    def scatter_row(order_ref, r, s):
        return pltpu.make_async_copy(obuf.at[s, pl.ds(r, 1)],
                                     slots_hbm.at[pl.ds(order_ref[r], 1)], ssem.at[s])

    @pl.when(v == 0)
    def _():
        obuf[1] = jnp.zeros((tile, D_MODEL), F32)
        for r in range(tile):
            gather_row(ord_cur, r, 0).start()

    @pl.when(jnp.logical_and(active, new_exp))
    def _():
        wgu_bf[...] = wgu_ref[...].astype(BF16)
        wd_bf[...] = wd_ref[...].astype(BF16)

    rowpos = i * tile + lax.broadcasted_iota(jnp.int32, (tile, 1), 0)
    mine = jnp.logical_and(rowpos >= gs_ref[e], rowpos < gs_ref[e + 1])

    def compute(first, between):
        xb = xbuf[slot].astype(BF16)
        for cb in range(N_FF_BLK):
            c0 = cb * FF_BLK
            g = _dot(xb, wgu_bf[:, c0:c0 + FF_BLK]) + bgu_ref[:, c0:c0 + FF_BLK]
            u = (_dot(xb, wgu_bf[:, EXPERT_FF + c0:EXPERT_FF + c0 + FF_BLK])
                 + bgu_ref[:, EXPERT_FF + c0:EXPERT_FF + c0 + FF_BLK])
            g = jnp.minimum(g, SWIGLU_LIMIT)
            u = jnp.clip(u, -SWIGLU_LIMIT, SWIGLU_LIMIT)
            act[:, c0:c0 + FF_BLK] = ((u + 1.0) * g * _sigmoid(SWIGLU_ALPHA * g)).astype(BF16)
            between(cb)
        for ob in range(N_FF_BLK):
            c0 = ob * FF_BLK
            o = _dot(act[...], wd_bf[:, c0:c0 + FF_BLK]) + bd_ref[:, c0:c0 + FF_BLK]
            if first:
                obuf[slot, :, c0:c0 + FF_BLK] = jnp.where(mine, o, 0.0)
            else:
                obuf[slot, :, c0:c0 + FF_BLK] = jnp.where(mine, o, obuf[slot, :, c0:c0 + FF_BLK])
            between(N_FF_BLK + ob)

    per_stage = tile // (2 * N_FF_BLK)

    @pl.when(jnp.logical_and(active, new_tile))
    def _():
        for r in range(tile):
            gather_row(ord_cur, r, slot).wait()

        @pl.when(i >= 1)
        def _():
            for r in range(tile):
                scatter_row(ord_prev, r, slot).wait()

        def between(j):
            for r in range(j * per_stage, (j + 1) * per_stage):
                gather_row(ord_next, r, other).start()
                scatter_row(ord_prev, r, other).start()

        compute(True, between)

    @pl.when(jnp.logical_and(active, jnp.logical_not(new_tile)))
    def _():
        compute(False, lambda j: None)

    @pl.when(v == nv - 1)
    def _():
        for r in range(tile):
            scatter_row(ord_cur, r, slot).start()
        for r in range(tile):
            scatter_row(ord_prev, r, other).wait()
            scatter_row(ord_cur, r, slot).wait()
            gather_row(ord_next, r, other).wait()


def _experts(v_tile, v_exp, n_visits, gs, order_ext, h2, w_gu, b_gu, w_down, b_down, layer,
             n_slots, tile):
    n_vis = v_tile.shape[0]
    ff2 = 2 * EXPERT_FF
    kern = functools.partial(_experts_kernel, tile=tile)
    smem_blk = lambda off: pl.BlockSpec(
        (tile,), lambda v, vt, *_: (vt[v] + off,), memory_space=pltpu.SMEM)
    wspec = lambda a, b: pl.BlockSpec(
        (None, None, a, b), lambda v, vt, ve, *_: (layer, ve[v], 0, 0))
    return pl.pallas_call(
        kern,
        grid_spec=pltpu.PrefetchScalarGridSpec(
            num_scalar_prefetch=4,
            grid=(n_vis,),
            in_specs=[
                smem_blk(0), smem_blk(1), smem_blk(2),
                pl.BlockSpec(memory_space=pl.ANY),
                wspec(D_MODEL, ff2), wspec(1, ff2), wspec(EXPERT_FF, D_MODEL), wspec(1, D_MODEL),
            ],
            out_specs=pl.BlockSpec(memory_space=pl.ANY),
            scratch_shapes=[
                pltpu.VMEM((2, tile, D_MODEL), F32),
                pltpu.VMEM((2, tile, D_MODEL), F32),
                pltpu.VMEM((tile, EXPERT_FF), BF16),
                pltpu.VMEM((D_MODEL, ff2), BF16),
                pltpu.VMEM((EXPERT_FF, D_MODEL), BF16),
                pltpu.SemaphoreType.DMA((2,)),
                pltpu.SemaphoreType.DMA((2,)),
            ],
        ),
        out_shape=jax.ShapeDtypeStruct((n_slots, D_MODEL), F32),
        compiler_params=pltpu.CompilerParams(
            dimension_semantics=("arbitrary",), vmem_limit_bytes=VMEM_LIMIT),
        name="moe_experts",
    )(v_tile, v_exp, n_visits, gs, order_ext, order_ext, order_ext, h2,
      w_gu, b_gu[:, :, None, :], w_down, b_down[:, :, None, :])


def _combine_kernel(slots_ref, prob_ref, x1_ref, g2_ref, lng_ref, lnb_ref, x2_ref):
    prob = prob_ref[...]
    ffn = prob[:, 0:1] * slots_ref[:, 0:D_MODEL]
    for k in range(1, TOP_K):
        ffn = ffn + prob[:, k:k + 1] * slots_ref[:, k * D_MODEL:(k + 1) * D_MODEL]
    x2_ref[...] = (_layer_norm(DN_ALPHA * x1_ref[...] + g2_ref[...] * ffn) * lng_ref[...]
                   + lnb_ref[...])


def _combine(slots2d, probs, x1, g2, lng, lnb, rows):
    t = x1.shape[0]
    row = pl.BlockSpec((1, D_MODEL), lambda i: (0, 0))
    return pl.pallas_call(
        _combine_kernel,
        grid=(t // rows,),
        in_specs=[
            pl.BlockSpec((rows, TOP_K * D_MODEL), lambda i: (i, 0)),
            pl.BlockSpec((rows, LANES), lambda i: (i, 0)),
            pl.BlockSpec((rows, D_MODEL), lambda i: (i, 0)),
            row, row, row,
        ],
        out_specs=pl.BlockSpec((rows, D_MODEL), lambda i: (i, 0)),
        out_shape=jax.ShapeDtypeStruct((t, D_MODEL), F32),
        compiler_params=pltpu.CompilerParams(
            dimension_semantics=("arbitrary",), vmem_limit_bytes=VMEM_LIMIT),
        name="moe_combine",
    )(slots2d, probs, x1, g2, lng, lnb)


def _tile(t, pref):
    return pref if t % pref == 0 else t


def _visit_plan(counts, tile, n_vis):
    ge = jnp.cumsum(counts)
    gs = ge - counts
    t_lo = gs // tile
    nt = jnp.where(counts > 0, (ge - 1) // tile - t_lo + 1, 0)
    vend = jnp.cumsum(nt)
    vbase = vend - nt
    n_visits = vend[-1:]
    vc = jnp.minimum(jnp.arange(n_vis, dtype=jnp.int32), n_visits[0] - 1)
    v_exp = jnp.sum((vend[None, :] <= vc[:, None]).astype(jnp.int32), axis=1)
    onehot = (v_exp[:, None] == jnp.arange(N_EXPERTS, dtype=jnp.int32)[None, :]).astype(jnp.int32)
    v_tile = jnp.sum(onehot * (t_lo - vbase)[None, :], axis=1) + vc
    gs33 = jnp.concatenate([gs, ge[-1:]])
    return (v_tile.astype(jnp.int32), v_exp.astype(jnp.int32), n_visits.astype(jnp.int32),
            gs33.astype(jnp.int32))


def kernel(x, c, w_ada, b_ada, w_in, conv_ssd_w, conv_ssd_b, dt_bias, a_log, d_skip, ssd_norm_w,
           w_ssd_out, conv_short_w, w_short_out, b_gate, w_o, ln1_g, ln1_b, w_router, b_router,
           w_gu, b_gu, w_down, b_down, ln2_g, ln2_b):
    batch, seq, d = x.shape
    assert batch == 1 and d == D_MODEL
    depth = w_in.shape[0]
    t = seq
    xt = x.reshape(t, d)

    tm_in = _tile(t, 1024)
    rows_ssd = _tile(t, 512)
    chunk = 128
    rows_post = 256
    rows_moe = 256
    tile_e = 256
    assert t % rows_post == 0
    n_tiles = t * TOP_K // tile_e
    n_vis = n_tiles + N_EXPERTS
    n_slots = t * TOP_K + tile_e

    mods = _ada_mod(c, w_ada, b_ada)

    s0 = SSD_INNER
    s1 = s0 + SSD_INNER + 2 * SSD_GROUPS * SSD_STATE
    s2 = s1 + SSD_HEADS
    w_main = jnp.concatenate([w_in[:, :, :s1], w_in[:, :, s2:]], axis=-1).astype(BF16)
    w_dt = jnp.pad(w_in[:, :, s1:s2], ((0, 0), (0, 0), (0, LANES - SSD_HEADS)))
    w_r = jnp.pad(w_router, ((0, 0), (0, 0), (0, LANES - N_EXPERTS)))
    b_r = jnp.pad(b_router, ((0, 0), (0, LANES - N_EXPERTS)), constant_values=NEG_BIG)
    wa_bf = w_ssd_out.astype(BF16)
    wb_bf = w_short_out.astype(BF16)
    wo_bf = w_o.astype(BF16)
    warm = t * TOP_K + jnp.arange(tile_e, dtype=jnp.int32)
    cool = jnp.zeros((tile_e,), jnp.int32)

    for l in range(depth):
        m = mods[l]
        sh1, sc1, g1, sh2, sc2, g2 = [m[:, k * d:(k + 1) * d] for k in range(N_ADA)]
        u, dt_raw = _in_proj(xt, sh1, sc1, w_main[l], w_dt[l], tm_in, 1024)
        y = _ssd(u, dt_raw, conv_ssd_w[l], conv_ssd_b[l], dt_bias[l], a_log[l], d_skip[l],
                 rows_ssd, chunk)
        vecs = (ssd_norm_w[l][None, :], conv_short_w[l], b_gate[l][None, :], g1,
                ln1_g[l][None, :], ln1_b[l][None, :], sh2, sc2, b_r[l][None, :])
        x1, h2, probs, cst, sidt, runtab, cnt = _post(
            y, u, xt, vecs, (wa_bf[l], wb_bf[l], wo_bf[l], w_r[l]), rows_post)

        v_tile, v_exp, n_visits, gs = _visit_plan(cnt[0, :N_EXPERTS], tile_e, n_vis)
        runflat = jnp.concatenate([runtab[:, 0, :N_EXPERTS], cnt[:, :N_EXPERTS]]).reshape(-1)
        order = _invert(v_tile, v_exp, n_visits, gs, runflat, cst, sidt, n_tiles, tile_e,
                        rows_post)
        order_ext = jnp.concatenate([warm, order.reshape(-1), cool])
        slots = _experts(v_tile, v_exp, n_visits, gs, order_ext, h2, w_gu, b_gu, w_down, b_down,
                         l, n_slots, tile_e)
        xt = _combine(slots.reshape(n_slots // TOP_K, TOP_K * d), probs, x1, g2,
                      ln2_g[l][None, :], ln2_b[l][None, :], rows_moe)

    return xt.reshape(batch, seq, d)
```

```python
import functools

import jax
import jax.numpy as jnp
from jax import lax
from jax.experimental import pallas as pl
from jax.experimental.pallas import tpu as pltpu

F32 = jnp.float32
BF16 = jnp.bfloat16

D_MODEL = 1024
SSD_INNER = 2048
SSD_HEADS = 32
SSD_HEADDIM = 64
SSD_GROUPS = 4
SSD_STATE = 128
SSD_CONV = 4
SC_CONV = 3
N_EXPERTS = 32
TOP_K = 4
EXPERT_FF = 1024
SWIGLU_LIMIT = 7.0
SWIGLU_ALPHA = 1.702
DEPTH = 4
DN_ALPHA = (2.0 * DEPTH) ** 0.25
LN_EPS = 1e-5
RMS_EPS = 1e-5
N_ADA = 6

LANES = 128
SUBLANES = 8
U_MAIN = 10240
COL_Z = 0
COL_XS = 2048
COL_B = 4096
COL_C = 4608
COL_SCB = 5120
COL_SCC = 6144
COL_SCX = 7168
COL_GATE = 8192
NEG_BIG = -1e30

VMEM_LIMIT = 56 * 1024 * 1024


def _sigmoid(v):
    return 1.0 / (1.0 + jnp.exp(-v))


def _softplus(v):
    return jnp.maximum(v, 0.0) + jnp.log(1.0 + jnp.exp(-jnp.abs(v)))


def _layer_norm(v):
    mu = jnp.mean(v, axis=-1, keepdims=True)
    vc = v - mu
    var = jnp.mean(vc * vc, axis=-1, keepdims=True)
    return vc * lax.rsqrt(var + LN_EPS)


def _split3(v):
    hi = v.astype(BF16)
    r1 = v - hi.astype(F32)
    mid = r1.astype(BF16)
    lo = (r1 - mid.astype(F32)).astype(BF16)
    return hi, mid, lo


def _dot(a, b):
    return jnp.dot(a, b, preferred_element_type=F32)


def _pack_bf16_pair(lo, hi):
    lo_bits = pltpu.bitcast(lo.astype(BF16).astype(F32), jnp.uint32)
    hi_bits = pltpu.bitcast(hi.astype(BF16).astype(F32), jnp.uint32)
    return lax.shift_right_logical(lo_bits, jnp.uint32(16)) | (hi_bits & jnp.uint32(0xFFFF0000))


def _unpack_bf16_pair(w):
    lo = pltpu.bitcast(lax.shift_left(w, jnp.uint32(16)), F32)
    hi = pltpu.bitcast(w & jnp.uint32(0xFFFF0000), F32)
    return lo, hi


def _dot_split(a, b):
    a_hi = a.astype(BF16)
    a_mid = (a - a_hi.astype(F32)).astype(BF16)
    b_hi = b.astype(BF16)
    b_mid = (b - b_hi.astype(F32)).astype(BF16)
    return _dot(a_hi, b_hi) + _dot(a_hi, b_mid) + _dot(a_mid, b_hi)


def _dot_exact_lhs(a_bf16, v):
    hi, mid, lo = _split3(v)
    return _dot(a_bf16, hi) + _dot(a_bf16, mid) + _dot(a_bf16, lo)


def _ada_kernel(c_ref, w_ref, b_ref, o_ref):
    c = c_ref[...]
    s = c * _sigmoid(c)
    o_ref[0] = jnp.sum(w_ref[0] * s, axis=0, keepdims=True) + b_ref[0]


def _ada_mod(c, w_ada, b_ada):
    depth, d, n = w_ada.shape
    tn = 1024
    return pl.pallas_call(
        _ada_kernel,
        grid=(depth, n // tn),
        in_specs=[
            pl.BlockSpec((d, 1), lambda l, j: (0, 0)),
            pl.BlockSpec((1, d, tn), lambda l, j: (l, 0, j)),
            pl.BlockSpec((1, 1, tn), lambda l, j: (l, 0, j)),
        ],
        out_specs=pl.BlockSpec((1, 1, tn), lambda l, j: (l, 0, j)),
        out_shape=jax.ShapeDtypeStruct((depth, 1, n), F32),
        compiler_params=pltpu.CompilerParams(
            dimension_semantics=("arbitrary", "arbitrary"), vmem_limit_bytes=VMEM_LIMIT),
        name="ada_mod",
    )(c.reshape(d, 1), w_ada, b_ada.reshape(depth, 1, n))


def _inproj_kernel(x_ref, sh_ref, sc_ref, w_ref, wdt_ref, u_ref, dt_ref, h_scr):
    @pl.when(pl.program_id(1) == 0)
    def _():
        h = _layer_norm(x_ref[...]) * (1.0 + sc_ref[...]) + sh_ref[...]
        h_scr[...] = h.astype(BF16)
        dt_ref[...] = _dot_split(h, wdt_ref[...])

    u_ref[...] = _dot(h_scr[...], w_ref[...]).astype(BF16)


def _in_proj(x, sh, sc, w_main, w_dt, tm, tn):
    t, d = x.shape
    n = w_main.shape[1]
    return pl.pallas_call(
        _inproj_kernel,
        grid=(t // tm, n // tn),
        in_specs=[
            pl.BlockSpec((tm, d), lambda i, j: (i, 0)),
            pl.BlockSpec((1, d), lambda i, j: (0, 0)),
            pl.BlockSpec((1, d), lambda i, j: (0, 0)),
            pl.BlockSpec((d, tn), lambda i, j: (0, j)),
            pl.BlockSpec((d, LANES), lambda i, j: (0, 0)),
        ],
        out_specs=[
            pl.BlockSpec((tm, tn), lambda i, j: (i, j)),
            pl.BlockSpec((tm, LANES), lambda i, j: (i, 0)),
        ],
        out_shape=[
            jax.ShapeDtypeStruct((t, n), BF16),
            jax.ShapeDtypeStruct((t, LANES), F32),
        ],
        scratch_shapes=[pltpu.VMEM((tm, d), BF16)],
        compiler_params=pltpu.CompilerParams(
            dimension_semantics=("arbitrary", "arbitrary"), vmem_limit_bytes=VMEM_LIMIT),
        name="in_proj",
    )(x, sh, sc, w_main, w_dt)


def _ssd_kernel(xs_ref, b_ref, c_ref, dt_ref, cwx_ref, cwb_ref, cwc_ref, cbx_ref, cbb_ref,
                cbc_ref, dtb_ref, alog_ref, dskip_ref, y_ref,
                xbuf, bbuf, cbuf, xcs, bcs, ccs, state, *, chunk, rows):
    @pl.when(pl.program_id(0) == 0)
    def _():
        xbuf[0:SUBLANES, :] = jnp.zeros((SUBLANES, xbuf.shape[1]), F32)
        bbuf[0:SUBLANES, :] = jnp.zeros((SUBLANES, bbuf.shape[1]), F32)
        cbuf[0:SUBLANES, :] = jnp.zeros((SUBLANES, cbuf.shape[1]), F32)
        state[...] = jnp.zeros(state.shape, F32)

    def conv_silu(in_ref, buf, w_ref, bias_ref, out_scr):
        buf[SUBLANES:SUBLANES + rows, :] = in_ref[...].astype(F32)
        acc = bias_ref[...]
        for j in range(SSD_CONV):
            off = SUBLANES - (SSD_CONV - 1) + j
            acc = acc + w_ref[j:j + 1, :] * buf[off:off + rows, :]
        out_scr[...] = acc * _sigmoid(acc)
        buf[0:SUBLANES, :] = buf[rows:rows + SUBLANES, :]

    conv_silu(xs_ref, xbuf, cwx_ref, cbx_ref, xcs)
    conv_silu(b_ref, bbuf, cwb_ref, cbb_ref, bcs)
    conv_silu(c_ref, cbuf, cwc_ref, cbc_ref, ccs)

    li = lax.broadcasted_iota(jnp.int32, (chunk, chunk), 0)
    si = lax.broadcasted_iota(jnp.int32, (chunk, chunk), 1)
    causal = li >= si
    tri = jnp.where(causal, 1.0, 0.0).astype(BF16)
    first_half = lax.broadcasted_iota(jnp.int32, (1, LANES), 1) < SSD_HEADDIM
    a_row = -jnp.exp(alog_ref[...])
    heads_per_group = SSD_HEADS // SSD_GROUPS

    def chunk_body(ci, carry):
        r0 = pl.multiple_of(ci * chunk, chunk)
        dt = _softplus(dt_ref[pl.ds(r0, chunk), :] + dtb_ref[...])
        la = dt * a_row
        acum = _dot_exact_lhs(tri, la)
        exp_acum = jnp.exp(acum)
        acum_t = acum.T
        dt_t = dt.T
        last_t = acum_t[:, chunk - 1:chunk]
        w_t = jnp.exp(last_t - acum_t) * dt_t
        cdec_t = jnp.exp(last_t)

        cb = []
        bt = []
        cg = []
        for g in range(SSD_GROUPS):
            bg = bcs[pl.ds(r0, chunk), g * SSD_STATE:(g + 1) * SSD_STATE]
            cgv = ccs[pl.ds(r0, chunk), g * SSD_STATE:(g + 1) * SSD_STATE]
            btg = bg.T
            bt.append(btg)
            cg.append(cgv)
            cb.append(_dot(cgv.astype(BF16), btg.astype(BF16)))

        for pr in range(SSD_HEADS // 2):
            lo = pr * LANES
            xs_pair = xcs[pl.ds(r0, chunk), lo:lo + LANES]
            xs_bf = xs_pair.astype(BF16)
            prev = state[:, lo:lo + LANES]
            lhs_m, lhs_c, lhs_b, cd = [], [], [], []
            for h in (2 * pr, 2 * pr + 1):
                g = h // heads_per_group
                col = acum[:, h:h + 1]
                row = acum_t[h:h + 1, :]
                dec = jnp.exp(jnp.where(causal, col - row, NEG_BIG))
                lhs_m.append((cb[g] * dec * dt_t[h:h + 1, :]).astype(BF16))
                lhs_c.append((cg[g] * exp_acum[:, h:h + 1]).astype(BF16))
                lhs_b.append((bt[g] * w_t[h:h + 1, :]).astype(BF16))
                cd.append(cdec_t[h:h + 1, :])
            out = (_dot(jnp.concatenate(lhs_m, axis=0), xs_bf)
                   + _dot(jnp.concatenate(lhs_c, axis=0), prev.astype(BF16)))
            y_pair = jnp.where(first_half, out[0:chunk, :], out[chunk:2 * chunk, :])
            y_ref[pl.ds(r0, chunk), lo:lo + LANES] = (
                y_pair + xs_pair * dskip_ref[:, lo:lo + LANES]).astype(y_ref.dtype)
            st = _dot(jnp.concatenate(lhs_b, axis=0), xs_bf)
            cd_pair = jnp.where(first_half, cd[0], cd[1])
            state[:, lo:lo + LANES] = prev * cd_pair + jnp.where(
                first_half, st[0:SSD_STATE, :], st[SSD_STATE:2 * SSD_STATE, :])
        return carry

    lax.fori_loop(0, rows // chunk, chunk_body, 0)


def _ssd(u, dt_raw, cw, cb, dt_bias, a_log, d_skip, rows, chunk):
    t = u.shape[0]
    gn = SSD_GROUPS * SSD_STATE
    pad = LANES - SSD_HEADS
    kern = functools.partial(_ssd_kernel, chunk=chunk, rows=rows)
    full = lambda shape: pl.BlockSpec(shape, lambda i: (0, 0))
    return pl.pallas_call(
        kern,
        grid=(t // rows,),
        in_specs=[
            pl.BlockSpec((rows, SSD_INNER), lambda i: (i, COL_XS // SSD_INNER)),
            pl.BlockSpec((rows, gn), lambda i: (i, COL_B // gn)),
            pl.BlockSpec((rows, gn), lambda i: (i, COL_C // gn)),
            pl.BlockSpec((rows, LANES), lambda i: (i, 0)),
            full((SSD_CONV, SSD_INNER)), full((SSD_CONV, gn)), full((SSD_CONV, gn)),
            full((1, SSD_INNER)), full((1, gn)), full((1, gn)),
            full((1, LANES)), full((1, LANES)), full((1, SSD_INNER)),
        ],
        out_specs=pl.BlockSpec((rows, SSD_INNER), lambda i: (i, 0)),
        out_shape=jax.ShapeDtypeStruct((t, SSD_INNER), BF16),
        scratch_shapes=[
            pltpu.VMEM((rows + SUBLANES, SSD_INNER), F32),
            pltpu.VMEM((rows + SUBLANES, gn), F32),
            pltpu.VMEM((rows + SUBLANES, gn), F32),
            pltpu.VMEM((rows, SSD_INNER), F32),
            pltpu.VMEM((rows, gn), F32),
            pltpu.VMEM((rows, gn), F32),
            pltpu.VMEM((SSD_STATE, SSD_INNER), F32),
        ],
        compiler_params=pltpu.CompilerParams(
            dimension_semantics=("arbitrary",), vmem_limit_bytes=VMEM_LIMIT),
        name="ssd",
    )(u, u, u, dt_raw,
      cw[:, :SSD_INNER], cw[:, SSD_INNER:SSD_INNER + gn], cw[:, SSD_INNER + gn:],
      cb[None, :SSD_INNER], cb[None, SSD_INNER:SSD_INNER + gn], cb[None, SSD_INNER + gn:],
      jnp.pad(dt_bias, (0, pad))[None, :], jnp.pad(a_log, (0, pad))[None, :],
      jnp.repeat(d_skip, SSD_HEADDIM)[None, :])


def _post_kernel(y_ref, z_ref, scb_ref, scc_ref, scx_ref, gate_ref, x_ref,
                 nw_ref, wa_ref, csw_ref, wb_ref, bg_ref, wo_ref, g1_ref, lng_ref, lnb_ref,
                 sh2_ref, sc2_ref, wr_ref, br_ref,
                 x1_ref, h2_ref, prob_ref, cst_ref, sidt_ref, runtab_ref, cnt_ref,
                 sbuf, run, *, rows, n_tok):
    @pl.when(pl.program_id(0) == 0)
    def _():
        sbuf[0:SUBLANES, :] = jnp.zeros((SUBLANES, sbuf.shape[1]), F32)
        run[...] = jnp.zeros(run.shape, F32)

    z = z_ref[...].astype(F32)
    yg = y_ref[...].astype(F32) * (z * _sigmoid(z))
    ms = jnp.mean(yg * yg, axis=-1, keepdims=True)
    yn = yg * lax.rsqrt(ms + RMS_EPS) * nw_ref[...]
    u_a = _dot(yn.astype(BF16), wa_ref[...])

    sbuf[SUBLANES:SUBLANES + rows, :] = scc_ref[...].astype(F32) * scx_ref[...].astype(F32)
    v = jnp.zeros((rows, D_MODEL), F32)
    for j in range(SC_CONV):
        off = SUBLANES - (SC_CONV - 1) + j
        v = v + csw_ref[j:j + 1, :] * sbuf[off:off + rows, :]
    sbuf[0:SUBLANES, :] = sbuf[rows:rows + SUBLANES, :]
    u_b = _dot((scb_ref[...].astype(F32) * v).astype(BF16), wb_ref[...])

    gl = gate_ref[...].astype(F32) + bg_ref[...]
    merged = _sigmoid(gl[:, :D_MODEL]) * u_a + _sigmoid(gl[:, D_MODEL:]) * u_b
    mix = _dot(merged.astype(BF16), wo_ref[...])
    x1 = _layer_norm(DN_ALPHA * x_ref[...] + g1_ref[...] * mix) * lng_ref[...] + lnb_ref[...]
    x1_ref[...] = x1
    h2 = _layer_norm(x1) * (1.0 + sc2_ref[...]) + sh2_ref[...]
    half = D_MODEL // 2
    h2_ref[...] = _pack_bf16_pair(h2[:, :half], h2[:, half:])

    logits = _dot_split(h2, wr_ref[...]) + br_ref[...]
    lane = lax.broadcasted_iota(jnp.int32, (rows, LANES), 1).astype(F32)
    work = logits
    onehots, vals = [], []
    for k in range(TOP_K):
        m = jnp.max(work, axis=-1, keepdims=True)
        ik = jnp.min(jnp.where(work == m, lane, float(LANES)), axis=-1, keepdims=True)
        oh = lane == ik
        onehots.append(oh)
        vals.append(m)
        work = jnp.where(oh, -jnp.inf, work)
    es = [jnp.exp(vk - vals[0]) for vk in vals]
    denom = es[0] + es[1] + es[2] + es[3]
    prob_out = jnp.zeros((rows, LANES), F32)
    for k in range(TOP_K):
        prob_out = jnp.where(lane == float(k), es[k] / denom, prob_out)

    sel = jnp.zeros((rows, LANES), F32)
    kk = jnp.zeros((rows, LANES), F32)
    for k, oh in enumerate(onehots):
        sel = sel + jnp.where(oh, 1.0, 0.0)
        kk = kk + jnp.where(oh, float(k), 0.0)
    ri = lax.broadcasted_iota(jnp.int32, (rows, rows), 0)
    rj = lax.broadcasted_iota(jnp.int32, (rows, rows), 1)
    strict = jnp.where(ri > rj, 1.0, 0.0).astype(BF16)
    base = _dot(strict, sel.astype(BF16)) + run[...]
    tok = (pl.program_id(0) * rows
           + lax.broadcasted_iota(jnp.int32, (rows, LANES), 0)).astype(F32)
    picked = sel > 0.0
    cs = jnp.where(picked, base + 1.0, 0.0)
    sid = jnp.where(picked, kk * float(n_tok) + tok, 0.0)
    cst_ref[...] = cs.T[:N_EXPERTS, :]
    sidt_ref[...] = sid.T[:N_EXPERTS, :]
    runtab_ref[0] = run[...].astype(jnp.int32)
    run[...] = run[...] + jnp.sum(sel, axis=0, keepdims=True)
    prob_ref[...] = prob_out
    cnt_ref[...] = run[...].astype(jnp.int32)


def _post(y, u, x, vecs, mats, rows):
    t = x.shape[0]
    kern = functools.partial(_post_kernel, rows=rows, n_tok=t)
    row = lambda w: pl.BlockSpec((1, w), lambda i: (0, 0))
    mat = lambda a, b: pl.BlockSpec((a, b), lambda i: (0, 0))
    ublk = lambda w, col: pl.BlockSpec((rows, w), lambda i: (i, col // w))
    nw, csw, bg, g1, lng, lnb, sh2, sc2, br = vecs
    wa, wb, wo, wr = mats
    tok = pl.BlockSpec((rows, LANES), lambda i: (i, 0))
    return pl.pallas_call(
        kern,
        grid=(t // rows,),
        in_specs=[
            pl.BlockSpec((rows, SSD_INNER), lambda i: (i, 0)),
            ublk(SSD_INNER, COL_Z), ublk(D_MODEL, COL_SCB), ublk(D_MODEL, COL_SCC),
            ublk(D_MODEL, COL_SCX), ublk(2 * D_MODEL, COL_GATE),
            pl.BlockSpec((rows, D_MODEL), lambda i: (i, 0)),
            row(SSD_INNER), mat(SSD_INNER, D_MODEL), mat(SC_CONV, D_MODEL), mat(D_MODEL, D_MODEL),
            row(2 * D_MODEL), mat(D_MODEL, D_MODEL), row(D_MODEL), row(D_MODEL), row(D_MODEL),
            row(D_MODEL), row(D_MODEL), mat(D_MODEL, LANES), row(LANES),
        ],
        out_specs=[
            pl.BlockSpec((rows, D_MODEL), lambda i: (i, 0)),
            pl.BlockSpec((rows, D_MODEL // 2), lambda i: (i, 0)),
            tok,
            pl.BlockSpec((N_EXPERTS, rows), lambda i: (0, i)),
            pl.BlockSpec((N_EXPERTS, rows), lambda i: (0, i)),
            pl.BlockSpec((1, 1, LANES), lambda i: (i, 0, 0)),
            pl.BlockSpec((1, LANES), lambda i: (0, 0)),
        ],
        out_shape=[
            jax.ShapeDtypeStruct((t, D_MODEL), F32),
            jax.ShapeDtypeStruct((t, D_MODEL // 2), jnp.uint32),
            jax.ShapeDtypeStruct((t, LANES), F32),
            jax.ShapeDtypeStruct((N_EXPERTS, t), F32),
            jax.ShapeDtypeStruct((N_EXPERTS, t), F32),
            jax.ShapeDtypeStruct((t // rows, 1, LANES), jnp.int32),
            jax.ShapeDtypeStruct((1, LANES), jnp.int32),
        ],
        scratch_shapes=[
            pltpu.VMEM((rows + SUBLANES, D_MODEL), F32),
            pltpu.VMEM((1, LANES), F32),
        ],
        compiler_params=pltpu.CompilerParams(
            dimension_semantics=("arbitrary",), vmem_limit_bytes=VMEM_LIMIT),
        name="post_mix",
    )(y, u, u, u, u, u, x, nw, wa, csw, wb, bg, wo, g1, lng, lnb, sh2, sc2, wr, br)


def _invert_kernel(vt_ref, ve_ref, nv_ref, gs_ref, run_ref, cst_ref, sidt_ref, o_ref, ptr,
                   *, tile, tok_tile, n_tok_tiles):
    v = pl.program_id(0)
    i = vt_ref[v]
    e = ve_ref[v]
    vp = jnp.maximum(v - 1, 0)
    new_tile = jnp.logical_or(v == 0, vt_ref[vp] != i)
    new_exp = jnp.logical_or(v == 0, ve_ref[vp] != e)

    @pl.when(new_exp)
    def _():
        ptr[0] = 0

    @pl.when(new_tile)
    def _():
        o_ref[...] = jnp.zeros(o_ref.shape, jnp.int32)

    @pl.when(v < nv_ref[0])
    def _():
        g0 = gs_ref[e]
        row0 = i * tile
        ra = jnp.maximum(g0, row0) - g0
        rb = jnp.minimum(gs_ref[e + 1], row0 + tile) - g0
        b_lo = lax.while_loop(lambda b: run_ref[(b + 1) * N_EXPERTS + e] <= ra,
                              lambda b: b + 1, ptr[0])
        ptr[0] = b_lo
        b_hi = lax.while_loop(
            lambda b: jnp.logical_and(b < n_tok_tiles, run_ref[b * N_EXPERTS + e] < rb),
            lambda b: b + 1, b_lo)
        for rblk in range(tile // LANES):
            base = (row0 - g0 + 1 + rblk * LANES).astype(F32)
            want = base + lax.broadcasted_iota(jnp.int32, (LANES, LANES), 0).astype(F32)

            def body(b, acc, want=want):
                c0 = pl.multiple_of(b * tok_tile, tok_tile)
                cs_row = cst_ref[pl.ds(e, 1), pl.ds(c0, tok_tile)]
                sid_row = sidt_ref[pl.ds(e, 1), pl.ds(c0, tok_tile)]
                for j in range(tok_tile // LANES):
                    lanes = slice(j * LANES, (j + 1) * LANES)
                    acc = acc + jnp.where(cs_row[:, lanes] == want, sid_row[:, lanes], 0.0)
                return acc

            acc = lax.fori_loop(b_lo, b_hi, body, jnp.zeros((LANES, LANES), F32))
            contrib = jnp.sum(acc.T, axis=0, keepdims=True)
            lanes = slice(rblk * LANES, (rblk + 1) * LANES)
            o_ref[0, :, lanes] = o_ref[0, :, lanes] + contrib.astype(jnp.int32)


def _invert(v_tile, v_exp, n_visits, gs, runflat, cst, sidt, n_tiles, tile, tok_tile):
    t = cst.shape[1]
    n_vis = v_tile.shape[0]
    kern = functools.partial(_invert_kernel, tile=tile, tok_tile=tok_tile,
                             n_tok_tiles=t // tok_tile)
    return pl.pallas_call(
        kern,
        grid_spec=pltpu.PrefetchScalarGridSpec(
            num_scalar_prefetch=5,
            grid=(n_vis,),
            in_specs=[
                pl.BlockSpec((N_EXPERTS, t), lambda v, *_: (0, 0)),
                pl.BlockSpec((N_EXPERTS, t), lambda v, *_: (0, 0)),
            ],
            out_specs=pl.BlockSpec((1, 1, tile), lambda v, vt, *_: (vt[v], 0, 0)),
            scratch_shapes=[pltpu.SMEM((1,), jnp.int32)],
        ),
        out_shape=jax.ShapeDtypeStruct((n_tiles, 1, tile), jnp.int32),
        compiler_params=pltpu.CompilerParams(
            dimension_semantics=("arbitrary",), vmem_limit_bytes=VMEM_LIMIT),
        name="moe_invert",
    )(v_tile, v_exp, n_visits, gs, runflat, cst, sidt)


N_FF_BLK = 4
FF_BLK = EXPERT_FF // N_FF_BLK
GATHER_PRIORITY = 0
SCATTER_PRIORITY = 1


def _experts_kernel(vt_ref, ve_ref, nv_ref, gs_ref,
                    ord_prev, ord_cur, ord_next, h_hbm, wgu_ref, bgu_ref, wd_ref, bd_ref,
                    slots_hbm, xbuf, obuf, act, wgu_bf, wd_bf, gsem, ssem, *, tile, n_tok):
    v = pl.program_id(0)
    nv = nv_ref[0]
    i = vt_ref[v]
    e = ve_ref[v]
    vp = jnp.maximum(v - 1, 0)
    active = v < nv
    new_tile = jnp.logical_or(v == 0, vt_ref[vp] != i)
    new_exp = jnp.logical_or(v == 0, ve_ref[vp] != e)
    slot = lax.rem(i, 2)
    other = 1 - slot

    half = D_MODEL // 2

    def gather_row(order_ref, r, s):
        sid = order_ref[r]
        tok = sid & (n_tok - 1) if n_tok & (n_tok - 1) == 0 else lax.rem(sid, n_tok)
        return pltpu.make_async_copy(h_hbm.at[pl.ds(tok, 1)], xbuf.at[s, pl.ds(r, 1)], gsem.at[s])

    def scatter_row(order_ref, r, s):
        return pltpu.make_async_copy(obuf.at[s, pl.ds(r, 1)],
                                     slots_hbm.at[pl.ds(order_ref[r], 1)], ssem.at[s])

    @pl.when(v == 0)
    def _():
        obuf[1] = jnp.zeros((tile, half), jnp.uint32)
        for r in range(tile):
            gather_row(ord_cur, r, 0).start(priority=GATHER_PRIORITY)

    @pl.when(jnp.logical_and(active, new_exp))
    def _():
        wgu_bf[...] = wgu_ref[...].astype(BF16)
        wd_bf[...] = wd_ref[...].astype(BF16)

    rowpos = i * tile + lax.broadcasted_iota(jnp.int32, (tile, 1), 0)
    mine = jnp.logical_and(rowpos >= gs_ref[e], rowpos < gs_ref[e + 1])

    def compute(first, between):
        x_lo, x_hi = _unpack_bf16_pair(xbuf[slot])
        xb = jnp.concatenate([x_lo.astype(BF16), x_hi.astype(BF16)], axis=1)
        for cb in range(N_FF_BLK):
            c0 = cb * FF_BLK
            g = _dot(xb, wgu_bf[:, c0:c0 + FF_BLK]) + bgu_ref[:, c0:c0 + FF_BLK]
            u = (_dot(xb, wgu_bf[:, EXPERT_FF + c0:EXPERT_FF + c0 + FF_BLK])
                 + bgu_ref[:, EXPERT_FF + c0:EXPERT_FF + c0 + FF_BLK])
            g = jnp.minimum(g, SWIGLU_LIMIT)
            u = jnp.clip(u, -SWIGLU_LIMIT, SWIGLU_LIMIT)
            act[:, c0:c0 + FF_BLK] = ((u + 1.0) * g * _sigmoid(SWIGLU_ALPHA * g)).astype(BF16)
            between(cb)
        for pb in range(N_FF_BLK // 2):
            c0 = pb * FF_BLK
            o_lo = _dot(act[...], wd_bf[:, c0:c0 + FF_BLK]) + bd_ref[:, c0:c0 + FF_BLK]
            between(N_FF_BLK + 2 * pb)
            o_hi = (_dot(act[...], wd_bf[:, half + c0:half + c0 + FF_BLK])
                    + bd_ref[:, half + c0:half + c0 + FF_BLK])
            packed = _pack_bf16_pair(o_lo, o_hi)
            keep = jnp.uint32(0) if first else obuf[slot, :, c0:c0 + FF_BLK]
            obuf[slot, :, c0:c0 + FF_BLK] = jnp.where(mine, packed, keep)
            between(N_FF_BLK + 2 * pb + 1)

    per_stage = tile // (2 * N_FF_BLK)

    @pl.when(jnp.logical_and(active, new_tile))
    def _():
        for r in range(tile):
            gather_row(ord_cur, r, slot).wait()

        @pl.when(i >= 1)
        def _():
            for r in range(tile):
                scatter_row(ord_prev, r, slot).wait()

        def between(j):
            for r in range(j * per_stage, (j + 1) * per_stage):
                gather_row(ord_next, r, other).start(priority=GATHER_PRIORITY)
                scatter_row(ord_prev, r, other).start(priority=SCATTER_PRIORITY)

        compute(True, between)

    @pl.when(jnp.logical_and(active, jnp.logical_not(new_tile)))
    def _():
        compute(False, lambda j: None)

    @pl.when(v == nv - 1)
    def _():
        for r in range(tile):
            scatter_row(ord_cur, r, slot).start(priority=SCATTER_PRIORITY)
        for r in range(tile):
            scatter_row(ord_prev, r, other).wait()
            scatter_row(ord_cur, r, slot).wait()
            gather_row(ord_next, r, other).wait()


def _experts(v_tile, v_exp, n_visits, gs, order_ext, h2, w_gu, b_gu, w_down, b_down, layer,
             n_slots, tile):
    n_vis = v_tile.shape[0]
    ff2 = 2 * EXPERT_FF
    half = D_MODEL // 2
    kern = functools.partial(_experts_kernel, tile=tile, n_tok=h2.shape[0])
    smem_blk = lambda off: pl.BlockSpec(
        (tile,), lambda v, vt, *_: (vt[v] + off,), memory_space=pltpu.SMEM)
    wspec = lambda a, b: pl.BlockSpec(
        (None, None, a, b), lambda v, vt, ve, *_: (layer, ve[v], 0, 0))
    return pl.pallas_call(
        kern,
        grid_spec=pltpu.PrefetchScalarGridSpec(
            num_scalar_prefetch=4,
            grid=(n_vis,),
            in_specs=[
                smem_blk(0), smem_blk(1), smem_blk(2),
                pl.BlockSpec(memory_space=pl.ANY),
                wspec(D_MODEL, ff2), wspec(1, ff2), wspec(EXPERT_FF, D_MODEL), wspec(1, D_MODEL),
            ],
            out_specs=pl.BlockSpec(memory_space=pl.ANY),
            scratch_shapes=[
                pltpu.VMEM((2, tile, half), jnp.uint32),
                pltpu.VMEM((2, tile, half), jnp.uint32),
                pltpu.VMEM((tile, EXPERT_FF), BF16),
                pltpu.VMEM((D_MODEL, ff2), BF16),
                pltpu.VMEM((EXPERT_FF, D_MODEL), BF16),
                pltpu.SemaphoreType.DMA((2,)),
                pltpu.SemaphoreType.DMA((2,)),
            ],
        ),
        out_shape=jax.ShapeDtypeStruct((n_slots, half), jnp.uint32),
        compiler_params=pltpu.CompilerParams(
            dimension_semantics=("arbitrary",), vmem_limit_bytes=VMEM_LIMIT),
        name="moe_experts",
    )(v_tile, v_exp, n_visits, gs, order_ext, order_ext, order_ext, h2,
      w_gu, b_gu[:, :, None, :], w_down, b_down[:, :, None, :])


def _combine_kernel(s0_ref, s1_ref, s2_ref, s3_ref, prob_ref, x1_ref, g2_ref, lng_ref, lnb_ref,
                    x2_ref):
    prob = prob_ref[...]
    ffn_lo = jnp.zeros((x1_ref.shape[0], D_MODEL // 2), F32)
    ffn_hi = jnp.zeros((x1_ref.shape[0], D_MODEL // 2), F32)
    for k, s_ref in enumerate((s0_ref, s1_ref, s2_ref, s3_ref)):
        lo, hi = _unpack_bf16_pair(s_ref[...])
        ffn_lo = ffn_lo + prob[:, k:k + 1] * lo
        ffn_hi = ffn_hi + prob[:, k:k + 1] * hi
    ffn = jnp.concatenate([ffn_lo, ffn_hi], axis=1)
    x2_ref[...] = (_layer_norm(DN_ALPHA * x1_ref[...] + g2_ref[...] * ffn) * lng_ref[...]
                   + lnb_ref[...])


def _combine(slots, probs, x1, g2, lng, lnb, rows):
    t = x1.shape[0]
    row = pl.BlockSpec((1, D_MODEL), lambda i: (0, 0))
    slot_blk = lambda k: pl.BlockSpec((rows, D_MODEL // 2), lambda i: (k * (t // rows) + i, 0))
    return pl.pallas_call(
        _combine_kernel,
        grid=(t // rows,),
        in_specs=[
            slot_blk(0), slot_blk(1), slot_blk(2), slot_blk(3),
            pl.BlockSpec((rows, LANES), lambda i: (i, 0)),
            pl.BlockSpec((rows, D_MODEL), lambda i: (i, 0)),
            row, row, row,
        ],
        out_specs=pl.BlockSpec((rows, D_MODEL), lambda i: (i, 0)),
        out_shape=jax.ShapeDtypeStruct((t, D_MODEL), F32),
        compiler_params=pltpu.CompilerParams(
            dimension_semantics=("arbitrary",), vmem_limit_bytes=VMEM_LIMIT),
        name="moe_combine",
    )(slots, slots, slots, slots, probs, x1, g2, lng, lnb)


def _tile(t, pref):
    return pref if t % pref == 0 else t


def _visit_plan(counts, tile, n_vis):
    ge = jnp.cumsum(counts)
    gs = ge - counts
    t_lo = gs // tile
    nt = jnp.where(counts > 0, (ge - 1) // tile - t_lo + 1, 0)
    vend = jnp.cumsum(nt)
    vbase = vend - nt
    n_visits = vend[-1:]
    vc = jnp.minimum(jnp.arange(n_vis, dtype=jnp.int32), n_visits[0] - 1)
    v_exp = jnp.sum((vend[None, :] <= vc[:, None]).astype(jnp.int32), axis=1)
    onehot = (v_exp[:, None] == jnp.arange(N_EXPERTS, dtype=jnp.int32)[None, :]).astype(jnp.int32)
    v_tile = jnp.sum(onehot * (t_lo - vbase)[None, :], axis=1) + vc
    gs33 = jnp.concatenate([gs, ge[-1:]])
    return (v_tile.astype(jnp.int32), v_exp.astype(jnp.int32), n_visits.astype(jnp.int32),
            gs33.astype(jnp.int32))


def kernel(x, c, w_ada, b_ada, w_in, conv_ssd_w, conv_ssd_b, dt_bias, a_log, d_skip, ssd_norm_w,
           w_ssd_out, conv_short_w, w_short_out, b_gate, w_o, ln1_g, ln1_b, w_router, b_router,
           w_gu, b_gu, w_down, b_down, ln2_g, ln2_b):
    batch, seq, d = x.shape
    assert batch == 1 and d == D_MODEL
    depth = w_in.shape[0]
    t = seq
    xt = x.reshape(t, d)

    tm_in = _tile(t, 1024)
    rows_ssd = _tile(t, 512)
    chunk = 128
    rows_post = 256
    rows_moe = 256
    tile_e = 256
    assert t % rows_post == 0
    n_tiles = t * TOP_K // tile_e
    n_vis = n_tiles + N_EXPERTS
    n_slots = t * TOP_K + tile_e

    mods = _ada_mod(c, w_ada, b_ada)

    s0 = SSD_INNER
    s1 = s0 + SSD_INNER + 2 * SSD_GROUPS * SSD_STATE
    s2 = s1 + SSD_HEADS
    w_main = jnp.concatenate([w_in[:, :, :s1], w_in[:, :, s2:]], axis=-1).astype(BF16)
    w_dt = jnp.pad(w_in[:, :, s1:s2], ((0, 0), (0, 0), (0, LANES - SSD_HEADS)))
    w_r = jnp.pad(w_router, ((0, 0), (0, 0), (0, LANES - N_EXPERTS)))
    b_r = jnp.pad(b_router, ((0, 0), (0, LANES - N_EXPERTS)), constant_values=NEG_BIG)
    wa_bf = w_ssd_out.astype(BF16)
    wb_bf = w_short_out.astype(BF16)
    wo_bf = w_o.astype(BF16)
    warm = t * TOP_K + jnp.arange(tile_e, dtype=jnp.int32)
    cool = jnp.zeros((tile_e,), jnp.int32)

    for l in range(depth):
        m = mods[l]
        sh1, sc1, g1, sh2, sc2, g2 = [m[:, k * d:(k + 1) * d] for k in range(N_ADA)]
        u, dt_raw = _in_proj(xt, sh1, sc1, w_main[l], w_dt[l], tm_in, 1024)
        y = _ssd(u, dt_raw, conv_ssd_w[l], conv_ssd_b[l], dt_bias[l], a_log[l], d_skip[l],
                 rows_ssd, chunk)
        vecs = (ssd_norm_w[l][None, :], conv_short_w[l], b_gate[l][None, :], g1,
                ln1_g[l][None, :], ln1_b[l][None, :], sh2, sc2, b_r[l][None, :])
        x1, h2, probs, cst, sidt, runtab, cnt = _post(
            y, u, xt, vecs, (wa_bf[l], wb_bf[l], wo_bf[l], w_r[l]), rows_post)

        v_tile, v_exp, n_visits, gs = _visit_plan(cnt[0, :N_EXPERTS], tile_e, n_vis)
        runflat = jnp.concatenate([runtab[:, 0, :N_EXPERTS], cnt[:, :N_EXPERTS]]).reshape(-1)
        order = _invert(v_tile, v_exp, n_visits, gs, runflat, cst, sidt, n_tiles, tile_e,
                        rows_post)
        order_ext = jnp.concatenate([warm, order.reshape(-1), cool])
        slots = _experts(v_tile, v_exp, n_visits, gs, order_ext, h2, w_gu, b_gu, w_down, b_down,
                         l, n_slots, tile_e)
        xt = _combine(slots, probs, x1, g2, ln2_g[l][None, :], ln2_b[l][None, :], rows_moe)

    return xt.reshape(batch, seq, d)
```

```python
import functools

import jax
import jax.numpy as jnp
from jax import lax
from jax.experimental import pallas as pl
from jax.experimental.pallas import tpu as pltpu

F32 = jnp.float32
BF16 = jnp.bfloat16

D_MODEL = 1024
SSD_INNER = 2048
SSD_HEADS = 32
SSD_HEADDIM = 64
SSD_GROUPS = 4
SSD_STATE = 128
SSD_CONV = 4
SC_CONV = 3
N_EXPERTS = 32
TOP_K = 4
EXPERT_FF = 1024
SWIGLU_LIMIT = 7.0
SWIGLU_ALPHA = 1.702
DEPTH = 4
DN_ALPHA = (2.0 * DEPTH) ** 0.25
LN_EPS = 1e-5
RMS_EPS = 1e-5
N_ADA = 6

LANES = 128
SUBLANES = 8
U_MAIN = 10240
COL_Z = 0
COL_XS = 2048
COL_B = 4096
COL_C = 4608
COL_SCB = 5120
COL_SCC = 6144
COL_SCX = 7168
COL_GATE = 8192
NEG_BIG = -1e30

VMEM_LIMIT = 56 * 1024 * 1024
EXPERTS_VMEM_LIMIT = 60 * 1024 * 1024


def _sigmoid(v):
    return 1.0 / (1.0 + jnp.exp(-v))


def _softplus(v):
    return jnp.maximum(v, 0.0) + jnp.log(1.0 + jnp.exp(-jnp.abs(v)))


def _layer_norm(v):
    mu = jnp.mean(v, axis=-1, keepdims=True)
    vc = v - mu
    var = jnp.mean(vc * vc, axis=-1, keepdims=True)
    return vc * lax.rsqrt(var + LN_EPS)


def _split3(v):
    hi = v.astype(BF16)
    r1 = v - hi.astype(F32)
    mid = r1.astype(BF16)
    lo = (r1 - mid.astype(F32)).astype(BF16)
    return hi, mid, lo


def _dot(a, b):
    return jnp.dot(a, b, preferred_element_type=F32)


def _pack_bf16_pair(lo, hi):
    lo_bits = pltpu.bitcast(lo.astype(BF16).astype(F32), jnp.uint32)
    hi_bits = pltpu.bitcast(hi.astype(BF16).astype(F32), jnp.uint32)
    return lax.shift_right_logical(lo_bits, jnp.uint32(16)) | (hi_bits & jnp.uint32(0xFFFF0000))


def _unpack_bf16_pair(w):
    lo = pltpu.bitcast(lax.shift_left(w, jnp.uint32(16)), F32)
    hi = pltpu.bitcast(w & jnp.uint32(0xFFFF0000), F32)
    return lo, hi


def _dot_split(a, b):
    a_hi = a.astype(BF16)
    a_mid = (a - a_hi.astype(F32)).astype(BF16)
    b_hi = b.astype(BF16)
    b_mid = (b - b_hi.astype(F32)).astype(BF16)
    return _dot(a_hi, b_hi) + _dot(a_hi, b_mid) + _dot(a_mid, b_hi)


def _dot_exact_lhs(a_bf16, v):
    hi, mid, lo = _split3(v)
    return _dot(a_bf16, hi) + _dot(a_bf16, mid) + _dot(a_bf16, lo)


def _ada_kernel(c_ref, w_ref, b_ref, o_ref):
    c = c_ref[...]
    s = c * _sigmoid(c)
    o_ref[0] = jnp.sum(w_ref[0] * s, axis=0, keepdims=True) + b_ref[0]


def _ada_mod(c, w_ada, b_ada):
    depth, d, n = w_ada.shape
    tn = 1024
    return pl.pallas_call(
        _ada_kernel,
        grid=(depth, n // tn),
        in_specs=[
            pl.BlockSpec((d, 1), lambda l, j: (0, 0)),
            pl.BlockSpec((1, d, tn), lambda l, j: (l, 0, j)),
            pl.BlockSpec((1, 1, tn), lambda l, j: (l, 0, j)),
        ],
        out_specs=pl.BlockSpec((1, 1, tn), lambda l, j: (l, 0, j)),
        out_shape=jax.ShapeDtypeStruct((depth, 1, n), F32),
        compiler_params=pltpu.CompilerParams(
            dimension_semantics=("arbitrary", "arbitrary"), vmem_limit_bytes=VMEM_LIMIT),
        name="ada_mod",
    )(c.reshape(d, 1), w_ada, b_ada.reshape(depth, 1, n))


def _inproj_kernel(x_ref, sh_ref, sc_ref, w_ref, wdt_ref, u_ref, dt_ref, h_scr):
    @pl.when(pl.program_id(1) == 0)
    def _():
        h = _layer_norm(x_ref[...]) * (1.0 + sc_ref[...]) + sh_ref[...]
        h_scr[...] = h.astype(BF16)
        dt_ref[...] = _dot_split(h, wdt_ref[...])

    u_ref[...] = _dot(h_scr[...], w_ref[...]).astype(BF16)


def _in_proj(x, sh, sc, w_main, w_dt, tm, tn):
    t, d = x.shape
    n = w_main.shape[1]
    return pl.pallas_call(
        _inproj_kernel,
        grid=(t // tm, n // tn),
        in_specs=[
            pl.BlockSpec((tm, d), lambda i, j: (i, 0)),
            pl.BlockSpec((1, d), lambda i, j: (0, 0)),
            pl.BlockSpec((1, d), lambda i, j: (0, 0)),
            pl.BlockSpec((d, tn), lambda i, j: (0, j)),
            pl.BlockSpec((d, LANES), lambda i, j: (0, 0)),
        ],
        out_specs=[
            pl.BlockSpec((tm, tn), lambda i, j: (i, j)),
            pl.BlockSpec((tm, LANES), lambda i, j: (i, 0)),
        ],
        out_shape=[
            jax.ShapeDtypeStruct((t, n), BF16),
            jax.ShapeDtypeStruct((t, LANES), F32),
        ],
        scratch_shapes=[pltpu.VMEM((tm, d), BF16)],
        compiler_params=pltpu.CompilerParams(
            dimension_semantics=("arbitrary", "arbitrary"), vmem_limit_bytes=VMEM_LIMIT),
        name="in_proj",
    )(x, sh, sc, w_main, w_dt)


def _ssd_kernel(xs_ref, b_ref, c_ref, dt_ref, cwx_ref, cwb_ref, cwc_ref, cbx_ref, cbb_ref,
                cbc_ref, dtb_ref, alog_ref, dskip_ref, y_ref,
                xbuf, bbuf, cbuf, xcs, bcs, ccs, state, *, chunk, rows):
    @pl.when(pl.program_id(0) == 0)
    def _():
        xbuf[0:SUBLANES, :] = jnp.zeros((SUBLANES, xbuf.shape[1]), F32)
        bbuf[0:SUBLANES, :] = jnp.zeros((SUBLANES, bbuf.shape[1]), F32)
        cbuf[0:SUBLANES, :] = jnp.zeros((SUBLANES, cbuf.shape[1]), F32)
        state[...] = jnp.zeros(state.shape, F32)

    def conv_silu(in_ref, buf, w_ref, bias_ref, out_scr):
        buf[SUBLANES:SUBLANES + rows, :] = in_ref[...].astype(F32)
        acc = bias_ref[...]
        for j in range(SSD_CONV):
            off = SUBLANES - (SSD_CONV - 1) + j
            acc = acc + w_ref[j:j + 1, :] * buf[off:off + rows, :]
        out_scr[...] = acc * _sigmoid(acc)
        buf[0:SUBLANES, :] = buf[rows:rows + SUBLANES, :]

    conv_silu(xs_ref, xbuf, cwx_ref, cbx_ref, xcs)
    conv_silu(b_ref, bbuf, cwb_ref, cbb_ref, bcs)
    conv_silu(c_ref, cbuf, cwc_ref, cbc_ref, ccs)

    li = lax.broadcasted_iota(jnp.int32, (chunk, chunk), 0)
    si = lax.broadcasted_iota(jnp.int32, (chunk, chunk), 1)
    causal = li >= si
    tri = jnp.where(causal, 1.0, 0.0).astype(BF16)
    first_half = lax.broadcasted_iota(jnp.int32, (1, LANES), 1) < SSD_HEADDIM
    a_row = -jnp.exp(alog_ref[...])
    heads_per_group = SSD_HEADS // SSD_GROUPS

    def chunk_body(ci, carry):
        r0 = pl.multiple_of(ci * chunk, chunk)
        dt = _softplus(dt_ref[pl.ds(r0, chunk), :] + dtb_ref[...])
        la = dt * a_row
        acum = _dot_exact_lhs(tri, la)
        exp_acum = jnp.exp(acum)
        acum_t = acum.T
        dt_t = dt.T
        last_t = acum_t[:, chunk - 1:chunk]
        w_t = jnp.exp(last_t - acum_t) * dt_t
        cdec_t = jnp.exp(last_t)

        cb = []
        bt = []
        cg = []
        for g in range(SSD_GROUPS):
            bg = bcs[pl.ds(r0, chunk), g * SSD_STATE:(g + 1) * SSD_STATE]
            cgv = ccs[pl.ds(r0, chunk), g * SSD_STATE:(g + 1) * SSD_STATE]
            btg = bg.T
            bt.append(btg)
            cg.append(cgv)
            cb.append(_dot(cgv.astype(BF16), btg.astype(BF16)))

        for pr in range(SSD_HEADS // 2):
            lo = pr * LANES
            xs_pair = xcs[pl.ds(r0, chunk), lo:lo + LANES]
            xs_bf = xs_pair.astype(BF16)
            prev = state[:, lo:lo + LANES]
            lhs_m, lhs_c, lhs_b, cd = [], [], [], []
            for h in (2 * pr, 2 * pr + 1):
                g = h // heads_per_group
                col = acum[:, h:h + 1]
                row = acum_t[h:h + 1, :]
                dec = jnp.exp(jnp.where(causal, col - row, NEG_BIG))
                lhs_m.append((cb[g] * dec * dt_t[h:h + 1, :]).astype(BF16))
                lhs_c.append((cg[g] * exp_acum[:, h:h + 1]).astype(BF16))
                lhs_b.append((bt[g] * w_t[h:h + 1, :]).astype(BF16))
                cd.append(cdec_t[h:h + 1, :])
            out = (_dot(jnp.concatenate(lhs_m, axis=0), xs_bf)
                   + _dot(jnp.concatenate(lhs_c, axis=0), prev.astype(BF16)))
            y_pair = jnp.where(first_half, out[0:chunk, :], out[chunk:2 * chunk, :])
            y_ref[pl.ds(r0, chunk), lo:lo + LANES] = (
                y_pair + xs_pair * dskip_ref[:, lo:lo + LANES]).astype(y_ref.dtype)
            st = _dot(jnp.concatenate(lhs_b, axis=0), xs_bf)
            cd_pair = jnp.where(first_half, cd[0], cd[1])
            state[:, lo:lo + LANES] = prev * cd_pair + jnp.where(
                first_half, st[0:SSD_STATE, :], st[SSD_STATE:2 * SSD_STATE, :])
        return carry

    lax.fori_loop(0, rows // chunk, chunk_body, 0)


def _ssd(u, dt_raw, cw, cb, dt_bias, a_log, d_skip, rows, chunk):
    t = u.shape[0]
    gn = SSD_GROUPS * SSD_STATE
    pad = LANES - SSD_HEADS
    kern = functools.partial(_ssd_kernel, chunk=chunk, rows=rows)
    full = lambda shape: pl.BlockSpec(shape, lambda i: (0, 0))
    return pl.pallas_call(
        kern,
        grid=(t // rows,),
        in_specs=[
            pl.BlockSpec((rows, SSD_INNER), lambda i: (i, COL_XS // SSD_INNER)),
            pl.BlockSpec((rows, gn), lambda i: (i, COL_B // gn)),
            pl.BlockSpec((rows, gn), lambda i: (i, COL_C // gn)),
            pl.BlockSpec((rows, LANES), lambda i: (i, 0)),
            full((SSD_CONV, SSD_INNER)), full((SSD_CONV, gn)), full((SSD_CONV, gn)),
            full((1, SSD_INNER)), full((1, gn)), full((1, gn)),
            full((1, LANES)), full((1, LANES)), full((1, SSD_INNER)),
        ],
        out_specs=pl.BlockSpec((rows, SSD_INNER), lambda i: (i, 0)),
        out_shape=jax.ShapeDtypeStruct((t, SSD_INNER), BF16),
        scratch_shapes=[
            pltpu.VMEM((rows + SUBLANES, SSD_INNER), F32),
            pltpu.VMEM((rows + SUBLANES, gn), F32),
            pltpu.VMEM((rows + SUBLANES, gn), F32),
            pltpu.VMEM((rows, SSD_INNER), F32),
            pltpu.VMEM((rows, gn), F32),
            pltpu.VMEM((rows, gn), F32),
            pltpu.VMEM((SSD_STATE, SSD_INNER), F32),
        ],
        compiler_params=pltpu.CompilerParams(
            dimension_semantics=("arbitrary",), vmem_limit_bytes=VMEM_LIMIT),
        name="ssd",
    )(u, u, u, dt_raw,
      cw[:, :SSD_INNER], cw[:, SSD_INNER:SSD_INNER + gn], cw[:, SSD_INNER + gn:],
      cb[None, :SSD_INNER], cb[None, SSD_INNER:SSD_INNER + gn], cb[None, SSD_INNER + gn:],
      jnp.pad(dt_bias, (0, pad))[None, :], jnp.pad(a_log, (0, pad))[None, :],
      jnp.repeat(d_skip, SSD_HEADDIM)[None, :])


def _post_kernel(y_ref, z_ref, scb_ref, scc_ref, scx_ref, gate_ref, x_ref,
                 nw_ref, wa_ref, csw_ref, wb_ref, bg_ref, wo_ref, g1_ref, lng_ref, lnb_ref,
                 sh2_ref, sc2_ref, wr_ref, br_ref,
                 x1_ref, h2_ref, prob_ref, cst_ref, sidt_ref, runtab_ref, cnt_ref,
                 sbuf, run, *, rows, n_tok):
    @pl.when(pl.program_id(0) == 0)
    def _():
        sbuf[0:SUBLANES, :] = jnp.zeros((SUBLANES, sbuf.shape[1]), F32)
        run[...] = jnp.zeros(run.shape, F32)

    z = z_ref[...].astype(F32)
    yg = y_ref[...].astype(F32) * (z * _sigmoid(z))
    ms = jnp.mean(yg * yg, axis=-1, keepdims=True)
    yn = yg * lax.rsqrt(ms + RMS_EPS) * nw_ref[...]
    u_a = _dot(yn.astype(BF16), wa_ref[...])

    sbuf[SUBLANES:SUBLANES + rows, :] = scc_ref[...].astype(F32) * scx_ref[...].astype(F32)
    v = jnp.zeros((rows, D_MODEL), F32)
    for j in range(SC_CONV):
        off = SUBLANES - (SC_CONV - 1) + j
        v = v + csw_ref[j:j + 1, :] * sbuf[off:off + rows, :]
    sbuf[0:SUBLANES, :] = sbuf[rows:rows + SUBLANES, :]
    u_b = _dot((scb_ref[...].astype(F32) * v).astype(BF16), wb_ref[...])

    gl = gate_ref[...].astype(F32) + bg_ref[...]
    merged = _sigmoid(gl[:, :D_MODEL]) * u_a + _sigmoid(gl[:, D_MODEL:]) * u_b
    mix = _dot(merged.astype(BF16), wo_ref[...])
    x1 = _layer_norm(DN_ALPHA * x_ref[...] + g1_ref[...] * mix) * lng_ref[...] + lnb_ref[...]
    x1_ref[...] = x1
    h2 = _layer_norm(x1) * (1.0 + sc2_ref[...]) + sh2_ref[...]
    half = D_MODEL // 2
    h2_ref[...] = _pack_bf16_pair(h2[:, :half], h2[:, half:])

    logits = _dot_split(h2, wr_ref[...]) + br_ref[...]
    lane = lax.broadcasted_iota(jnp.int32, (rows, LANES), 1).astype(F32)
    work = logits
    onehots, vals = [], []
    for k in range(TOP_K):
        m = jnp.max(work, axis=-1, keepdims=True)
        ik = jnp.min(jnp.where(work == m, lane, float(LANES)), axis=-1, keepdims=True)
        oh = lane == ik
        onehots.append(oh)
        vals.append(m)
        work = jnp.where(oh, -jnp.inf, work)
    es = [jnp.exp(vk - vals[0]) for vk in vals]
    denom = es[0] + es[1] + es[2] + es[3]
    prob_out = jnp.zeros((rows, LANES), F32)
    for k in range(TOP_K):
        prob_out = jnp.where(lane == float(k), es[k] / denom, prob_out)

    sel = jnp.zeros((rows, LANES), F32)
    kk = jnp.zeros((rows, LANES), F32)
    for k, oh in enumerate(onehots):
        sel = sel + jnp.where(oh, 1.0, 0.0)
        kk = kk + jnp.where(oh, float(k), 0.0)
    ri = lax.broadcasted_iota(jnp.int32, (rows, rows), 0)
    rj = lax.broadcasted_iota(jnp.int32, (rows, rows), 1)
    strict = jnp.where(ri > rj, 1.0, 0.0).astype(BF16)
    base = _dot(strict, sel.astype(BF16)) + run[...]
    tok = (pl.program_id(0) * rows
           + lax.broadcasted_iota(jnp.int32, (rows, LANES), 0)).astype(F32)
    picked = sel > 0.0
    cs = jnp.where(picked, base + 1.0, 0.0)
    sid = jnp.where(picked, kk * float(n_tok) + tok, 0.0)
    cst_ref[...] = cs.T[:N_EXPERTS, :]
    sidt_ref[...] = sid.T[:N_EXPERTS, :]
    runtab_ref[0] = run[...].astype(jnp.int32)
    run[...] = run[...] + jnp.sum(sel, axis=0, keepdims=True)
    prob_ref[...] = prob_out
    cnt_ref[...] = run[...].astype(jnp.int32)


def _post(y, u, x, vecs, mats, rows):
    t = x.shape[0]
    kern = functools.partial(_post_kernel, rows=rows, n_tok=t)
    row = lambda w: pl.BlockSpec((1, w), lambda i: (0, 0))
    mat = lambda a, b: pl.BlockSpec((a, b), lambda i: (0, 0))
    ublk = lambda w, col: pl.BlockSpec((rows, w), lambda i: (i, col // w))
    nw, csw, bg, g1, lng, lnb, sh2, sc2, br = vecs
    wa, wb, wo, wr = mats
    tok = pl.BlockSpec((rows, LANES), lambda i: (i, 0))
    return pl.pallas_call(
        kern,
        grid=(t // rows,),
        in_specs=[
            pl.BlockSpec((rows, SSD_INNER), lambda i: (i, 0)),
            ublk(SSD_INNER, COL_Z), ublk(D_MODEL, COL_SCB), ublk(D_MODEL, COL_SCC),
            ublk(D_MODEL, COL_SCX), ublk(2 * D_MODEL, COL_GATE),
            pl.BlockSpec((rows, D_MODEL), lambda i: (i, 0)),
            row(SSD_INNER), mat(SSD_INNER, D_MODEL), mat(SC_CONV, D_MODEL), mat(D_MODEL, D_MODEL),
            row(2 * D_MODEL), mat(D_MODEL, D_MODEL), row(D_MODEL), row(D_MODEL), row(D_MODEL),
            row(D_MODEL), row(D_MODEL), mat(D_MODEL, LANES), row(LANES),
        ],
        out_specs=[
            pl.BlockSpec((rows, D_MODEL), lambda i: (i, 0)),
            pl.BlockSpec((rows, D_MODEL // 2), lambda i: (i, 0)),
            tok,
            pl.BlockSpec((N_EXPERTS, rows), lambda i: (0, i)),
            pl.BlockSpec((N_EXPERTS, rows), lambda i: (0, i)),
            pl.BlockSpec((1, 1, LANES), lambda i: (i, 0, 0)),
            pl.BlockSpec((1, LANES), lambda i: (0, 0)),
        ],
        out_shape=[
            jax.ShapeDtypeStruct((t, D_MODEL), F32),
            jax.ShapeDtypeStruct((t, D_MODEL // 2), jnp.uint32),
            jax.ShapeDtypeStruct((t, LANES), F32),
            jax.ShapeDtypeStruct((N_EXPERTS, t), F32),
            jax.ShapeDtypeStruct((N_EXPERTS, t), F32),
            jax.ShapeDtypeStruct((t // rows, 1, LANES), jnp.int32),
            jax.ShapeDtypeStruct((1, LANES), jnp.int32),
        ],
        scratch_shapes=[
            pltpu.VMEM((rows + SUBLANES, D_MODEL), F32),
            pltpu.VMEM((1, LANES), F32),
        ],
        compiler_params=pltpu.CompilerParams(
            dimension_semantics=("arbitrary",), vmem_limit_bytes=VMEM_LIMIT),
        name="post_mix",
    )(y, u, u, u, u, u, x, nw, wa, csw, wb, bg, wo, g1, lng, lnb, sh2, sc2, wr, br)


def _invert_kernel(vt_ref, ve_ref, nv_ref, gs_ref, run_ref, cst_ref, sidt_ref, o_ref, ptr,
                   *, tile, tok_tile, n_tok_tiles):
    v = pl.program_id(0)
    i = vt_ref[v]
    e = ve_ref[v]
    vp = jnp.maximum(v - 1, 0)
    new_tile = jnp.logical_or(v == 0, vt_ref[vp] != i)
    new_exp = jnp.logical_or(v == 0, ve_ref[vp] != e)

    @pl.when(new_exp)
    def _():
        ptr[0] = 0

    @pl.when(new_tile)
    def _():
        o_ref[...] = jnp.zeros(o_ref.shape, jnp.int32)

    @pl.when(v < nv_ref[0])
    def _():
        g0 = gs_ref[e]
        row0 = i * tile
        ra = jnp.maximum(g0, row0) - g0
        rb = jnp.minimum(gs_ref[e + 1], row0 + tile) - g0
        b_lo = lax.while_loop(lambda b: run_ref[(b + 1) * N_EXPERTS + e] <= ra,
                              lambda b: b + 1, ptr[0])
        ptr[0] = b_lo
        b_hi = lax.while_loop(
            lambda b: jnp.logical_and(b < n_tok_tiles, run_ref[b * N_EXPERTS + e] < rb),
            lambda b: b + 1, b_lo)
        for rblk in range(tile // LANES):
            base = (row0 - g0 + 1 + rblk * LANES).astype(F32)
            want = base + lax.broadcasted_iota(jnp.int32, (LANES, LANES), 0).astype(F32)

            def body(b, acc, want=want):
                c0 = pl.multiple_of(b * tok_tile, tok_tile)
                cs_row = cst_ref[pl.ds(e, 1), pl.ds(c0, tok_tile)]
                sid_row = sidt_ref[pl.ds(e, 1), pl.ds(c0, tok_tile)]
                for j in range(tok_tile // LANES):
                    lanes = slice(j * LANES, (j + 1) * LANES)
                    acc = acc + jnp.where(cs_row[:, lanes] == want, sid_row[:, lanes], 0.0)
                return acc

            acc = lax.fori_loop(b_lo, b_hi, body, jnp.zeros((LANES, LANES), F32))
            contrib = jnp.sum(acc.T, axis=0, keepdims=True)
            lanes = slice(rblk * LANES, (rblk + 1) * LANES)
            o_ref[0, :, lanes] = o_ref[0, :, lanes] + contrib.astype(jnp.int32)


def _invert(v_tile, v_exp, n_visits, gs, runflat, cst, sidt, n_tiles, tile, tok_tile):
    t = cst.shape[1]
    n_vis = v_tile.shape[0]
    kern = functools.partial(_invert_kernel, tile=tile, tok_tile=tok_tile,
                             n_tok_tiles=t // tok_tile)
    return pl.pallas_call(
        kern,
        grid_spec=pltpu.PrefetchScalarGridSpec(
            num_scalar_prefetch=5,
            grid=(n_vis,),
            in_specs=[
                pl.BlockSpec((N_EXPERTS, t), lambda v, *_: (0, 0)),
                pl.BlockSpec((N_EXPERTS, t), lambda v, *_: (0, 0)),
            ],
            out_specs=pl.BlockSpec((1, 1, tile), lambda v, vt, *_: (vt[v], 0, 0)),
            scratch_shapes=[pltpu.SMEM((1,), jnp.int32)],
        ),
        out_shape=jax.ShapeDtypeStruct((n_tiles, 1, tile), jnp.int32),
        compiler_params=pltpu.CompilerParams(
            dimension_semantics=("arbitrary",), vmem_limit_bytes=VMEM_LIMIT),
        name="moe_invert",
    )(v_tile, v_exp, n_visits, gs, runflat, cst, sidt)


N_FF_BLK = 4
FF_BLK = EXPERT_FF // N_FF_BLK
N_STAGES = 2 * N_FF_BLK


def _experts_kernel(vt_ref, ve_ref, vn_ref, nv_ref, gs_ref,
                    ord_prev, ord_cur, ord_next, h_hbm, wgu_hbm, bgu_ref, wd_hbm, bd_ref,
                    slots_hbm, h_vmem, x_a, x_b, o_a, o_b, act, wgu_st, wd_st, wgu_bf, wd_bf,
                    hsem, wsem, ssem, *, tile, n_tok, layer):
    v = pl.program_id(0)
    nv = nv_ref[0]
    i = vt_ref[v]
    e = ve_ref[v]
    vp = jnp.maximum(v - 1, 0)
    active = v < nv
    new_tile = jnp.logical_or(v == 0, vt_ref[vp] != i)
    new_exp = jnp.logical_or(v == 0, ve_ref[vp] != e)
    even = lax.rem(i, 2) == 0
    half = D_MODEL // 2
    xs = (x_a, x_b)
    os_ = (o_a, o_b)

    def weight_copies(expert):
        return (pltpu.make_async_copy(wgu_hbm.at[layer, expert], wgu_st, wsem.at[0]),
                pltpu.make_async_copy(wd_hbm.at[layer, expert], wd_st, wsem.at[1]))

    def fetch_row(order_ref, r, dst):
        sid = order_ref[r]
        tok = sid & (n_tok - 1) if n_tok & (n_tok - 1) == 0 else lax.rem(sid, n_tok)
        dst[pl.ds(r, 1), :] = h_vmem[pl.ds(tok, 1), :]

    def scatter_row(order_ref, r, src, p):
        return pltpu.make_async_copy(src.at[pl.ds(r, 1)],
                                     slots_hbm.at[pl.ds(order_ref[r], 1)], ssem.at[p])

    @pl.when(v == 0)
    def _():
        tokens = pltpu.make_async_copy(h_hbm, h_vmem, hsem)
        tokens.start()
        for c in weight_copies(e):
            c.start()
        o_b[...] = jnp.zeros(o_b.shape, jnp.uint32)
        tokens.wait()
        for r in range(tile):
            fetch_row(ord_cur, r, x_a)

    @pl.when(jnp.logical_and(active, new_exp))
    def _():
        for c in weight_copies(e):
            c.wait()
        wgu_bf[...] = wgu_st[...].astype(BF16)
        wd_bf[...] = wd_st[...].astype(BF16)

        @pl.when(vn_ref[v] >= 0)
        def _():
            for c in weight_copies(vn_ref[v]):
                c.start()

    rowpos = i * tile + lax.broadcasted_iota(jnp.int32, (tile, 1), 0)
    mine = jnp.logical_and(rowpos >= gs_ref[e], rowpos < gs_ref[e + 1])

    def compute(x_ref, o_ref, first, between):
        x_lo, x_hi = _unpack_bf16_pair(x_ref[...])
        xb = jnp.concatenate([x_lo.astype(BF16), x_hi.astype(BF16)], axis=1)
        for cb in range(N_FF_BLK):
            c0 = cb * FF_BLK
            g = _dot(xb, wgu_bf[:, c0:c0 + FF_BLK]) + bgu_ref[:, c0:c0 + FF_BLK]
            u = (_dot(xb, wgu_bf[:, EXPERT_FF + c0:EXPERT_FF + c0 + FF_BLK])
                 + bgu_ref[:, EXPERT_FF + c0:EXPERT_FF + c0 + FF_BLK])
            g = jnp.minimum(g, SWIGLU_LIMIT)
            u = jnp.clip(u, -SWIGLU_LIMIT, SWIGLU_LIMIT)
            act[:, c0:c0 + FF_BLK] = ((u + 1.0) * g * _sigmoid(SWIGLU_ALPHA * g)).astype(BF16)
            between(cb)
        for pb in range(N_FF_BLK // 2):
            c0 = pb * FF_BLK
            o_lo = _dot(act[...], wd_bf[:, c0:c0 + FF_BLK]) + bd_ref[:, c0:c0 + FF_BLK]
            between(N_FF_BLK + 2 * pb)
            o_hi = (_dot(act[...], wd_bf[:, half + c0:half + c0 + FF_BLK])
                    + bd_ref[:, half + c0:half + c0 + FF_BLK])
            packed = _pack_bf16_pair(o_lo, o_hi)
            keep = jnp.uint32(0) if first else o_ref[:, c0:c0 + FF_BLK]
            o_ref[:, c0:c0 + FF_BLK] = jnp.where(mine, packed, keep)
            between(N_FF_BLK + 2 * pb + 1)

    per_stage = tile // N_STAGES

    for p in range(2):
        x_cur, x_nxt = xs[p], xs[1 - p]
        o_cur, o_oth = os_[p], os_[1 - p]
        on_parity = even if p == 0 else jnp.logical_not(even)

        @pl.when(jnp.logical_and(jnp.logical_and(active, new_tile), on_parity))
        def _(x_cur=x_cur, x_nxt=x_nxt, o_cur=o_cur, o_oth=o_oth, p=p):
            @pl.when(i >= 1)
            def _():
                for r in range(tile):
                    scatter_row(ord_prev, r, o_cur, p).wait()

            def between(j):
                for r in range(j * per_stage, (j + 1) * per_stage):
                    fetch_row(ord_next, r, x_nxt)
                    scatter_row(ord_prev, r, o_oth, 1 - p).start(priority=r % 2)

            compute(x_cur, o_cur, True, between)

        @pl.when(jnp.logical_and(jnp.logical_and(active, jnp.logical_not(new_tile)), on_parity))
        def _(x_cur=x_cur, o_cur=o_cur):
            compute(x_cur, o_cur, False, lambda j: None)

        @pl.when(jnp.logical_and(v == nv - 1, on_parity))
        def _(o_cur=o_cur, o_oth=o_oth, p=p):
            for r in range(tile):
                scatter_row(ord_cur, r, o_cur, p).start(priority=r % 2)
            for r in range(tile):
                scatter_row(ord_prev, r, o_oth, 1 - p).wait()
                scatter_row(ord_cur, r, o_cur, p).wait()


def _experts(v_tile, v_exp, v_next, n_visits, gs, order_ext, h2, w_gu, b_gu, w_down, b_down,
             layer, n_slots, tile):
    n_vis = v_tile.shape[0]
    n_tok = h2.shape[0]
    ff2 = 2 * EXPERT_FF
    half = D_MODEL // 2
    kern = functools.partial(_experts_kernel, tile=tile, n_tok=n_tok, layer=layer)
    smem_blk = lambda off: pl.BlockSpec(
        (tile,), lambda v, vt, *_: (vt[v] + off,), memory_space=pltpu.SMEM)
    bias = lambda n: pl.BlockSpec(
        (None, None, 1, n), lambda v, vt, ve, *_: (layer, ve[v], 0, 0))
    hbm = pl.BlockSpec(memory_space=pl.ANY)
    return pl.pallas_call(
        kern,
        grid_spec=pltpu.PrefetchScalarGridSpec(
            num_scalar_prefetch=5,
            grid=(n_vis,),
            in_specs=[smem_blk(0), smem_blk(1), smem_blk(2), hbm, hbm, bias(ff2), hbm,
                      bias(D_MODEL)],
            out_specs=hbm,
            scratch_shapes=[
                pltpu.VMEM((n_tok, half), jnp.uint32),
                pltpu.VMEM((tile, half), jnp.uint32),
                pltpu.VMEM((tile, half), jnp.uint32),
                pltpu.VMEM((tile, half), jnp.uint32),
                pltpu.VMEM((tile, half), jnp.uint32),
                pltpu.VMEM((tile, EXPERT_FF), BF16),
                pltpu.VMEM((D_MODEL, ff2), F32),
                pltpu.VMEM((EXPERT_FF, D_MODEL), F32),
                pltpu.VMEM((D_MODEL, ff2), BF16),
                pltpu.VMEM((EXPERT_FF, D_MODEL), BF16),
                pltpu.SemaphoreType.DMA(()),
                pltpu.SemaphoreType.DMA((2,)),
                pltpu.SemaphoreType.DMA((2,)),
            ],
        ),
        out_shape=jax.ShapeDtypeStruct((n_slots, half), jnp.uint32),
        compiler_params=pltpu.CompilerParams(
            dimension_semantics=("arbitrary",), vmem_limit_bytes=EXPERTS_VMEM_LIMIT),
        name="moe_experts",
    )(v_tile, v_exp, v_next, n_visits, gs, order_ext, order_ext, order_ext, h2,
      w_gu, b_gu[:, :, None, :], w_down, b_down[:, :, None, :])


def _combine_kernel(s0_ref, s1_ref, s2_ref, s3_ref, prob_ref, x1_ref, g2_ref, lng_ref, lnb_ref,
                    x2_ref):
    prob = prob_ref[...]
    ffn_lo = jnp.zeros((x1_ref.shape[0], D_MODEL // 2), F32)
    ffn_hi = jnp.zeros((x1_ref.shape[0], D_MODEL // 2), F32)
    for k, s_ref in enumerate((s0_ref, s1_ref, s2_ref, s3_ref)):
        lo, hi = _unpack_bf16_pair(s_ref[...])
        ffn_lo = ffn_lo + prob[:, k:k + 1] * lo
        ffn_hi = ffn_hi + prob[:, k:k + 1] * hi
    ffn = jnp.concatenate([ffn_lo, ffn_hi], axis=1)
    x2_ref[...] = (_layer_norm(DN_ALPHA * x1_ref[...] + g2_ref[...] * ffn) * lng_ref[...]
                   + lnb_ref[...])


def _combine(slots, probs, x1, g2, lng, lnb, rows):
    t = x1.shape[0]
    row = pl.BlockSpec((1, D_MODEL), lambda i: (0, 0))
    slot_blk = lambda k: pl.BlockSpec((rows, D_MODEL // 2), lambda i: (k * (t // rows) + i, 0))
    return pl.pallas_call(
        _combine_kernel,
        grid=(t // rows,),
        in_specs=[
            slot_blk(0), slot_blk(1), slot_blk(2), slot_blk(3),
            pl.BlockSpec((rows, LANES), lambda i: (i, 0)),
            pl.BlockSpec((rows, D_MODEL), lambda i: (i, 0)),
            row, row, row,
        ],
        out_specs=pl.BlockSpec((rows, D_MODEL), lambda i: (i, 0)),
        out_shape=jax.ShapeDtypeStruct((t, D_MODEL), F32),
        compiler_params=pltpu.CompilerParams(
            dimension_semantics=("arbitrary",), vmem_limit_bytes=VMEM_LIMIT),
        name="moe_combine",
    )(slots, slots, slots, slots, probs, x1, g2, lng, lnb)


def _tile(t, pref):
    return pref if t % pref == 0 else t


def _visit_plan(counts, tile, n_vis):
    ge = jnp.cumsum(counts)
    gs = ge - counts
    t_lo = gs // tile
    nt = jnp.where(counts > 0, (ge - 1) // tile - t_lo + 1, 0)
    vend = jnp.cumsum(nt)
    vbase = vend - nt
    n_visits = vend[-1:]
    vc = jnp.minimum(jnp.arange(n_vis, dtype=jnp.int32), n_visits[0] - 1)
    v_exp = jnp.sum((vend[None, :] <= vc[:, None]).astype(jnp.int32), axis=1)
    onehot = (v_exp[:, None] == jnp.arange(N_EXPERTS, dtype=jnp.int32)[None, :]).astype(jnp.int32)
    v_tile = jnp.sum(onehot * (t_lo - vbase)[None, :], axis=1) + vc
    gs33 = jnp.concatenate([gs, ge[-1:]])
    after = jnp.sum(onehot * vend[None, :], axis=1)
    e_after = jnp.sum((vend[None, :] <= after[:, None]).astype(jnp.int32), axis=1)
    v_next = jnp.where(after < n_visits[0], e_after, -1)
    return (v_tile.astype(jnp.int32), v_exp.astype(jnp.int32), v_next.astype(jnp.int32),
            n_visits.astype(jnp.int32), gs33.astype(jnp.int32))


def kernel(x, c, w_ada, b_ada, w_in, conv_ssd_w, conv_ssd_b, dt_bias, a_log, d_skip, ssd_norm_w,
           w_ssd_out, conv_short_w, w_short_out, b_gate, w_o, ln1_g, ln1_b, w_router, b_router,
           w_gu, b_gu, w_down, b_down, ln2_g, ln2_b):
    batch, seq, d = x.shape
    assert batch == 1 and d == D_MODEL
    depth = w_in.shape[0]
    t = seq
    xt = x.reshape(t, d)

    tm_in = _tile(t, 1024)
    rows_ssd = _tile(t, 512)
    chunk = 128
    rows_post = 256
    rows_moe = 256
    tile_e = 256
    assert t % rows_post == 0
    n_tiles = t * TOP_K // tile_e
    n_vis = n_tiles + N_EXPERTS
    n_slots = t * TOP_K + tile_e

    mods = _ada_mod(c, w_ada, b_ada)

    s0 = SSD_INNER
    s1 = s0 + SSD_INNER + 2 * SSD_GROUPS * SSD_STATE
    s2 = s1 + SSD_HEADS
    w_main = jnp.concatenate([w_in[:, :, :s1], w_in[:, :, s2:]], axis=-1).astype(BF16)
    w_dt = jnp.pad(w_in[:, :, s1:s2], ((0, 0), (0, 0), (0, LANES - SSD_HEADS)))
    w_r = jnp.pad(w_router, ((0, 0), (0, 0), (0, LANES - N_EXPERTS)))
    b_r = jnp.pad(b_router, ((0, 0), (0, LANES - N_EXPERTS)), constant_values=NEG_BIG)
    wa_bf = w_ssd_out.astype(BF16)
    wb_bf = w_short_out.astype(BF16)
    wo_bf = w_o.astype(BF16)
    warm = t * TOP_K + jnp.arange(tile_e, dtype=jnp.int32)
    cool = jnp.zeros((tile_e,), jnp.int32)

    for l in range(depth):
        m = mods[l]
        sh1, sc1, g1, sh2, sc2, g2 = [m[:, k * d:(k + 1) * d] for k in range(N_ADA)]
        u, dt_raw = _in_proj(xt, sh1, sc1, w_main[l], w_dt[l], tm_in, 1024)
        y = _ssd(u, dt_raw, conv_ssd_w[l], conv_ssd_b[l], dt_bias[l], a_log[l], d_skip[l],
                 rows_ssd, chunk)
        vecs = (ssd_norm_w[l][None, :], conv_short_w[l], b_gate[l][None, :], g1,
                ln1_g[l][None, :], ln1_b[l][None, :], sh2, sc2, b_r[l][None, :])
        x1, h2, probs, cst, sidt, runtab, cnt = _post(
            y, u, xt, vecs, (wa_bf[l], wb_bf[l], wo_bf[l], w_r[l]), rows_post)

        v_tile, v_exp, v_next, n_visits, gs = _visit_plan(cnt[0, :N_EXPERTS], tile_e, n_vis)
        runflat = jnp.concatenate([runtab[:, 0, :N_EXPERTS], cnt[:, :N_EXPERTS]]).reshape(-1)
        order = _invert(v_tile, v_exp, n_visits, gs, runflat, cst, sidt, n_tiles, tile_e,
                        rows_post)
        order_ext = jnp.concatenate([warm, order.reshape(-1), cool])
        slots = _experts(v_tile, v_exp, v_next, n_visits, gs, order_ext, h2, w_gu, b_gu, w_down,
                         b_down, l, n_slots, tile_e)
        xt = _combine(slots, probs, x1, g2, ln2_g[l][None, :], ln2_b[l][None, :], rows_moe)

    return xt.reshape(batch, seq, d)
```

```python
import functools

import jax
import jax.numpy as jnp
from jax import lax
from jax.experimental import pallas as pl
from jax.experimental.pallas import tpu as pltpu

F32 = jnp.float32
BF16 = jnp.bfloat16

D_MODEL = 1024
SSD_INNER = 2048
SSD_HEADS = 32
SSD_HEADDIM = 64
SSD_GROUPS = 4
SSD_STATE = 128
SSD_CONV = 4
SC_CONV = 3
N_EXPERTS = 32
TOP_K = 4
EXPERT_FF = 1024
SWIGLU_LIMIT = 7.0
SWIGLU_ALPHA = 1.702
DEPTH = 4
DN_ALPHA = (2.0 * DEPTH) ** 0.25
LN_EPS = 1e-5
RMS_EPS = 1e-5
N_ADA = 6

LANES = 128
SUBLANES = 8
U_MAIN = 10240
COL_Z = 0
COL_XS = 2048
COL_B = 4096
COL_C = 4608
COL_SCB = 5120
COL_SCC = 6144
COL_SCX = 7168
COL_GATE = 8192
NEG_BIG = -1e30

VMEM_LIMIT = 56 * 1024 * 1024
EXPERTS_VMEM_LIMIT = 60 * 1024 * 1024


def _sigmoid(v):
    return 1.0 / (1.0 + jnp.exp(-v))


def _softplus(v):
    return jnp.maximum(v, 0.0) + jnp.log(1.0 + jnp.exp(-jnp.abs(v)))


def _layer_norm(v):
    mu = jnp.mean(v, axis=-1, keepdims=True)
    vc = v - mu
    var = jnp.mean(vc * vc, axis=-1, keepdims=True)
    return vc * lax.rsqrt(var + LN_EPS)


def _split3(v):
    hi = v.astype(BF16)
    r1 = v - hi.astype(F32)
    mid = r1.astype(BF16)
    lo = (r1 - mid.astype(F32)).astype(BF16)
    return hi, mid, lo


def _dot(a, b):
    return jnp.dot(a, b, preferred_element_type=F32)


def _pack_bf16_pair(lo, hi):
    lo_bits = pltpu.bitcast(lo.astype(BF16).astype(F32), jnp.uint32)
    hi_bits = pltpu.bitcast(hi.astype(BF16).astype(F32), jnp.uint32)
    return lax.shift_right_logical(lo_bits, jnp.uint32(16)) | (hi_bits & jnp.uint32(0xFFFF0000))


def _unpack_bf16_pair(w):
    lo = pltpu.bitcast(lax.shift_left(w, jnp.uint32(16)), F32)
    hi = pltpu.bitcast(w & jnp.uint32(0xFFFF0000), F32)
    return lo, hi


def _dot_split(a, b):
    a_hi = a.astype(BF16)
    a_mid = (a - a_hi.astype(F32)).astype(BF16)
    b_hi = b.astype(BF16)
    b_mid = (b - b_hi.astype(F32)).astype(BF16)
    return _dot(a_hi, b_hi) + _dot(a_hi, b_mid) + _dot(a_mid, b_hi)


def _dot_exact_lhs(a_bf16, v):
    hi, mid, lo = _split3(v)
    return _dot(a_bf16, hi) + _dot(a_bf16, mid) + _dot(a_bf16, lo)


def _ada_kernel(c_ref, w_ref, b_ref, o_ref):
    c = c_ref[...]
    s = c * _sigmoid(c)
    o_ref[0] = jnp.sum(w_ref[0] * s, axis=0, keepdims=True) + b_ref[0]


def _ada_mod(c, w_ada, b_ada):
    depth, d, n = w_ada.shape
    tn = 1024
    return pl.pallas_call(
        _ada_kernel,
        grid=(depth, n // tn),
        in_specs=[
            pl.BlockSpec((d, 1), lambda l, j: (0, 0)),
            pl.BlockSpec((1, d, tn), lambda l, j: (l, 0, j)),
            pl.BlockSpec((1, 1, tn), lambda l, j: (l, 0, j)),
        ],
        out_specs=pl.BlockSpec((1, 1, tn), lambda l, j: (l, 0, j)),
        out_shape=jax.ShapeDtypeStruct((depth, 1, n), F32),
        compiler_params=pltpu.CompilerParams(
            dimension_semantics=("arbitrary", "arbitrary"), vmem_limit_bytes=VMEM_LIMIT),
        name="ada_mod",
    )(c.reshape(d, 1), w_ada, b_ada.reshape(depth, 1, n))


def _inproj_kernel(x_ref, sh_ref, sc_ref, w_ref, wdt_ref, u_ref, dt_ref, h_scr):
    @pl.when(pl.program_id(1) == 0)
    def _():
        h = _layer_norm(x_ref[...]) * (1.0 + sc_ref[...]) + sh_ref[...]
        h_scr[...] = h.astype(BF16)
        dt_ref[...] = _dot_split(h, wdt_ref[...])

    u_ref[...] = _dot(h_scr[...], w_ref[...]).astype(BF16)


def _in_proj(x, sh, sc, w_main, w_dt, tm, tn):
    t, d = x.shape
    n = w_main.shape[1]
    return pl.pallas_call(
        _inproj_kernel,
        grid=(t // tm, n // tn),
        in_specs=[
            pl.BlockSpec((tm, d), lambda i, j: (i, 0)),
            pl.BlockSpec((1, d), lambda i, j: (0, 0)),
            pl.BlockSpec((1, d), lambda i, j: (0, 0)),
            pl.BlockSpec((d, tn), lambda i, j: (0, j)),
            pl.BlockSpec((d, LANES), lambda i, j: (0, 0)),
        ],
        out_specs=[
            pl.BlockSpec((tm, tn), lambda i, j: (i, j)),
            pl.BlockSpec((tm, LANES), lambda i, j: (i, 0)),
        ],
        out_shape=[
            jax.ShapeDtypeStruct((t, n), BF16),
            jax.ShapeDtypeStruct((t, LANES), F32),
        ],
        scratch_shapes=[pltpu.VMEM((tm, d), BF16)],
        compiler_params=pltpu.CompilerParams(
            dimension_semantics=("arbitrary", "arbitrary"), vmem_limit_bytes=VMEM_LIMIT),
        name="in_proj",
    )(x, sh, sc, w_main, w_dt)


def _ssd_kernel(xs_ref, b_ref, c_ref, dt_ref, cwx_ref, cwb_ref, cwc_ref, cbx_ref, cbb_ref,
                cbc_ref, dtb_ref, alog_ref, dskip_ref, y_ref,
                xbuf, bbuf, cbuf, xcs, bcs, ccs, state, *, chunk, rows):
    @pl.when(pl.program_id(0) == 0)
    def _():
        xbuf[0:SUBLANES, :] = jnp.zeros((SUBLANES, xbuf.shape[1]), F32)
        bbuf[0:SUBLANES, :] = jnp.zeros((SUBLANES, bbuf.shape[1]), F32)
        cbuf[0:SUBLANES, :] = jnp.zeros((SUBLANES, cbuf.shape[1]), F32)
        state[...] = jnp.zeros(state.shape, F32)

    def conv_silu(in_ref, buf, w_ref, bias_ref, out_scr):
        buf[SUBLANES:SUBLANES + rows, :] = in_ref[...].astype(F32)
        acc = bias_ref[...]
        for j in range(SSD_CONV):
            off = SUBLANES - (SSD_CONV - 1) + j
            acc = acc + w_ref[j:j + 1, :] * buf[off:off + rows, :]
        out_scr[...] = acc * _sigmoid(acc)
        buf[0:SUBLANES, :] = buf[rows:rows + SUBLANES, :]

    conv_silu(xs_ref, xbuf, cwx_ref, cbx_ref, xcs)
    conv_silu(b_ref, bbuf, cwb_ref, cbb_ref, bcs)
    conv_silu(c_ref, cbuf, cwc_ref, cbc_ref, ccs)

    li = lax.broadcasted_iota(jnp.int32, (chunk, chunk), 0)
    si = lax.broadcasted_iota(jnp.int32, (chunk, chunk), 1)
    causal = li >= si
    tri = jnp.where(causal, 1.0, 0.0).astype(BF16)
    first_half = lax.broadcasted_iota(jnp.int32, (1, LANES), 1) < SSD_HEADDIM
    a_row = -jnp.exp(alog_ref[...])
    heads_per_group = SSD_HEADS // SSD_GROUPS

    def chunk_body(ci, carry):
        r0 = pl.multiple_of(ci * chunk, chunk)
        dt = _softplus(dt_ref[pl.ds(r0, chunk), :] + dtb_ref[...])
        la = dt * a_row
        acum = _dot_exact_lhs(tri, la)
        acum_t = acum.T
        dt_t = dt.T
        last_t = acum_t[:, chunk - 1:chunk]
        w_t = jnp.exp(last_t - acum_t) * dt_t
        cdec_t = jnp.exp(last_t)

        cb = []
        bt = []
        cg = []
        for g in range(SSD_GROUPS):
            bg = bcs[pl.ds(r0, chunk), g * SSD_STATE:(g + 1) * SSD_STATE]
            cgv = ccs[pl.ds(r0, chunk), g * SSD_STATE:(g + 1) * SSD_STATE]
            btg = bg.T
            bt.append(btg)
            cg.append(cgv)
            cb.append(_dot(cgv.astype(BF16), btg.astype(BF16)))

        for pr in range(SSD_HEADS // 2):
            lo = pr * LANES
            xs_pair = xcs[pl.ds(r0, chunk), lo:lo + LANES]
            xs_bf = xs_pair.astype(BF16)
            prev = state[:, lo:lo + LANES]
            lhs_m, lhs_c, lhs_b, cd = [], [], [], []
            for h in (2 * pr, 2 * pr + 1):
                g = h // heads_per_group
                col = jnp.broadcast_to(acum[:, h:h + 1], (chunk, chunk))
                row = acum_t[h:h + 1, :]
                dec = jnp.exp(jnp.where(causal, col - row, NEG_BIG))
                lhs_m.append((cb[g] * dec * dt_t[h:h + 1, :]).astype(BF16))
                lhs_c.append((cg[g] * jnp.exp(col)).astype(BF16))
                lhs_b.append((bt[g] * w_t[h:h + 1, :]).astype(BF16))
                cd.append(cdec_t[h:h + 1, :])
            out = (_dot(jnp.concatenate(lhs_m, axis=0), xs_bf)
                   + _dot(jnp.concatenate(lhs_c, axis=0), prev.astype(BF16)))
            y_pair = jnp.where(first_half, out[0:chunk, :], out[chunk:2 * chunk, :])
            y_ref[pl.ds(r0, chunk), lo:lo + LANES] = (
                y_pair + xs_pair * dskip_ref[:, lo:lo + LANES]).astype(y_ref.dtype)
            st = _dot(jnp.concatenate(lhs_b, axis=0), xs_bf)
            cd_pair = jnp.where(first_half, cd[0], cd[1])
            state[:, lo:lo + LANES] = prev * cd_pair + jnp.where(
                first_half, st[0:SSD_STATE, :], st[SSD_STATE:2 * SSD_STATE, :])
        return carry

    lax.fori_loop(0, rows // chunk, chunk_body, 0)


def _ssd(u, dt_raw, cw, cb, dt_bias, a_log, d_skip, rows, chunk):
    t = u.shape[0]
    gn = SSD_GROUPS * SSD_STATE
    pad = LANES - SSD_HEADS
    assert chunk == SSD_STATE
    kern = functools.partial(_ssd_kernel, chunk=chunk, rows=rows)
    full = lambda shape: pl.BlockSpec(shape, lambda i: (0, 0))
    return pl.pallas_call(
        kern,
        grid=(t // rows,),
        in_specs=[
            pl.BlockSpec((rows, SSD_INNER), lambda i: (i, COL_XS // SSD_INNER)),
            pl.BlockSpec((rows, gn), lambda i: (i, COL_B // gn)),
            pl.BlockSpec((rows, gn), lambda i: (i, COL_C // gn)),
            pl.BlockSpec((rows, LANES), lambda i: (i, 0)),
            full((SSD_CONV, SSD_INNER)), full((SSD_CONV, gn)), full((SSD_CONV, gn)),
            full((1, SSD_INNER)), full((1, gn)), full((1, gn)),
            full((1, LANES)), full((1, LANES)), full((1, SSD_INNER)),
        ],
        out_specs=pl.BlockSpec((rows, SSD_INNER), lambda i: (i, 0)),
        out_shape=jax.ShapeDtypeStruct((t, SSD_INNER), BF16),
        scratch_shapes=[
            pltpu.VMEM((rows + SUBLANES, SSD_INNER), F32),
            pltpu.VMEM((rows + SUBLANES, gn), F32),
            pltpu.VMEM((rows + SUBLANES, gn), F32),
            pltpu.VMEM((rows, SSD_INNER), F32),
            pltpu.VMEM((rows, gn), F32),
            pltpu.VMEM((rows, gn), F32),
            pltpu.VMEM((SSD_STATE, SSD_INNER), F32),
        ],
        compiler_params=pltpu.CompilerParams(
            dimension_semantics=("arbitrary",), vmem_limit_bytes=VMEM_LIMIT),
        name="ssd",
    )(u, u, u, dt_raw,
      cw[:, :SSD_INNER], cw[:, SSD_INNER:SSD_INNER + gn], cw[:, SSD_INNER + gn:],
      cb[None, :SSD_INNER], cb[None, SSD_INNER:SSD_INNER + gn], cb[None, SSD_INNER + gn:],
      jnp.pad(dt_bias, (0, pad))[None, :], jnp.pad(a_log, (0, pad))[None, :],
      jnp.repeat(d_skip, SSD_HEADDIM)[None, :])


def _post_kernel(y_ref, z_ref, scb_ref, scc_ref, scx_ref, gate_ref, x_ref,
                 nw_ref, wa_ref, csw_ref, wb_ref, bg_ref, wo_ref, g1_ref, lng_ref, lnb_ref,
                 sh2_ref, sc2_ref, wr_ref, br_ref,
                 x1_ref, h2_ref, prob_ref, cst_ref, sidt_ref, runtab_ref, cnt_ref,
                 sbuf, run, *, rows, n_tok):
    @pl.when(pl.program_id(0) == 0)
    def _():
        sbuf[0:SUBLANES, :] = jnp.zeros((SUBLANES, sbuf.shape[1]), F32)
        run[...] = jnp.zeros(run.shape, F32)

    z = z_ref[...].astype(F32)
    yg = y_ref[...].astype(F32) * (z * _sigmoid(z))
    ms = jnp.mean(yg * yg, axis=-1, keepdims=True)
    yn = yg * lax.rsqrt(ms + RMS_EPS) * nw_ref[...]
    u_a = _dot(yn.astype(BF16), wa_ref[...])

    sbuf[SUBLANES:SUBLANES + rows, :] = scc_ref[...].astype(F32) * scx_ref[...].astype(F32)
    v = jnp.zeros((rows, D_MODEL), F32)
    for j in range(SC_CONV):
        off = SUBLANES - (SC_CONV - 1) + j
        v = v + csw_ref[j:j + 1, :] * sbuf[off:off + rows, :]
    sbuf[0:SUBLANES, :] = sbuf[rows:rows + SUBLANES, :]
    u_b = _dot((scb_ref[...].astype(F32) * v).astype(BF16), wb_ref[...])

    gl = gate_ref[...].astype(F32) + bg_ref[...]
    merged = _sigmoid(gl[:, :D_MODEL]) * u_a + _sigmoid(gl[:, D_MODEL:]) * u_b
    mix = _dot(merged.astype(BF16), wo_ref[...])
    x1 = _layer_norm(DN_ALPHA * x_ref[...] + g1_ref[...] * mix) * lng_ref[...] + lnb_ref[...]
    x1_ref[...] = x1
    h2 = _layer_norm(x1) * (1.0 + sc2_ref[...]) + sh2_ref[...]
    half = D_MODEL // 2
    h2_ref[...] = _pack_bf16_pair(h2[:, :half], h2[:, half:])

    logits = _dot_split(h2, wr_ref[...]) + br_ref[...]
    lane = lax.broadcasted_iota(jnp.int32, (rows, LANES), 1).astype(F32)
    work = logits
    onehots, vals = [], []
    for k in range(TOP_K):
        m = jnp.max(work, axis=-1, keepdims=True)
        ik = jnp.min(jnp.where(work == m, lane, float(LANES)), axis=-1, keepdims=True)
        oh = lane == ik
        onehots.append(oh)
        vals.append(m)
        work = jnp.where(oh, -jnp.inf, work)
    es = [jnp.exp(vk - vals[0]) for vk in vals]
    denom = es[0] + es[1] + es[2] + es[3]
    prob_out = jnp.zeros((rows, LANES), F32)
    for k in range(TOP_K):
        prob_out = jnp.where(lane == float(k), es[k] / denom, prob_out)

    sel = jnp.zeros((rows, LANES), F32)
    kk = jnp.zeros((rows, LANES), F32)
    for k, oh in enumerate(onehots):
        sel = sel + jnp.where(oh, 1.0, 0.0)
        kk = kk + jnp.where(oh, float(k), 0.0)
    ri = lax.broadcasted_iota(jnp.int32, (rows, rows), 0)
    rj = lax.broadcasted_iota(jnp.int32, (rows, rows), 1)
    strict = jnp.where(ri > rj, 1.0, 0.0).astype(BF16)
    base = _dot(strict, sel.astype(BF16)) + run[...]
    tok = (pl.program_id(0) * rows
           + lax.broadcasted_iota(jnp.int32, (rows, LANES), 0)).astype(F32)
    picked = sel > 0.0
    cs = jnp.where(picked, base + 1.0, 0.0)
    sid = jnp.where(picked, kk * float(n_tok) + tok, 0.0)
    cst_ref[...] = cs.T[:N_EXPERTS, :]
    sidt_ref[...] = sid.T[:N_EXPERTS, :]
    runtab_ref[0] = run[...].astype(jnp.int32)
    run[...] = run[...] + jnp.sum(sel, axis=0, keepdims=True)
    prob_ref[...] = prob_out
    cnt_ref[...] = run[...].astype(jnp.int32)


def _post(y, u, x, vecs, mats, rows):
    t = x.shape[0]
    kern = functools.partial(_post_kernel, rows=rows, n_tok=t)
    row = lambda w: pl.BlockSpec((1, w), lambda i: (0, 0))
    mat = lambda a, b: pl.BlockSpec((a, b), lambda i: (0, 0))
    ublk = lambda w, col: pl.BlockSpec((rows, w), lambda i: (i, col // w))
    nw, csw, bg, g1, lng, lnb, sh2, sc2, br = vecs
    wa, wb, wo, wr = mats
    tok = pl.BlockSpec((rows, LANES), lambda i: (i, 0))
    return pl.pallas_call(
        kern,
        grid=(t // rows,),
        in_specs=[
            pl.BlockSpec((rows, SSD_INNER), lambda i: (i, 0)),
            ublk(SSD_INNER, COL_Z), ublk(D_MODEL, COL_SCB), ublk(D_MODEL, COL_SCC),
            ublk(D_MODEL, COL_SCX), ublk(2 * D_MODEL, COL_GATE),
            pl.BlockSpec((rows, D_MODEL), lambda i: (i, 0)),
            row(SSD_INNER), mat(SSD_INNER, D_MODEL), mat(SC_CONV, D_MODEL), mat(D_MODEL, D_MODEL),
            row(2 * D_MODEL), mat(D_MODEL, D_MODEL), row(D_MODEL), row(D_MODEL), row(D_MODEL),
            row(D_MODEL), row(D_MODEL), mat(D_MODEL, LANES), row(LANES),
        ],
        out_specs=[
            pl.BlockSpec((rows, D_MODEL), lambda i: (i, 0)),
            pl.BlockSpec((rows, D_MODEL // 2), lambda i: (i, 0)),
            tok,
            pl.BlockSpec((N_EXPERTS, rows), lambda i: (0, i)),
            pl.BlockSpec((N_EXPERTS, rows), lambda i: (0, i)),
            pl.BlockSpec((1, 1, LANES), lambda i: (i, 0, 0)),
            pl.BlockSpec((1, LANES), lambda i: (0, 0)),
        ],
        out_shape=[
            jax.ShapeDtypeStruct((t, D_MODEL), F32),
            jax.ShapeDtypeStruct((t, D_MODEL // 2), jnp.uint32),
            jax.ShapeDtypeStruct((t, LANES), F32),
            jax.ShapeDtypeStruct((N_EXPERTS, t), F32),
            jax.ShapeDtypeStruct((N_EXPERTS, t), F32),
            jax.ShapeDtypeStruct((t // rows, 1, LANES), jnp.int32),
            jax.ShapeDtypeStruct((1, LANES), jnp.int32),
        ],
        scratch_shapes=[
            pltpu.VMEM((rows + SUBLANES, D_MODEL), F32),
            pltpu.VMEM((1, LANES), F32),
        ],
        compiler_params=pltpu.CompilerParams(
            dimension_semantics=("arbitrary",), vmem_limit_bytes=VMEM_LIMIT),
        name="post_mix",
    )(y, u, u, u, u, u, x, nw, wa, csw, wb, bg, wo, g1, lng, lnb, sh2, sc2, wr, br)


def _invert_kernel(vt_ref, ve_ref, nv_ref, gs_ref, run_ref, cst_ref, sidt_ref, o_ref, ptr,
                   *, tile, tok_tile, n_tok_tiles):
    v = pl.program_id(0)
    i = vt_ref[v]
    e = ve_ref[v]
    vp = jnp.maximum(v - 1, 0)
    new_tile = jnp.logical_or(v == 0, vt_ref[vp] != i)
    new_exp = jnp.logical_or(v == 0, ve_ref[vp] != e)

    @pl.when(new_exp)
    def _():
        ptr[0] = 0

    @pl.when(new_tile)
    def _():
        o_ref[...] = jnp.zeros(o_ref.shape, jnp.int32)

    @pl.when(v < nv_ref[0])
    def _():
        g0 = gs_ref[e]
        row0 = i * tile
        ra = jnp.maximum(g0, row0) - g0
        rb = jnp.minimum(gs_ref[e + 1], row0 + tile) - g0
        b_lo = lax.while_loop(lambda b: run_ref[(b + 1) * N_EXPERTS + e] <= ra,
                              lambda b: b + 1, ptr[0])
        ptr[0] = b_lo
        b_hi = lax.while_loop(
            lambda b: jnp.logical_and(b < n_tok_tiles, run_ref[b * N_EXPERTS + e] < rb),
            lambda b: b + 1, b_lo)
        for rblk in range(tile // LANES):
            base = (row0 - g0 + 1 + rblk * LANES).astype(F32)
            want = base + lax.broadcasted_iota(jnp.int32, (LANES, LANES), 0).astype(F32)

            def body(b, acc, want=want):
                c0 = pl.multiple_of(b * tok_tile, tok_tile)
                cs_row = cst_ref[pl.ds(e, 1), pl.ds(c0, tok_tile)]
                sid_row = sidt_ref[pl.ds(e, 1), pl.ds(c0, tok_tile)]
                for j in range(tok_tile // LANES):
                    lanes = slice(j * LANES, (j + 1) * LANES)
                    acc = acc + jnp.where(cs_row[:, lanes] == want, sid_row[:, lanes], 0.0)
                return acc

            acc = lax.fori_loop(b_lo, b_hi, body, jnp.zeros((LANES, LANES), F32))
            contrib = jnp.sum(acc.T, axis=0, keepdims=True)
            lanes = slice(rblk * LANES, (rblk + 1) * LANES)
            o_ref[0, :, lanes] = o_ref[0, :, lanes] + contrib.astype(jnp.int32)


def _invert(v_tile, v_exp, n_visits, gs, runflat, cst, sidt, n_tiles, tile, tok_tile):
    t = cst.shape[1]
    n_vis = v_tile.shape[0]
    kern = functools.partial(_invert_kernel, tile=tile, tok_tile=tok_tile,
                             n_tok_tiles=t // tok_tile)
    return pl.pallas_call(
        kern,
        grid_spec=pltpu.PrefetchScalarGridSpec(
            num_scalar_prefetch=5,
            grid=(n_vis,),
            in_specs=[
                pl.BlockSpec((N_EXPERTS, t), lambda v, *_: (0, 0)),
                pl.BlockSpec((N_EXPERTS, t), lambda v, *_: (0, 0)),
            ],
            out_specs=pl.BlockSpec((1, 1, tile), lambda v, vt, *_: (vt[v], 0, 0)),
            scratch_shapes=[pltpu.SMEM((1,), jnp.int32)],
        ),
        out_shape=jax.ShapeDtypeStruct((n_tiles, 1, tile), jnp.int32),
        compiler_params=pltpu.CompilerParams(
            dimension_semantics=("arbitrary",), vmem_limit_bytes=VMEM_LIMIT),
        name="moe_invert",
    )(v_tile, v_exp, n_visits, gs, runflat, cst, sidt)


N_FF_BLK = 4
FF_BLK = EXPERT_FF // N_FF_BLK
N_STAGES = 2 * N_FF_BLK


def _experts_kernel(vt_ref, ve_ref, vn_ref, nv_ref, gs_ref,
                    ord_cur, ord_next, h_hbm, wgu_hbm, bgu_ref, wd_hbm, bd_ref,
                    slots_hbm, h_vmem, x_a, x_b, o_a, o_b, act, wgu_st, wd_st, wgu_bf, wd_bf,
                    hsem, wsem, ssem, *, tile, n_tok, layer):
    v = pl.program_id(0)
    nv = nv_ref[0]
    i = vt_ref[v]
    e = ve_ref[v]
    vp = jnp.maximum(v - 1, 0)
    active = v < nv
    new_tile = jnp.logical_or(v == 0, vt_ref[vp] != i)
    new_exp = jnp.logical_or(v == 0, ve_ref[vp] != e)
    tile_done = jnp.logical_or(v == nv - 1, vt_ref[jnp.minimum(v + 1, vt_ref.shape[0] - 1)] != i)
    even = lax.rem(i, 2) == 0
    half = D_MODEL // 2
    xs = (x_a, x_b)
    os_ = (o_a, o_b)

    def weight_copies(expert):
        return (pltpu.make_async_copy(wgu_hbm.at[layer, expert], wgu_st, wsem.at[0]),
                pltpu.make_async_copy(wd_hbm.at[layer, expert], wd_st, wsem.at[1]))

    def fetch_row(order_ref, r, dst):
        sid = order_ref[r]
        tok = sid & (n_tok - 1) if n_tok & (n_tok - 1) == 0 else lax.rem(sid, n_tok)
        dst[pl.ds(r, 1), :] = h_vmem[pl.ds(tok, 1), :]

    def scatter_row(order_ref, r, src, p):
        return pltpu.make_async_copy(src.at[pl.ds(r, 1)],
                                     slots_hbm.at[pl.ds(order_ref[r], 1)], ssem.at[p])

    @pl.when(v == 0)
    def _():
        tokens = pltpu.make_async_copy(h_hbm, h_vmem, hsem)
        tokens.start()
        for c in weight_copies(e):
            c.start()
        tokens.wait()
        for r in range(tile):
            fetch_row(ord_cur, r, x_a)

    @pl.when(jnp.logical_and(active, new_exp))
    def _():
        for c in weight_copies(e):
            c.wait()
        wgu_bf[...] = wgu_st[...].astype(BF16)
        wd_bf[...] = wd_st[...].astype(BF16)

        @pl.when(vn_ref[v] >= 0)
        def _():
            for c in weight_copies(vn_ref[v]):
                c.start()

    rowpos = i * tile + lax.broadcasted_iota(jnp.int32, (tile, 1), 0)
    mine = jnp.logical_and(rowpos >= gs_ref[e], rowpos < gs_ref[e + 1])

    def compute(x_ref, o_ref, first, between):
        x_lo, x_hi = _unpack_bf16_pair(x_ref[...])
        xb = jnp.concatenate([x_lo.astype(BF16), x_hi.astype(BF16)], axis=1)
        for cb in range(N_FF_BLK):
            c0 = cb * FF_BLK
            g = _dot(xb, wgu_bf[:, c0:c0 + FF_BLK]) + bgu_ref[:, c0:c0 + FF_BLK]
            u = (_dot(xb, wgu_bf[:, EXPERT_FF + c0:EXPERT_FF + c0 + FF_BLK])
                 + bgu_ref[:, EXPERT_FF + c0:EXPERT_FF + c0 + FF_BLK])
            g = jnp.minimum(g, SWIGLU_LIMIT)
            u = jnp.clip(u, -SWIGLU_LIMIT, SWIGLU_LIMIT)
            act[:, c0:c0 + FF_BLK] = ((u + 1.0) * g * _sigmoid(SWIGLU_ALPHA * g)).astype(BF16)
            between(cb)
        for pb in range(N_FF_BLK // 2):
            c0 = pb * FF_BLK
            o_lo = _dot(act[...], wd_bf[:, c0:c0 + FF_BLK]) + bd_ref[:, c0:c0 + FF_BLK]
            between(N_FF_BLK + 2 * pb)
            o_hi = (_dot(act[...], wd_bf[:, half + c0:half + c0 + FF_BLK])
                    + bd_ref[:, half + c0:half + c0 + FF_BLK])
            packed = _pack_bf16_pair(o_lo, o_hi)
            keep = jnp.uint32(0) if first else o_ref[:, c0:c0 + FF_BLK]
            o_ref[:, c0:c0 + FF_BLK] = jnp.where(mine, packed, keep)
            between(N_FF_BLK + 2 * pb + 1)

    per_stage = tile // N_STAGES

    for p in range(2):
        x_cur, x_nxt = xs[p], xs[1 - p]
        o_cur, o_oth = os_[p], os_[1 - p]
        on_parity = even if p == 0 else jnp.logical_not(even)

        @pl.when(jnp.logical_and(jnp.logical_and(active, new_tile), on_parity))
        def _(x_cur=x_cur, x_nxt=x_nxt, o_cur=o_cur, p=p):
            @pl.when(i >= 2)
            def _():
                for r in range(tile):
                    scatter_row(ord_cur, r, o_cur, p).wait()

            def between(j):
                for r in range(j * per_stage, (j + 1) * per_stage):
                    fetch_row(ord_next, r, x_nxt)

            compute(x_cur, o_cur, True, between)

        @pl.when(jnp.logical_and(jnp.logical_and(active, jnp.logical_not(new_tile)), on_parity))
        def _(x_cur=x_cur, o_cur=o_cur):
            compute(x_cur, o_cur, False, lambda j: None)

        @pl.when(jnp.logical_and(jnp.logical_and(active, tile_done), on_parity))
        def _(o_cur=o_cur, p=p):
            for r in range(tile):
                scatter_row(ord_cur, r, o_cur, p).start(priority=r % 2)

        @pl.when(jnp.logical_and(v == nv - 1, on_parity))
        def _(o_cur=o_cur, o_oth=o_oth, p=p):
            for r in range(tile):
                scatter_row(ord_cur, r, o_cur, p).wait()

            @pl.when(i >= 1)
            def _():
                for r in range(tile):
                    scatter_row(ord_cur, r, o_oth, 1 - p).wait()


def _experts(v_tile, v_exp, v_next, n_visits, gs, order_ext, h2, w_gu, b_gu, w_down, b_down,
             layer, n_slots, tile):
    n_vis = v_tile.shape[0]
    n_tok = h2.shape[0]
    ff2 = 2 * EXPERT_FF
    half = D_MODEL // 2
    kern = functools.partial(_experts_kernel, tile=tile, n_tok=n_tok, layer=layer)
    smem_blk = lambda off: pl.BlockSpec(
        (tile,), lambda v, vt, *_: (vt[v] + off,), memory_space=pltpu.SMEM)
    bias = lambda n: pl.BlockSpec(
        (None, None, 1, n), lambda v, vt, ve, *_: (layer, ve[v], 0, 0))
    hbm = pl.BlockSpec(memory_space=pl.ANY)
    return pl.pallas_call(
        kern,
        grid_spec=pltpu.PrefetchScalarGridSpec(
            num_scalar_prefetch=5,
            grid=(n_vis,),
            in_specs=[smem_blk(0), smem_blk(1), hbm, hbm, bias(ff2), hbm, bias(D_MODEL)],
            out_specs=hbm,
            scratch_shapes=[
                pltpu.VMEM((n_tok, half), jnp.uint32),
                pltpu.VMEM((tile, half), jnp.uint32),
                pltpu.VMEM((tile, half), jnp.uint32),
                pltpu.VMEM((tile, half), jnp.uint32),
                pltpu.VMEM((tile, half), jnp.uint32),
                pltpu.VMEM((tile, EXPERT_FF), BF16),
                pltpu.VMEM((D_MODEL, ff2), F32),
                pltpu.VMEM((EXPERT_FF, D_MODEL), F32),
                pltpu.VMEM((D_MODEL, ff2), BF16),
                pltpu.VMEM((EXPERT_FF, D_MODEL), BF16),
                pltpu.SemaphoreType.DMA(()),
                pltpu.SemaphoreType.DMA((2,)),
                pltpu.SemaphoreType.DMA((2,)),
            ],
        ),
        out_shape=jax.ShapeDtypeStruct((n_slots, half), jnp.uint32),
        compiler_params=pltpu.CompilerParams(
            dimension_semantics=("arbitrary",), vmem_limit_bytes=EXPERTS_VMEM_LIMIT),
        name="moe_experts",
    )(v_tile, v_exp, v_next, n_visits, gs, order_ext, order_ext, h2,
      w_gu, b_gu[:, :, None, :], w_down, b_down[:, :, None, :])


def _combine_kernel(s0_ref, s1_ref, s2_ref, s3_ref, prob_ref, x1_ref, g2_ref, lng_ref, lnb_ref,
                    x2_ref):
    prob = prob_ref[...]
    ffn_lo = jnp.zeros((x1_ref.shape[0], D_MODEL // 2), F32)
    ffn_hi = jnp.zeros((x1_ref.shape[0], D_MODEL // 2), F32)
    for k, s_ref in enumerate((s0_ref, s1_ref, s2_ref, s3_ref)):
        lo, hi = _unpack_bf16_pair(s_ref[...])
        ffn_lo = ffn_lo + prob[:, k:k + 1] * lo
        ffn_hi = ffn_hi + prob[:, k:k + 1] * hi
    ffn = jnp.concatenate([ffn_lo, ffn_hi], axis=1)
    x2_ref[...] = (_layer_norm(DN_ALPHA * x1_ref[...] + g2_ref[...] * ffn) * lng_ref[...]
                   + lnb_ref[...])


def _combine(slots, probs, x1, g2, lng, lnb, rows):
    t = x1.shape[0]
    row = pl.BlockSpec((1, D_MODEL), lambda i: (0, 0))
    slot_blk = lambda k: pl.BlockSpec((rows, D_MODEL // 2), lambda i: (k * (t // rows) + i, 0))
    return pl.pallas_call(
        _combine_kernel,
        grid=(t // rows,),
        in_specs=[
            slot_blk(0), slot_blk(1), slot_blk(2), slot_blk(3),
            pl.BlockSpec((rows, LANES), lambda i: (i, 0)),
            pl.BlockSpec((rows, D_MODEL), lambda i: (i, 0)),
            row, row, row,
        ],
        out_specs=pl.BlockSpec((rows, D_MODEL), lambda i: (i, 0)),
        out_shape=jax.ShapeDtypeStruct((t, D_MODEL), F32),
        compiler_params=pltpu.CompilerParams(
            dimension_semantics=("arbitrary",), vmem_limit_bytes=VMEM_LIMIT),
        name="moe_combine",
    )(slots, slots, slots, slots, probs, x1, g2, lng, lnb)


def _tile(t, pref):
    return pref if t % pref == 0 else t


def _visit_plan(counts, tile, n_vis):
    ge = jnp.cumsum(counts)
    gs = ge - counts
    t_lo = gs // tile
    nt = jnp.where(counts > 0, (ge - 1) // tile - t_lo + 1, 0)
    vend = jnp.cumsum(nt)
    vbase = vend - nt
    n_visits = vend[-1:]
    vc = jnp.minimum(jnp.arange(n_vis, dtype=jnp.int32), n_visits[0] - 1)
    v_exp = jnp.sum((vend[None, :] <= vc[:, None]).astype(jnp.int32), axis=1)
    onehot = (v_exp[:, None] == jnp.arange(N_EXPERTS, dtype=jnp.int32)[None, :]).astype(jnp.int32)
    v_tile = jnp.sum(onehot * (t_lo - vbase)[None, :], axis=1) + vc
    gs33 = jnp.concatenate([gs, ge[-1:]])
    after = jnp.sum(onehot * vend[None, :], axis=1)
    e_after = jnp.sum((vend[None, :] <= after[:, None]).astype(jnp.int32), axis=1)
    v_next = jnp.where(after < n_visits[0], e_after, -1)
    return (v_tile.astype(jnp.int32), v_exp.astype(jnp.int32), v_next.astype(jnp.int32),
            n_visits.astype(jnp.int32), gs33.astype(jnp.int32))


def kernel(x, c, w_ada, b_ada, w_in, conv_ssd_w, conv_ssd_b, dt_bias, a_log, d_skip, ssd_norm_w,
           w_ssd_out, conv_short_w, w_short_out, b_gate, w_o, ln1_g, ln1_b, w_router, b_router,
           w_gu, b_gu, w_down, b_down, ln2_g, ln2_b):
    batch, seq, d = x.shape
    assert batch == 1 and d == D_MODEL
    depth = w_in.shape[0]
    t = seq
    xt = x.reshape(t, d)

    tm_in = _tile(t, 1024)
    rows_ssd = _tile(t, 512)
    chunk = 128
    rows_post = 256
    rows_moe = 256
    tile_e = 256
    assert t % rows_post == 0
    n_tiles = t * TOP_K // tile_e
    n_vis = n_tiles + N_EXPERTS
    n_slots = t * TOP_K

    mods = _ada_mod(c, w_ada, b_ada)

    s0 = SSD_INNER
    s1 = s0 + SSD_INNER + 2 * SSD_GROUPS * SSD_STATE
    s2 = s1 + SSD_HEADS
    w_main = jnp.concatenate([w_in[:, :, :s1], w_in[:, :, s2:]], axis=-1).astype(BF16)
    w_dt = jnp.pad(w_in[:, :, s1:s2], ((0, 0), (0, 0), (0, LANES - SSD_HEADS)))
    w_r = jnp.pad(w_router, ((0, 0), (0, 0), (0, LANES - N_EXPERTS)))
    b_r = jnp.pad(b_router, ((0, 0), (0, LANES - N_EXPERTS)), constant_values=NEG_BIG)
    wa_bf = w_ssd_out.astype(BF16)
    wb_bf = w_short_out.astype(BF16)
    wo_bf = w_o.astype(BF16)
    cool = jnp.zeros((tile_e,), jnp.int32)

    for l in range(depth):
        m = mods[l]
        sh1, sc1, g1, sh2, sc2, g2 = [m[:, k * d:(k + 1) * d] for k in range(N_ADA)]
        u, dt_raw = _in_proj(xt, sh1, sc1, w_main[l], w_dt[l], tm_in, 1024)
        y = _ssd(u, dt_raw, conv_ssd_w[l], conv_ssd_b[l], dt_bias[l], a_log[l], d_skip[l],
                 rows_ssd, chunk)
        vecs = (ssd_norm_w[l][None, :], conv_short_w[l], b_gate[l][None, :], g1,
                ln1_g[l][None, :], ln1_b[l][None, :], sh2, sc2, b_r[l][None, :])
        x1, h2, probs, cst, sidt, runtab, cnt = _post(
            y, u, xt, vecs, (wa_bf[l], wb_bf[l], wo_bf[l], w_r[l]), rows_post)

        v_tile, v_exp, v_next, n_visits, gs = _visit_plan(cnt[0, :N_EXPERTS], tile_e, n_vis)
        runflat = jnp.concatenate([runtab[:, 0, :N_EXPERTS], cnt[:, :N_EXPERTS]]).reshape(-1)
        order = _invert(v_tile, v_exp, n_visits, gs, runflat, cst, sidt, n_tiles, tile_e,
                        rows_post)
        order_ext = jnp.concatenate([order.reshape(-1), cool])
        slots = _experts(v_tile, v_exp, v_next, n_visits, gs, order_ext, h2, w_gu, b_gu, w_down,
                         b_down, l, n_slots, tile_e)
        xt = _combine(slots, probs, x1, g2, ln2_g[l][None, :], ln2_b[l][None, :], rows_moe)

    return xt.reshape(batch, seq, d)
```

```python
import functools

import jax
import jax.numpy as jnp
from jax import lax
from jax.experimental import pallas as pl
from jax.experimental.pallas import tpu as pltpu

F32 = jnp.float32
BF16 = jnp.bfloat16

D_MODEL = 1024
SSD_INNER = 2048
SSD_HEADS = 32
SSD_HEADDIM = 64
SSD_GROUPS = 4
SSD_STATE = 128
SSD_CONV = 4
SC_CONV = 3
N_EXPERTS = 32
TOP_K = 4
EXPERT_FF = 1024
SWIGLU_LIMIT = 7.0
SWIGLU_ALPHA = 1.702
DEPTH = 4
DN_ALPHA = (2.0 * DEPTH) ** 0.25
LN_EPS = 1e-5
RMS_EPS = 1e-5
N_ADA = 6

LANES = 128
SUBLANES = 8
U_MAIN = 10240
COL_Z = 0
COL_XS = 2048
COL_B = 4096
COL_C = 4608
COL_SCB = 5120
COL_SCC = 6144
COL_SCX = 7168
COL_GATE = 8192
NEG_BIG = -1e30

VMEM_LIMIT = 56 * 1024 * 1024
EXPERTS_VMEM_LIMIT = 60 * 1024 * 1024


def _sigmoid(v):
    return 1.0 / (1.0 + jnp.exp(-v))


def _softplus(v):
    return jnp.maximum(v, 0.0) + jnp.log(1.0 + jnp.exp(-jnp.abs(v)))


def _layer_norm(v):
    mu = jnp.mean(v, axis=-1, keepdims=True)
    vc = v - mu
    var = jnp.mean(vc * vc, axis=-1, keepdims=True)
    return vc * lax.rsqrt(var + LN_EPS)


def _split3(v):
    hi = v.astype(BF16)
    r1 = v - hi.astype(F32)
    mid = r1.astype(BF16)
    lo = (r1 - mid.astype(F32)).astype(BF16)
    return hi, mid, lo


def _dot(a, b):
    return jnp.dot(a, b, preferred_element_type=F32)


def _pack_bf16_pair(lo, hi):
    lo_bits = pltpu.bitcast(lo.astype(BF16).astype(F32), jnp.uint32)
    hi_bits = pltpu.bitcast(hi.astype(BF16).astype(F32), jnp.uint32)
    return lax.shift_right_logical(lo_bits, jnp.uint32(16)) | (hi_bits & jnp.uint32(0xFFFF0000))


def _unpack_bf16_pair(w):
    lo = pltpu.bitcast(lax.shift_left(w, jnp.uint32(16)), F32)
    hi = pltpu.bitcast(w & jnp.uint32(0xFFFF0000), F32)
    return lo, hi


def _dot_split(a, b):
    a_hi = a.astype(BF16)
    a_mid = (a - a_hi.astype(F32)).astype(BF16)
    b_hi = b.astype(BF16)
    b_mid = (b - b_hi.astype(F32)).astype(BF16)
    return _dot(a_hi, b_hi) + _dot(a_hi, b_mid) + _dot(a_mid, b_hi)


def _dot_exact_lhs(a_bf16, v):
    hi, mid, lo = _split3(v)
    return _dot(a_bf16, hi) + _dot(a_bf16, mid) + _dot(a_bf16, lo)


def _ada_kernel(c_ref, w_ref, b_ref, o_ref):
    c = c_ref[...]
    s = c * _sigmoid(c)
    o_ref[0] = jnp.sum(w_ref[0] * s, axis=0, keepdims=True) + b_ref[0]


def _ada_mod(c, w_ada, b_ada):
    depth, d, n = w_ada.shape
    tn = 1024
    return pl.pallas_call(
        _ada_kernel,
        grid=(depth, n // tn),
        in_specs=[
            pl.BlockSpec((d, 1), lambda l, j: (0, 0)),
            pl.BlockSpec((1, d, tn), lambda l, j: (l, 0, j)),
            pl.BlockSpec((1, 1, tn), lambda l, j: (l, 0, j)),
        ],
        out_specs=pl.BlockSpec((1, 1, tn), lambda l, j: (l, 0, j)),
        out_shape=jax.ShapeDtypeStruct((depth, 1, n), F32),
        compiler_params=pltpu.CompilerParams(
            dimension_semantics=("arbitrary", "arbitrary"), vmem_limit_bytes=VMEM_LIMIT),
        name="ada_mod",
    )(c.reshape(d, 1), w_ada, b_ada.reshape(depth, 1, n))


def _inproj_kernel(x_ref, sh_ref, sc_ref, w_ref, wdt_ref, u_ref, dt_ref, h_scr):
    @pl.when(pl.program_id(1) == 0)
    def _():
        h = _layer_norm(x_ref[...]) * (1.0 + sc_ref[...]) + sh_ref[...]
        h_scr[...] = h.astype(BF16)
        dt_ref[...] = _dot_split(h, wdt_ref[...])

    u_ref[...] = _dot(h_scr[...], w_ref[...]).astype(BF16)


def _in_proj(x, sh, sc, w_main, w_dt, tm, tn):
    t, d = x.shape
    n = w_main.shape[1]
    return pl.pallas_call(
        _inproj_kernel,
        grid=(t // tm, n // tn),
        in_specs=[
            pl.BlockSpec((tm, d), lambda i, j: (i, 0)),
            pl.BlockSpec((1, d), lambda i, j: (0, 0)),
            pl.BlockSpec((1, d), lambda i, j: (0, 0)),
            pl.BlockSpec((d, tn), lambda i, j: (0, j)),
            pl.BlockSpec((d, LANES), lambda i, j: (0, 0)),
        ],
        out_specs=[
            pl.BlockSpec((tm, tn), lambda i, j: (i, j)),
            pl.BlockSpec((tm, LANES), lambda i, j: (i, 0)),
        ],
        out_shape=[
            jax.ShapeDtypeStruct((t, n), BF16),
            jax.ShapeDtypeStruct((t, LANES), F32),
        ],
        scratch_shapes=[pltpu.VMEM((tm, d), BF16)],
        compiler_params=pltpu.CompilerParams(
            dimension_semantics=("arbitrary", "arbitrary"), vmem_limit_bytes=VMEM_LIMIT),
        name="in_proj",
    )(x, sh, sc, w_main, w_dt)


def _ssd_kernel(xs_ref, b_ref, c_ref, dt_ref, cwx_ref, cwb_ref, cwc_ref, cbx_ref, cbb_ref,
                cbc_ref, dtb_ref, alog_ref, dskip_ref, y_ref,
                xbuf, bbuf, cbuf, xcs, bcs, ccs, state, *, chunk, rows):
    @pl.when(pl.program_id(0) == 0)
    def _():
        xbuf[...] = jnp.zeros(xbuf.shape, F32)
        bbuf[...] = jnp.zeros(bbuf.shape, F32)
        cbuf[...] = jnp.zeros(cbuf.shape, F32)
        state[...] = jnp.zeros(state.shape, F32)

    def conv_silu(in_ref, tail, w_ref, bias_ref, out_scr):
        x = in_ref[...].astype(F32)
        ext = jnp.concatenate([tail[...], x], axis=0)
        acc = bias_ref[...] + w_ref[SSD_CONV - 1:SSD_CONV, :] * x
        for j in range(1, SSD_CONV):
            shifted = pltpu.roll(ext, j, axis=0)[SUBLANES:, :]
            acc = acc + w_ref[SSD_CONV - 1 - j:SSD_CONV - j, :] * shifted
        out_scr[...] = acc * _sigmoid(acc)
        tail[...] = x[rows - SUBLANES:, :]

    conv_silu(xs_ref, xbuf, cwx_ref, cbx_ref, xcs)
    conv_silu(b_ref, bbuf, cwb_ref, cbb_ref, bcs)
    conv_silu(c_ref, cbuf, cwc_ref, cbc_ref, ccs)

    li = lax.broadcasted_iota(jnp.int32, (chunk, chunk), 0)
    si = lax.broadcasted_iota(jnp.int32, (chunk, chunk), 1)
    causal = li >= si
    tri = jnp.where(causal, 1.0, 0.0).astype(BF16)
    first_half = lax.broadcasted_iota(jnp.int32, (1, LANES), 1) < SSD_HEADDIM
    a_row = -jnp.exp(alog_ref[...])
    heads_per_group = SSD_HEADS // SSD_GROUPS

    def chunk_body(ci, carry):
        r0 = pl.multiple_of(ci * chunk, chunk)
        dt = _softplus(dt_ref[pl.ds(r0, chunk), :] + dtb_ref[...])
        la = dt * a_row
        acum = _dot_exact_lhs(tri, la)
        acum_t = acum.T
        dt_t = dt.T
        last_t = acum_t[:, chunk - 1:chunk]
        w_t = jnp.exp(last_t - acum_t) * dt_t
        cdec_t = jnp.exp(last_t)

        cb = []
        bt = []
        cg = []
        for g in range(SSD_GROUPS):
            bg = bcs[pl.ds(r0, chunk), g * SSD_STATE:(g + 1) * SSD_STATE]
            cgv = ccs[pl.ds(r0, chunk), g * SSD_STATE:(g + 1) * SSD_STATE]
            btg = bg.T
            bt.append(btg)
            cg.append(cgv)
            cb.append(_dot(cgv.astype(BF16), btg.astype(BF16)))

        for pr in range(SSD_HEADS // 2):
            lo = pr * LANES
            xs_pair = xcs[pl.ds(r0, chunk), lo:lo + LANES]
            xs_bf = xs_pair.astype(BF16)
            prev = state[:, lo:lo + LANES]
            lhs_m, lhs_c, lhs_b, cd = [], [], [], []
            for h in (2 * pr, 2 * pr + 1):
                g = h // heads_per_group
                col = jnp.broadcast_to(acum[:, h:h + 1], (chunk, chunk))
                row = acum_t[h:h + 1, :]
                dec = jnp.exp(jnp.where(causal, col - row, NEG_BIG))
                lhs_m.append((cb[g] * dec * dt_t[h:h + 1, :]).astype(BF16))
                lhs_c.append((cg[g] * jnp.exp(col)).astype(BF16))
                lhs_b.append((bt[g] * w_t[h:h + 1, :]).astype(BF16))
                cd.append(cdec_t[h:h + 1, :])
            out = (_dot(jnp.concatenate(lhs_m, axis=0), xs_bf)
                   + _dot(jnp.concatenate(lhs_c, axis=0), prev.astype(BF16)))
            y_pair = jnp.where(first_half, out[0:chunk, :], out[chunk:2 * chunk, :])
            y_ref[pl.ds(r0, chunk), lo:lo + LANES] = (
                y_pair + xs_pair * dskip_ref[:, lo:lo + LANES]).astype(y_ref.dtype)
            st = _dot(jnp.concatenate(lhs_b, axis=0), xs_bf)
            cd_pair = jnp.where(first_half, cd[0], cd[1])
            state[:, lo:lo + LANES] = prev * cd_pair + jnp.where(
                first_half, st[0:SSD_STATE, :], st[SSD_STATE:2 * SSD_STATE, :])
        return carry

    lax.fori_loop(0, rows // chunk, chunk_body, 0)


def _ssd(u, dt_raw, cw, cb, dt_bias, a_log, d_skip, rows, chunk):
    t = u.shape[0]
    gn = SSD_GROUPS * SSD_STATE
    pad = LANES - SSD_HEADS
    assert chunk == SSD_STATE
    kern = functools.partial(_ssd_kernel, chunk=chunk, rows=rows)
    full = lambda shape: pl.BlockSpec(shape, lambda i: (0, 0))
    return pl.pallas_call(
        kern,
        grid=(t // rows,),
        in_specs=[
            pl.BlockSpec((rows, SSD_INNER), lambda i: (i, COL_XS // SSD_INNER)),
            pl.BlockSpec((rows, gn), lambda i: (i, COL_B // gn)),
            pl.BlockSpec((rows, gn), lambda i: (i, COL_C // gn)),
            pl.BlockSpec((rows, LANES), lambda i: (i, 0)),
            full((SSD_CONV, SSD_INNER)), full((SSD_CONV, gn)), full((SSD_CONV, gn)),
            full((1, SSD_INNER)), full((1, gn)), full((1, gn)),
            full((1, LANES)), full((1, LANES)), full((1, SSD_INNER)),
        ],
        out_specs=pl.BlockSpec((rows, SSD_INNER), lambda i: (i, 0)),
        out_shape=jax.ShapeDtypeStruct((t, SSD_INNER), BF16),
        scratch_shapes=[
            pltpu.VMEM((SUBLANES, SSD_INNER), F32),
            pltpu.VMEM((SUBLANES, gn), F32),
            pltpu.VMEM((SUBLANES, gn), F32),
            pltpu.VMEM((rows, SSD_INNER), F32),
            pltpu.VMEM((rows, gn), F32),
            pltpu.VMEM((rows, gn), F32),
            pltpu.VMEM((SSD_STATE, SSD_INNER), F32),
        ],
        compiler_params=pltpu.CompilerParams(
            dimension_semantics=("arbitrary",), vmem_limit_bytes=VMEM_LIMIT),
        name="ssd",
    )(u, u, u, dt_raw,
      cw[:, :SSD_INNER], cw[:, SSD_INNER:SSD_INNER + gn], cw[:, SSD_INNER + gn:],
      cb[None, :SSD_INNER], cb[None, SSD_INNER:SSD_INNER + gn], cb[None, SSD_INNER + gn:],
      jnp.pad(dt_bias, (0, pad))[None, :], jnp.pad(a_log, (0, pad))[None, :],
      jnp.repeat(d_skip, SSD_HEADDIM)[None, :])


def _post_kernel(y_ref, z_ref, scb_ref, scc_ref, scx_ref, gate_ref, x_ref,
                 nw_ref, wa_ref, csw_ref, wb_ref, bg_ref, wo_ref, g1_ref, lng_ref, lnb_ref,
                 sh2_ref, sc2_ref, wr_ref, br_ref,
                 x1_ref, h2_ref, prob_ref, cst_ref, sidt_ref, runtab_ref, cnt_ref,
                 sbuf, run, *, rows, n_tok):
    @pl.when(pl.program_id(0) == 0)
    def _():
        sbuf[...] = jnp.zeros(sbuf.shape, F32)
        run[...] = jnp.zeros(run.shape, F32)

    z = z_ref[...].astype(F32)
    yg = y_ref[...].astype(F32) * (z * _sigmoid(z))
    ms = jnp.mean(yg * yg, axis=-1, keepdims=True)
    yn = yg * lax.rsqrt(ms + RMS_EPS) * nw_ref[...]
    u_a = _dot(yn.astype(BF16), wa_ref[...])

    cx = scc_ref[...].astype(F32) * scx_ref[...].astype(F32)
    ext = jnp.concatenate([sbuf[...], cx], axis=0)
    v = csw_ref[SC_CONV - 1:SC_CONV, :] * cx
    for j in range(1, SC_CONV):
        v = v + csw_ref[SC_CONV - 1 - j:SC_CONV - j, :] * pltpu.roll(ext, j, axis=0)[SUBLANES:, :]
    sbuf[...] = cx[rows - SUBLANES:, :]
    u_b = _dot((scb_ref[...].astype(F32) * v).astype(BF16), wb_ref[...])

    gl = gate_ref[...].astype(F32) + bg_ref[...]
    merged = _sigmoid(gl[:, :D_MODEL]) * u_a + _sigmoid(gl[:, D_MODEL:]) * u_b
    mix = _dot(merged.astype(BF16), wo_ref[...])
    x1 = _layer_norm(DN_ALPHA * x_ref[...] + g1_ref[...] * mix) * lng_ref[...] + lnb_ref[...]
    x1_ref[...] = x1
    h2 = _layer_norm(x1) * (1.0 + sc2_ref[...]) + sh2_ref[...]
    half = D_MODEL // 2
    h2_ref[...] = _pack_bf16_pair(h2[:, :half], h2[:, half:])

    logits = _dot_split(h2, wr_ref[...]) + br_ref[...]
    lane = lax.broadcasted_iota(jnp.int32, (rows, LANES), 1).astype(F32)
    work = logits
    onehots, vals = [], []
    for k in range(TOP_K):
        m = jnp.max(work, axis=-1, keepdims=True)
        ik = jnp.min(jnp.where(work == m, lane, float(LANES)), axis=-1, keepdims=True)
        oh = lane == ik
        onehots.append(oh)
        vals.append(m)
        work = jnp.where(oh, -jnp.inf, work)
    es = [jnp.exp(vk - vals[0]) for vk in vals]
    denom = es[0] + es[1] + es[2] + es[3]
    prob_out = jnp.zeros((rows, LANES), F32)
    for k in range(TOP_K):
        prob_out = jnp.where(lane == float(k), es[k] / denom, prob_out)

    sel = jnp.zeros((rows, LANES), F32)
    kk = jnp.zeros((rows, LANES), F32)
    for k, oh in enumerate(onehots):
        sel = sel + jnp.where(oh, 1.0, 0.0)
        kk = kk + jnp.where(oh, float(k), 0.0)
    ri = lax.broadcasted_iota(jnp.int32, (rows, rows), 0)
    rj = lax.broadcasted_iota(jnp.int32, (rows, rows), 1)
    strict = jnp.where(ri > rj, 1.0, 0.0).astype(BF16)
    base = _dot(strict, sel.astype(BF16)) + run[...]
    tok = (pl.program_id(0) * rows
           + lax.broadcasted_iota(jnp.int32, (rows, LANES), 0)).astype(F32)
    picked = sel > 0.0
    cs = jnp.where(picked, base + 1.0, 0.0)
    sid = jnp.where(picked, kk * float(n_tok) + tok, 0.0)
    cst_ref[...] = cs.T[:N_EXPERTS, :]
    sidt_ref[...] = sid.T[:N_EXPERTS, :]
    runtab_ref[0] = run[...].astype(jnp.int32)
    run[...] = run[...] + jnp.sum(sel, axis=0, keepdims=True)
    prob_ref[...] = prob_out
    cnt_ref[...] = run[...].astype(jnp.int32)


def _post(y, u, x, vecs, mats, rows):
    t = x.shape[0]
    kern = functools.partial(_post_kernel, rows=rows, n_tok=t)
    row = lambda w: pl.BlockSpec((1, w), lambda i: (0, 0))
    mat = lambda a, b: pl.BlockSpec((a, b), lambda i: (0, 0))
    ublk = lambda w, col: pl.BlockSpec((rows, w), lambda i: (i, col // w))
    nw, csw, bg, g1, lng, lnb, sh2, sc2, br = vecs
    wa, wb, wo, wr = mats
    tok = pl.BlockSpec((rows, LANES), lambda i: (i, 0))
    return pl.pallas_call(
        kern,
        grid=(t // rows,),
        in_specs=[
            pl.BlockSpec((rows, SSD_INNER), lambda i: (i, 0)),
            ublk(SSD_INNER, COL_Z), ublk(D_MODEL, COL_SCB), ublk(D_MODEL, COL_SCC),
            ublk(D_MODEL, COL_SCX), ublk(2 * D_MODEL, COL_GATE),
            pl.BlockSpec((rows, D_MODEL), lambda i: (i, 0)),
            row(SSD_INNER), mat(SSD_INNER, D_MODEL), mat(SC_CONV, D_MODEL), mat(D_MODEL, D_MODEL),
            row(2 * D_MODEL), mat(D_MODEL, D_MODEL), row(D_MODEL), row(D_MODEL), row(D_MODEL),
            row(D_MODEL), row(D_MODEL), mat(D_MODEL, LANES), row(LANES),
        ],
        out_specs=[
            pl.BlockSpec((rows, D_MODEL), lambda i: (i, 0)),
            pl.BlockSpec((rows, D_MODEL // 2), lambda i: (i, 0)),
            tok,
            pl.BlockSpec((N_EXPERTS, rows), lambda i: (0, i)),
            pl.BlockSpec((N_EXPERTS, rows), lambda i: (0, i)),
            pl.BlockSpec((1, 1, LANES), lambda i: (i, 0, 0)),
            pl.BlockSpec((1, LANES), lambda i: (0, 0)),
        ],
        out_shape=[
            jax.ShapeDtypeStruct((t, D_MODEL), F32),
            jax.ShapeDtypeStruct((t, D_MODEL // 2), jnp.uint32),
            jax.ShapeDtypeStruct((t, LANES), F32),
            jax.ShapeDtypeStruct((N_EXPERTS, t), F32),
            jax.ShapeDtypeStruct((N_EXPERTS, t), F32),
            jax.ShapeDtypeStruct((t // rows, 1, LANES), jnp.int32),
            jax.ShapeDtypeStruct((1, LANES), jnp.int32),
        ],
        scratch_shapes=[
            pltpu.VMEM((SUBLANES, D_MODEL), F32),
            pltpu.VMEM((1, LANES), F32),
        ],
        compiler_params=pltpu.CompilerParams(
            dimension_semantics=("arbitrary",), vmem_limit_bytes=VMEM_LIMIT),
        name="post_mix",
    )(y, u, u, u, u, u, x, nw, wa, csw, wb, bg, wo, g1, lng, lnb, sh2, sc2, wr, br)


def _invert_kernel(vt_ref, ve_ref, nv_ref, gs_ref, run_ref, cst_ref, sidt_ref, o_ref, ptr,
                   *, tile, tok_tile, n_tok_tiles):
    v = pl.program_id(0)
    i = vt_ref[v]
    e = ve_ref[v]
    vp = jnp.maximum(v - 1, 0)
    new_tile = jnp.logical_or(v == 0, vt_ref[vp] != i)
    new_exp = jnp.logical_or(v == 0, ve_ref[vp] != e)

    @pl.when(new_exp)
    def _():
        ptr[0] = 0

    @pl.when(new_tile)
    def _():
        o_ref[...] = jnp.zeros(o_ref.shape, jnp.int32)

    @pl.when(v < nv_ref[0])
    def _():
        g0 = gs_ref[e]
        row0 = i * tile
        ra = jnp.maximum(g0, row0) - g0
        rb = jnp.minimum(gs_ref[e + 1], row0 + tile) - g0
        b_lo = lax.while_loop(lambda b: run_ref[(b + 1) * N_EXPERTS + e] <= ra,
                              lambda b: b + 1, ptr[0])
        ptr[0] = b_lo
        b_hi = lax.while_loop(
            lambda b: jnp.logical_and(b < n_tok_tiles, run_ref[b * N_EXPERTS + e] < rb),
            lambda b: b + 1, b_lo)
        n_blk = tile // LANES
        row_in_blk = lax.broadcasted_iota(jnp.int32, (LANES, LANES), 0).astype(F32)
        firsts = [(row0 - g0 + 1 + k * LANES).astype(F32) for k in range(n_blk)]

        def body(b, accs):
            c0 = pl.multiple_of(b * tok_tile, tok_tile)
            cs_row = cst_ref[pl.ds(e, 1), pl.ds(c0, tok_tile)]
            sid_row = sidt_ref[pl.ds(e, 1), pl.ds(c0, tok_tile)]
            out = []
            for k in range(n_blk):
                rel = cs_row - firsts[k]
                acc = accs[k]
                for j in range(tok_tile // LANES):
                    lanes = slice(j * LANES, (j + 1) * LANES)
                    acc = acc + jnp.where(rel[:, lanes] == row_in_blk, sid_row[:, lanes], 0.0)
                out.append(acc)
            return tuple(out)

        accs = lax.fori_loop(b_lo, b_hi, body,
                             tuple(jnp.zeros((LANES, LANES), F32) for _ in range(n_blk)))
        for k in range(n_blk):
            contrib = jnp.sum(accs[k].T, axis=0, keepdims=True)
            lanes = slice(k * LANES, (k + 1) * LANES)
            o_ref[0, :, lanes] = o_ref[0, :, lanes] + contrib.astype(jnp.int32)


def _invert(v_tile, v_exp, n_visits, gs, runflat, cst, sidt, n_tiles, tile, tok_tile):
    t = cst.shape[1]
    n_vis = v_tile.shape[0]
    kern = functools.partial(_invert_kernel, tile=tile, tok_tile=tok_tile,
                             n_tok_tiles=t // tok_tile)
    return pl.pallas_call(
        kern,
        grid_spec=pltpu.PrefetchScalarGridSpec(
            num_scalar_prefetch=5,
            grid=(n_vis,),
            in_specs=[
                pl.BlockSpec((N_EXPERTS, t), lambda v, *_: (0, 0)),
                pl.BlockSpec((N_EXPERTS, t), lambda v, *_: (0, 0)),
            ],
            out_specs=pl.BlockSpec((1, 1, tile), lambda v, vt, *_: (vt[v], 0, 0)),
            scratch_shapes=[pltpu.SMEM((1,), jnp.int32)],
        ),
        out_shape=jax.ShapeDtypeStruct((n_tiles, 1, tile), jnp.int32),
        compiler_params=pltpu.CompilerParams(
            dimension_semantics=("arbitrary",), vmem_limit_bytes=VMEM_LIMIT),
        name="moe_invert",
    )(v_tile, v_exp, n_visits, gs, runflat, cst, sidt)


N_FF_BLK = 4
FF_BLK = EXPERT_FF // N_FF_BLK
N_STAGES = 2 * N_FF_BLK


def _experts_kernel(vt_ref, ve_ref, vn_ref, nv_ref, gs_ref,
                    ord_cur, ord_next, h_hbm, wgu_hbm, bgu_ref, wd_hbm, bd_ref,
                    slots_hbm, h_vmem, x_a, x_b, o_a, o_b, act, wgu_st, wd_st, wgu_bf, wd_bf,
                    hsem, wsem, ssem, *, tile, n_tok, layer):
    v = pl.program_id(0)
    nv = nv_ref[0]
    i = vt_ref[v]
    e = ve_ref[v]
    vp = jnp.maximum(v - 1, 0)
    active = v < nv
    new_tile = jnp.logical_or(v == 0, vt_ref[vp] != i)
    new_exp = jnp.logical_or(v == 0, ve_ref[vp] != e)
    tile_done = jnp.logical_or(v == nv - 1, vt_ref[jnp.minimum(v + 1, vt_ref.shape[0] - 1)] != i)
    even = lax.rem(i, 2) == 0
    half = D_MODEL // 2
    xs = (x_a, x_b)
    os_ = (o_a, o_b)

    def weight_copies(expert):
        return (pltpu.make_async_copy(wgu_hbm.at[layer, expert], wgu_st, wsem.at[0]),
                pltpu.make_async_copy(wd_hbm.at[layer, expert], wd_st, wsem.at[1]))

    def fetch_row(order_ref, r, dst):
        sid = order_ref[r]
        tok = sid & (n_tok - 1) if n_tok & (n_tok - 1) == 0 else lax.rem(sid, n_tok)
        dst[pl.ds(r, 1), :] = h_vmem[pl.ds(tok, 1), :]

    def scatter_row(order_ref, r, src, p):
        return pltpu.make_async_copy(src.at[pl.ds(r, 1)],
                                     slots_hbm.at[pl.ds(order_ref[r], 1)], ssem.at[p])

    @pl.when(v == 0)
    def _():
        tokens = pltpu.make_async_copy(h_hbm, h_vmem, hsem)
        tokens.start()
        for c in weight_copies(e):
            c.start()
        tokens.wait()
        for r in range(tile):
            fetch_row(ord_cur, r, x_a)

    @pl.when(jnp.logical_and(active, new_exp))
    def _():
        for c in weight_copies(e):
            c.wait()
        wgu_bf[...] = wgu_st[...].astype(BF16)
        wd_bf[...] = wd_st[...].astype(BF16)

        @pl.when(vn_ref[v] >= 0)
        def _():
            for c in weight_copies(vn_ref[v]):
                c.start()

    rowpos = i * tile + lax.broadcasted_iota(jnp.int32, (tile, 1), 0)
    mine = jnp.logical_and(rowpos >= gs_ref[e], rowpos < gs_ref[e + 1])

    def compute(x_ref, o_ref, first, between):
        x_lo, x_hi = _unpack_bf16_pair(x_ref[...])
        xb = jnp.concatenate([x_lo.astype(BF16), x_hi.astype(BF16)], axis=1)
        for cb in range(N_FF_BLK):
            c0 = cb * FF_BLK
            g = _dot(xb, wgu_bf[:, c0:c0 + FF_BLK]) + bgu_ref[:, c0:c0 + FF_BLK]
            u = (_dot(xb, wgu_bf[:, EXPERT_FF + c0:EXPERT_FF + c0 + FF_BLK])
                 + bgu_ref[:, EXPERT_FF + c0:EXPERT_FF + c0 + FF_BLK])
            g = jnp.minimum(g, SWIGLU_LIMIT)
            u = jnp.clip(u, -SWIGLU_LIMIT, SWIGLU_LIMIT)
            act[:, c0:c0 + FF_BLK] = ((u + 1.0) * g * _sigmoid(SWIGLU_ALPHA * g)).astype(BF16)
            between(cb)
        for pb in range(N_FF_BLK // 2):
            c0 = pb * FF_BLK
            o_lo = _dot(act[...], wd_bf[:, c0:c0 + FF_BLK]) + bd_ref[:, c0:c0 + FF_BLK]
            between(N_FF_BLK + 2 * pb)
            o_hi = (_dot(act[...], wd_bf[:, half + c0:half + c0 + FF_BLK])
                    + bd_ref[:, half + c0:half + c0 + FF_BLK])
            packed = _pack_bf16_pair(o_lo, o_hi)
            keep = jnp.uint32(0) if first else o_ref[:, c0:c0 + FF_BLK]
            o_ref[:, c0:c0 + FF_BLK] = jnp.where(mine, packed, keep)
            between(N_FF_BLK + 2 * pb + 1)

    per_stage = tile // N_STAGES

    for p in range(2):
        x_cur, x_nxt = xs[p], xs[1 - p]
        o_cur, o_oth = os_[p], os_[1 - p]
        on_parity = even if p == 0 else jnp.logical_not(even)

        @pl.when(jnp.logical_and(jnp.logical_and(active, new_tile), on_parity))
        def _(x_cur=x_cur, x_nxt=x_nxt, o_cur=o_cur, p=p):
            @pl.when(i >= 2)
            def _():
                for r in range(tile):
                    scatter_row(ord_cur, r, o_cur, p).wait()

            def between(j):
                for r in range(j * per_stage, (j + 1) * per_stage):
                    fetch_row(ord_next, r, x_nxt)

            compute(x_cur, o_cur, True, between)

        @pl.when(jnp.logical_and(jnp.logical_and(active, jnp.logical_not(new_tile)), on_parity))
        def _(x_cur=x_cur, o_cur=o_cur):
            compute(x_cur, o_cur, False, lambda j: None)

        @pl.when(jnp.logical_and(jnp.logical_and(active, tile_done), on_parity))
        def _(o_cur=o_cur, p=p):
            for r in range(tile):
                scatter_row(ord_cur, r, o_cur, p).start(priority=r % 2)

        @pl.when(jnp.logical_and(v == nv - 1, on_parity))
        def _(o_cur=o_cur, o_oth=o_oth, p=p):
            for r in range(tile):
                scatter_row(ord_cur, r, o_cur, p).wait()

            @pl.when(i >= 1)
            def _():
                for r in range(tile):
                    scatter_row(ord_cur, r, o_oth, 1 - p).wait()


def _experts(v_tile, v_exp, v_next, n_visits, gs, order_ext, h2, w_gu, b_gu, w_down, b_down,
             layer, n_slots, tile):
    n_vis = v_tile.shape[0]
    n_tok = h2.shape[0]
    ff2 = 2 * EXPERT_FF
    half = D_MODEL // 2
    kern = functools.partial(_experts_kernel, tile=tile, n_tok=n_tok, layer=layer)
    smem_blk = lambda off: pl.BlockSpec(
        (tile,), lambda v, vt, *_: (vt[v] + off,), memory_space=pltpu.SMEM)
    bias = lambda n: pl.BlockSpec(
        (None, None, 1, n), lambda v, vt, ve, *_: (layer, ve[v], 0, 0))
    hbm = pl.BlockSpec(memory_space=pl.ANY)
    return pl.pallas_call(
        kern,
        grid_spec=pltpu.PrefetchScalarGridSpec(
            num_scalar_prefetch=5,
            grid=(n_vis,),
            in_specs=[smem_blk(0), smem_blk(1), hbm, hbm, bias(ff2), hbm, bias(D_MODEL)],
            out_specs=hbm,
            scratch_shapes=[
                pltpu.VMEM((n_tok, half), jnp.uint32),
                pltpu.VMEM((tile, half), jnp.uint32),
                pltpu.VMEM((tile, half), jnp.uint32),
                pltpu.VMEM((tile, half), jnp.uint32),
                pltpu.VMEM((tile, half), jnp.uint32),
                pltpu.VMEM((tile, EXPERT_FF), BF16),
                pltpu.VMEM((D_MODEL, ff2), F32),
                pltpu.VMEM((EXPERT_FF, D_MODEL), F32),
                pltpu.VMEM((D_MODEL, ff2), BF16),
                pltpu.VMEM((EXPERT_FF, D_MODEL), BF16),
                pltpu.SemaphoreType.DMA(()),
                pltpu.SemaphoreType.DMA((2,)),
                pltpu.SemaphoreType.DMA((2,)),
            ],
        ),
        out_shape=jax.ShapeDtypeStruct((n_slots, half), jnp.uint32),
        compiler_params=pltpu.CompilerParams(
            dimension_semantics=("arbitrary",), vmem_limit_bytes=EXPERTS_VMEM_LIMIT),
        name="moe_experts",
    )(v_tile, v_exp, v_next, n_visits, gs, order_ext, order_ext, h2,
      w_gu, b_gu[:, :, None, :], w_down, b_down[:, :, None, :])


def _combine_kernel(s0_ref, s1_ref, s2_ref, s3_ref, prob_ref, x1_ref, g2_ref, lng_ref, lnb_ref,
                    x2_ref):
    prob = prob_ref[...]
    ffn_lo = jnp.zeros((x1_ref.shape[0], D_MODEL // 2), F32)
    ffn_hi = jnp.zeros((x1_ref.shape[0], D_MODEL // 2), F32)
    for k, s_ref in enumerate((s0_ref, s1_ref, s2_ref, s3_ref)):
        lo, hi = _unpack_bf16_pair(s_ref[...])
        ffn_lo = ffn_lo + prob[:, k:k + 1] * lo
        ffn_hi = ffn_hi + prob[:, k:k + 1] * hi
    ffn = jnp.concatenate([ffn_lo, ffn_hi], axis=1)
    x2_ref[...] = (_layer_norm(DN_ALPHA * x1_ref[...] + g2_ref[...] * ffn) * lng_ref[...]
                   + lnb_ref[...])


def _combine(slots, probs, x1, g2, lng, lnb, rows):
    t = x1.shape[0]
    row = pl.BlockSpec((1, D_MODEL), lambda i: (0, 0))
    slot_blk = lambda k: pl.BlockSpec((rows, D_MODEL // 2), lambda i: (k * (t // rows) + i, 0))
    return pl.pallas_call(
        _combine_kernel,
        grid=(t // rows,),
        in_specs=[
            slot_blk(0), slot_blk(1), slot_blk(2), slot_blk(3),
            pl.BlockSpec((rows, LANES), lambda i: (i, 0)),
            pl.BlockSpec((rows, D_MODEL), lambda i: (i, 0)),
            row, row, row,
        ],
        out_specs=pl.BlockSpec((rows, D_MODEL), lambda i: (i, 0)),
        out_shape=jax.ShapeDtypeStruct((t, D_MODEL), F32),
        compiler_params=pltpu.CompilerParams(
            dimension_semantics=("arbitrary",), vmem_limit_bytes=VMEM_LIMIT),
        name="moe_combine",
    )(slots, slots, slots, slots, probs, x1, g2, lng, lnb)


def _tile(t, pref):
    return pref if t % pref == 0 else t


def _visit_plan(counts, tile, n_vis):
    ge = jnp.cumsum(counts)
    gs = ge - counts
    t_lo = gs // tile
    nt = jnp.where(counts > 0, (ge - 1) // tile - t_lo + 1, 0)
    vend = jnp.cumsum(nt)
    vbase = vend - nt
    n_visits = vend[-1:]
    vc = jnp.minimum(jnp.arange(n_vis, dtype=jnp.int32), n_visits[0] - 1)
    v_exp = jnp.sum((vend[None, :] <= vc[:, None]).astype(jnp.int32), axis=1)
    onehot = (v_exp[:, None] == jnp.arange(N_EXPERTS, dtype=jnp.int32)[None, :]).astype(jnp.int32)
    v_tile = jnp.sum(onehot * (t_lo - vbase)[None, :], axis=1) + vc
    gs33 = jnp.concatenate([gs, ge[-1:]])
    after = jnp.sum(onehot * vend[None, :], axis=1)
    e_after = jnp.sum((vend[None, :] <= after[:, None]).astype(jnp.int32), axis=1)
    v_next = jnp.where(after < n_visits[0], e_after, -1)
    return (v_tile.astype(jnp.int32), v_exp.astype(jnp.int32), v_next.astype(jnp.int32),
            n_visits.astype(jnp.int32), gs33.astype(jnp.int32))


def kernel(x, c, w_ada, b_ada, w_in, conv_ssd_w, conv_ssd_b, dt_bias, a_log, d_skip, ssd_norm_w,
           w_ssd_out, conv_short_w, w_short_out, b_gate, w_o, ln1_g, ln1_b, w_router, b_router,
           w_gu, b_gu, w_down, b_down, ln2_g, ln2_b):
    batch, seq, d = x.shape
    assert batch == 1 and d == D_MODEL
    depth = w_in.shape[0]
    t = seq
    xt = x.reshape(t, d)

    tm_in = _tile(t, 1024)
    rows_ssd = _tile(t, 512)
    chunk = 128
    rows_post = 256
    rows_moe = 256
    tile_e = 256
    assert t % rows_post == 0
    n_tiles = t * TOP_K // tile_e
    n_vis = n_tiles + N_EXPERTS
    n_slots = t * TOP_K

    mods = _ada_mod(c, w_ada, b_ada)

    s0 = SSD_INNER
    s1 = s0 + SSD_INNER + 2 * SSD_GROUPS * SSD_STATE
    s2 = s1 + SSD_HEADS
    w_main = jnp.concatenate([w_in[:, :, :s1].astype(BF16), w_in[:, :, s2:].astype(BF16)], axis=-1)
    w_dt = jnp.pad(w_in[:, :, s1:s2], ((0, 0), (0, 0), (0, LANES - SSD_HEADS)))
    w_r = jnp.pad(w_router, ((0, 0), (0, 0), (0, LANES - N_EXPERTS)))
    b_r = jnp.pad(b_router, ((0, 0), (0, LANES - N_EXPERTS)), constant_values=NEG_BIG)
    wa_bf = w_ssd_out.astype(BF16)
    wb_bf = w_short_out.astype(BF16)
    wo_bf = w_o.astype(BF16)
    cool = jnp.zeros((tile_e,), jnp.int32)

    for l in range(depth):
        m = mods[l]
        sh1, sc1, g1, sh2, sc2, g2 = [m[:, k * d:(k + 1) * d] for k in range(N_ADA)]
        u, dt_raw = _in_proj(xt, sh1, sc1, w_main[l], w_dt[l], tm_in, 1024)
        y = _ssd(u, dt_raw, conv_ssd_w[l], conv_ssd_b[l], dt_bias[l], a_log[l], d_skip[l],
                 rows_ssd, chunk)
        vecs = (ssd_norm_w[l][None, :], conv_short_w[l], b_gate[l][None, :], g1,
                ln1_g[l][None, :], ln1_b[l][None, :], sh2, sc2, b_r[l][None, :])
        x1, h2, probs, cst, sidt, runtab, cnt = _post(
            y, u, xt, vecs, (wa_bf[l], wb_bf[l], wo_bf[l], w_r[l]), rows_post)

        v_tile, v_exp, v_next, n_visits, gs = _visit_plan(cnt[0, :N_EXPERTS], tile_e, n_vis)
        runflat = jnp.concatenate([runtab[:, 0, :N_EXPERTS], cnt[:, :N_EXPERTS]]).reshape(-1)
        order = _invert(v_tile, v_exp, n_visits, gs, runflat, cst, sidt, n_tiles, tile_e,
                        rows_post)
        order_ext = jnp.concatenate([order.reshape(-1), cool])
        slots = _experts(v_tile, v_exp, v_next, n_visits, gs, order_ext, h2, w_gu, b_gu, w_down,
                         b_down, l, n_slots, tile_e)
        xt = _combine(slots, probs, x1, g2, ln2_g[l][None, :], ln2_b[l][None, :], rows_moe)

    return xt.reshape(batch, seq, d)
```

```python
import functools

import jax
import jax.numpy as jnp
from jax import lax
from jax.experimental import pallas as pl
from jax.experimental.pallas import tpu as pltpu

F32 = jnp.float32
BF16 = jnp.bfloat16

D_MODEL = 1024
SSD_INNER = 2048
SSD_HEADS = 32
SSD_HEADDIM = 64
SSD_GROUPS = 4
SSD_STATE = 128
SSD_CONV = 4
SC_CONV = 3
N_EXPERTS = 32
TOP_K = 4
EXPERT_FF = 1024
SWIGLU_LIMIT = 7.0
SWIGLU_ALPHA = 1.702
DEPTH = 4
DN_ALPHA = (2.0 * DEPTH) ** 0.25
LN_EPS = 1e-5
RMS_EPS = 1e-5
N_ADA = 6

LANES = 128
SUBLANES = 8
U_MAIN = 10240
COL_Z = 0
COL_XS = 2048
COL_B = 4096
COL_C = 4608
COL_SCB = 5120
COL_SCC = 6144
COL_SCX = 7168
COL_GATE = 8192
NEG_BIG = -1e30

VMEM_LIMIT = 56 * 1024 * 1024
EXPERTS_VMEM_LIMIT = 60 * 1024 * 1024


def _sigmoid(v):
    return 1.0 / (1.0 + jnp.exp(-v))


def _softplus(v):
    return jnp.maximum(v, 0.0) + jnp.log(1.0 + jnp.exp(-jnp.abs(v)))


def _layer_norm(v):
    mu = jnp.mean(v, axis=-1, keepdims=True)
    vc = v - mu
    var = jnp.mean(vc * vc, axis=-1, keepdims=True)
    return vc * lax.rsqrt(var + LN_EPS)


def _split3(v):
    hi = v.astype(BF16)
    r1 = v - hi.astype(F32)
    mid = r1.astype(BF16)
    lo = (r1 - mid.astype(F32)).astype(BF16)
    return hi, mid, lo


def _dot(a, b):
    return jnp.dot(a, b, preferred_element_type=F32)


def _pack_bf16_pair(lo, hi):
    lo_bits = pltpu.bitcast(lo.astype(BF16).astype(F32), jnp.uint32)
    hi_bits = pltpu.bitcast(hi.astype(BF16).astype(F32), jnp.uint32)
    return lax.shift_right_logical(lo_bits, jnp.uint32(16)) | (hi_bits & jnp.uint32(0xFFFF0000))


def _unpack_bf16_pair(w):
    lo = pltpu.bitcast(lax.shift_left(w, jnp.uint32(16)), F32)
    hi = pltpu.bitcast(w & jnp.uint32(0xFFFF0000), F32)
    return lo, hi


def _dot_split(a, b):
    a_hi = a.astype(BF16)
    a_mid = (a - a_hi.astype(F32)).astype(BF16)
    b_hi = b.astype(BF16)
    b_mid = (b - b_hi.astype(F32)).astype(BF16)
    return _dot(a_hi, b_hi) + _dot(a_hi, b_mid) + _dot(a_mid, b_hi)


def _dot_exact_lhs(a_bf16, v):
    hi, mid, lo = _split3(v)
    return _dot(a_bf16, hi) + _dot(a_bf16, mid) + _dot(a_bf16, lo)


def _ada_kernel(c_ref, w_ref, b_ref, o_ref):
    c = c_ref[...]
    s = c * _sigmoid(c)
    o_ref[0] = jnp.sum(w_ref[0] * s, axis=0, keepdims=True) + b_ref[0]


def _ada_mod(c, w_ada, b_ada):
    depth, d, n = w_ada.shape
    tn = 1024
    return pl.pallas_call(
        _ada_kernel,
        grid=(depth, n // tn),
        in_specs=[
            pl.BlockSpec((d, 1), lambda l, j: (0, 0)),
            pl.BlockSpec((1, d, tn), lambda l, j: (l, 0, j)),
            pl.BlockSpec((1, 1, tn), lambda l, j: (l, 0, j)),
        ],
        out_specs=pl.BlockSpec((1, 1, tn), lambda l, j: (l, 0, j)),
        out_shape=jax.ShapeDtypeStruct((depth, 1, n), F32),
        compiler_params=pltpu.CompilerParams(
            dimension_semantics=("arbitrary", "arbitrary"), vmem_limit_bytes=VMEM_LIMIT),
        name="ada_mod",
    )(c.reshape(d, 1), w_ada, b_ada.reshape(depth, 1, n))


def _inproj_kernel(x_ref, sh_ref, sc_ref, w_ref, wdt_ref, u_ref, dt_ref, h_scr):
    @pl.when(pl.program_id(1) == 0)
    def _():
        h = _layer_norm(x_ref[...]) * (1.0 + sc_ref[...]) + sh_ref[...]
        h_scr[...] = h.astype(BF16)
        dt_ref[...] = _dot_split(h, wdt_ref[...])

    u_ref[...] = _dot(h_scr[...], w_ref[...]).astype(BF16)


def _in_proj(x, sh, sc, w_main, w_dt, tm, tn):
    t, d = x.shape
    n = w_main.shape[1]
    return pl.pallas_call(
        _inproj_kernel,
        grid=(t // tm, n // tn),
        in_specs=[
            pl.BlockSpec((tm, d), lambda i, j: (i, 0)),
            pl.BlockSpec((1, d), lambda i, j: (0, 0)),
            pl.BlockSpec((1, d), lambda i, j: (0, 0)),
            pl.BlockSpec((d, tn), lambda i, j: (0, j)),
            pl.BlockSpec((d, LANES), lambda i, j: (0, 0)),
        ],
        out_specs=[
            pl.BlockSpec((tm, tn), lambda i, j: (i, j)),
            pl.BlockSpec((tm, LANES), lambda i, j: (i, 0)),
        ],
        out_shape=[
            jax.ShapeDtypeStruct((t, n), BF16),
            jax.ShapeDtypeStruct((t, LANES), F32),
        ],
        scratch_shapes=[pltpu.VMEM((tm, d), BF16)],
        compiler_params=pltpu.CompilerParams(
            dimension_semantics=("arbitrary", "arbitrary"), vmem_limit_bytes=VMEM_LIMIT),
        name="in_proj",
    )(x, sh, sc, w_main, w_dt)


def _ssd_kernel(xs_ref, b_ref, c_ref, dt_ref, cwx_ref, cwb_ref, cwc_ref, cbx_ref, cbb_ref,
                cbc_ref, dtb_ref, alog_ref, dskip_ref, y_ref,
                xbuf, bbuf, cbuf, xcs, bcs, ccs, state, *, chunk, rows):
    @pl.when(pl.program_id(0) == 0)
    def _():
        xbuf[...] = jnp.zeros(xbuf.shape, F32)
        bbuf[...] = jnp.zeros(bbuf.shape, F32)
        cbuf[...] = jnp.zeros(cbuf.shape, F32)
        state[...] = jnp.zeros(state.shape, F32)

    def conv_silu(in_ref, tail, w_ref, bias_ref, out_scr):
        x = in_ref[...].astype(F32)
        ext = jnp.concatenate([tail[...], x], axis=0)
        acc = bias_ref[...] + w_ref[SSD_CONV - 1:SSD_CONV, :] * x
        for j in range(1, SSD_CONV):
            shifted = pltpu.roll(ext, j, axis=0)[SUBLANES:, :]
            acc = acc + w_ref[SSD_CONV - 1 - j:SSD_CONV - j, :] * shifted
        out_scr[...] = acc * _sigmoid(acc)
        tail[...] = x[rows - SUBLANES:, :]

    conv_silu(xs_ref, xbuf, cwx_ref, cbx_ref, xcs)
    conv_silu(b_ref, bbuf, cwb_ref, cbb_ref, bcs)
    conv_silu(c_ref, cbuf, cwc_ref, cbc_ref, ccs)

    li = lax.broadcasted_iota(jnp.int32, (chunk, chunk), 0)
    si = lax.broadcasted_iota(jnp.int32, (chunk, chunk), 1)
    causal = li >= si
    tri = jnp.where(causal, 1.0, 0.0).astype(BF16)
    first_half = lax.broadcasted_iota(jnp.int32, (1, LANES), 1) < SSD_HEADDIM
    a_row = -jnp.exp(alog_ref[...])
    heads_per_group = SSD_HEADS // SSD_GROUPS

    def chunk_body(ci, carry):
        r0 = pl.multiple_of(ci * chunk, chunk)
        dt = _softplus(dt_ref[pl.ds(r0, chunk), :] + dtb_ref[...])
        la = dt * a_row
        acum = _dot_exact_lhs(tri, la)
        acum_t = acum.T
        dt_t = dt.T
        last_t = acum_t[:, chunk - 1:chunk]
        w_t = jnp.exp(last_t - acum_t) * dt_t
        cdec_t = jnp.exp(last_t)

        cb = []
        bt = []
        cg = []
        for g in range(SSD_GROUPS):
            bg = bcs[pl.ds(r0, chunk), g * SSD_STATE:(g + 1) * SSD_STATE]
            cgv = ccs[pl.ds(r0, chunk), g * SSD_STATE:(g + 1) * SSD_STATE]
            btg = bg.T
            bt.append(btg)
            cg.append(cgv)
            cb.append(_dot(cgv.astype(BF16), btg.astype(BF16)))

        for pr in range(SSD_HEADS // 2):
            lo = pr * LANES
            xs_pair = xcs[pl.ds(r0, chunk), lo:lo + LANES]
            xs_bf = xs_pair.astype(BF16)
            prev = state[:, lo:lo + LANES]
            lhs_m, lhs_c, lhs_b, cd = [], [], [], []
            for h in (2 * pr, 2 * pr + 1):
                g = h // heads_per_group
                col = jnp.broadcast_to(acum[:, h:h + 1], (chunk, chunk))
                row = acum_t[h:h + 1, :]
                dec = jnp.exp(jnp.where(causal, col - row, NEG_BIG))
                lhs_m.append((cb[g] * dec * dt_t[h:h + 1, :]).astype(BF16))
                lhs_c.append((cg[g] * jnp.exp(col)).astype(BF16))
                lhs_b.append((bt[g] * w_t[h:h + 1, :]).astype(BF16))
                cd.append(cdec_t[h:h + 1, :])
            out = (_dot(jnp.concatenate(lhs_m, axis=0), xs_bf)
                   + _dot(jnp.concatenate(lhs_c, axis=0), prev.astype(BF16)))
            y_pair = jnp.where(first_half, out[0:chunk, :], out[chunk:2 * chunk, :])
            y_ref[pl.ds(r0, chunk), lo:lo + LANES] = (
                y_pair + xs_pair * dskip_ref[:, lo:lo + LANES]).astype(y_ref.dtype)
            st = _dot(jnp.concatenate(lhs_b, axis=0), xs_bf)
            cd_pair = jnp.where(first_half, cd[0], cd[1])
            state[:, lo:lo + LANES] = prev * cd_pair + jnp.where(
                first_half, st[0:SSD_STATE, :], st[SSD_STATE:2 * SSD_STATE, :])
        return carry

    lax.fori_loop(0, rows // chunk, chunk_body, 0)


def _ssd(u, dt_raw, cw, cb, dt_bias, a_log, d_skip, rows, chunk):
    t = u.shape[0]
    gn = SSD_GROUPS * SSD_STATE
    pad = LANES - SSD_HEADS
    assert chunk == SSD_STATE
    kern = functools.partial(_ssd_kernel, chunk=chunk, rows=rows)
    full = lambda shape: pl.BlockSpec(shape, lambda i: (0, 0))
    return pl.pallas_call(
        kern,
        grid=(t // rows,),
        in_specs=[
            pl.BlockSpec((rows, SSD_INNER), lambda i: (i, COL_XS // SSD_INNER)),
            pl.BlockSpec((rows, gn), lambda i: (i, COL_B // gn)),
            pl.BlockSpec((rows, gn), lambda i: (i, COL_C // gn)),
            pl.BlockSpec((rows, LANES), lambda i: (i, 0)),
            full((SSD_CONV, SSD_INNER)), full((SSD_CONV, gn)), full((SSD_CONV, gn)),
            full((1, SSD_INNER)), full((1, gn)), full((1, gn)),
            full((1, LANES)), full((1, LANES)), full((1, SSD_INNER)),
        ],
        out_specs=pl.BlockSpec((rows, SSD_INNER), lambda i: (i, 0)),
        out_shape=jax.ShapeDtypeStruct((t, SSD_INNER), BF16),
        scratch_shapes=[
            pltpu.VMEM((SUBLANES, SSD_INNER), F32),
            pltpu.VMEM((SUBLANES, gn), F32),
            pltpu.VMEM((SUBLANES, gn), F32),
            pltpu.VMEM((rows, SSD_INNER), F32),
            pltpu.VMEM((rows, gn), F32),
            pltpu.VMEM((rows, gn), F32),
            pltpu.VMEM((SSD_STATE, SSD_INNER), F32),
        ],
        compiler_params=pltpu.CompilerParams(
            dimension_semantics=("arbitrary",), vmem_limit_bytes=VMEM_LIMIT),
        name="ssd",
    )(u, u, u, dt_raw,
      cw[:, :SSD_INNER], cw[:, SSD_INNER:SSD_INNER + gn], cw[:, SSD_INNER + gn:],
      cb[None, :SSD_INNER], cb[None, SSD_INNER:SSD_INNER + gn], cb[None, SSD_INNER + gn:],
      jnp.pad(dt_bias, (0, pad))[None, :], jnp.pad(a_log, (0, pad))[None, :],
      jnp.repeat(d_skip, SSD_HEADDIM)[None, :])


def _post_kernel(y_ref, z_ref, scb_ref, scc_ref, scx_ref, gate_ref, x_ref,
                 nw_ref, wa_ref, csw_ref, wb_ref, bg_ref, wo_ref, g1_ref, lng_ref, lnb_ref,
                 sh2_ref, sc2_ref, wr_ref, br_ref,
                 x1_ref, h2_ref, prob_ref, idx_ref, lrank_ref, cst_ref, sidt_ref, runtab_ref, cnt_ref,
                 sbuf, run, *, rows, n_tok):
    @pl.when(pl.program_id(0) == 0)
    def _():
        sbuf[...] = jnp.zeros(sbuf.shape, F32)
        run[...] = jnp.zeros(run.shape, F32)

    z = z_ref[...].astype(F32)
    yg = y_ref[...].astype(F32) * (z * _sigmoid(z))
    ms = jnp.mean(yg * yg, axis=-1, keepdims=True)
    yn = yg * lax.rsqrt(ms + RMS_EPS) * nw_ref[...]
    u_a = _dot(yn.astype(BF16), wa_ref[...])

    cx = scc_ref[...].astype(F32) * scx_ref[...].astype(F32)
    ext = jnp.concatenate([sbuf[...], cx], axis=0)
    v = csw_ref[SC_CONV - 1:SC_CONV, :] * cx
    for j in range(1, SC_CONV):
        v = v + csw_ref[SC_CONV - 1 - j:SC_CONV - j, :] * pltpu.roll(ext, j, axis=0)[SUBLANES:, :]
    sbuf[...] = cx[rows - SUBLANES:, :]
    u_b = _dot((scb_ref[...].astype(F32) * v).astype(BF16), wb_ref[...])

    gl = gate_ref[...].astype(F32) + bg_ref[...]
    merged = _sigmoid(gl[:, :D_MODEL]) * u_a + _sigmoid(gl[:, D_MODEL:]) * u_b
    mix = _dot(merged.astype(BF16), wo_ref[...])
    x1 = _layer_norm(DN_ALPHA * x_ref[...] + g1_ref[...] * mix) * lng_ref[...] + lnb_ref[...]
    x1_ref[...] = x1
    h2 = _layer_norm(x1) * (1.0 + sc2_ref[...]) + sh2_ref[...]
    half = D_MODEL // 2
    h2_ref[...] = _pack_bf16_pair(h2[:, :half], h2[:, half:])

    logits = _dot_split(h2, wr_ref[...]) + br_ref[...]
    lane = lax.broadcasted_iota(jnp.int32, (rows, LANES), 1).astype(F32)
    work = logits
    onehots, vals = [], []
    idx_out = jnp.zeros((rows, LANES), F32)
    for k in range(TOP_K):
        m = jnp.max(work, axis=-1, keepdims=True)
        ik = jnp.min(jnp.where(work == m, lane, float(LANES)), axis=-1, keepdims=True)
        oh = lane == ik
        onehots.append(oh)
        vals.append(m)
        idx_out = jnp.where(lane == float(k), ik, idx_out)
        work = jnp.where(oh, -jnp.inf, work)
    es = [jnp.exp(vk - vals[0]) for vk in vals]
    denom = es[0] + es[1] + es[2] + es[3]
    prob_out = jnp.zeros((rows, LANES), F32)
    for k in range(TOP_K):
        prob_out = jnp.where(lane == float(k), es[k] / denom, prob_out)

    sel = jnp.zeros((rows, LANES), F32)
    kk = jnp.zeros((rows, LANES), F32)
    for k, oh in enumerate(onehots):
        sel = sel + jnp.where(oh, 1.0, 0.0)
        kk = kk + jnp.where(oh, float(k), 0.0)
    ri = lax.broadcasted_iota(jnp.int32, (rows, rows), 0)
    rj = lax.broadcasted_iota(jnp.int32, (rows, rows), 1)
    strict = jnp.where(ri > rj, 1.0, 0.0).astype(BF16)
    in_block = _dot(strict, sel.astype(BF16))
    base = in_block + run[...]
    lrank_out = jnp.zeros((rows, LANES), F32)
    for k in range(TOP_K):
        rk = jnp.sum(jnp.where(onehots[k], in_block, 0.0), axis=-1, keepdims=True)
        lrank_out = jnp.where(lane == float(k), rk, lrank_out)
    idx_ref[...] = idx_out.astype(jnp.int32)
    lrank_ref[...] = lrank_out.astype(jnp.int32)
    tok = (pl.program_id(0) * rows
           + lax.broadcasted_iota(jnp.int32, (rows, LANES), 0)).astype(F32)
    picked = sel > 0.0
    cs = jnp.where(picked, base + 1.0, 0.0)
    sid = jnp.where(picked, kk * float(n_tok) + tok, 0.0)
    cst_ref[...] = cs.T[:N_EXPERTS, :]
    sidt_ref[...] = sid.T[:N_EXPERTS, :]
    runtab_ref[0] = run[...].astype(jnp.int32)
    run[...] = run[...] + jnp.sum(sel, axis=0, keepdims=True)
    prob_ref[...] = prob_out
    cnt_ref[...] = run[...].astype(jnp.int32)


def _post(y, u, x, vecs, mats, rows):
    t = x.shape[0]
    kern = functools.partial(_post_kernel, rows=rows, n_tok=t)
    row = lambda w: pl.BlockSpec((1, w), lambda i: (0, 0))
    mat = lambda a, b: pl.BlockSpec((a, b), lambda i: (0, 0))
    ublk = lambda w, col: pl.BlockSpec((rows, w), lambda i: (i, col // w))
    nw, csw, bg, g1, lng, lnb, sh2, sc2, br = vecs
    wa, wb, wo, wr = mats
    tok = pl.BlockSpec((rows, LANES), lambda i: (i, 0))
    return pl.pallas_call(
        kern,
        grid=(t // rows,),
        in_specs=[
            pl.BlockSpec((rows, SSD_INNER), lambda i: (i, 0)),
            ublk(SSD_INNER, COL_Z), ublk(D_MODEL, COL_SCB), ublk(D_MODEL, COL_SCC),
            ublk(D_MODEL, COL_SCX), ublk(2 * D_MODEL, COL_GATE),
            pl.BlockSpec((rows, D_MODEL), lambda i: (i, 0)),
            row(SSD_INNER), mat(SSD_INNER, D_MODEL), mat(SC_CONV, D_MODEL), mat(D_MODEL, D_MODEL),
            row(2 * D_MODEL), mat(D_MODEL, D_MODEL), row(D_MODEL), row(D_MODEL), row(D_MODEL),
            row(D_MODEL), row(D_MODEL), mat(D_MODEL, LANES), row(LANES),
        ],
        out_specs=[
            pl.BlockSpec((rows, D_MODEL), lambda i: (i, 0)),
            pl.BlockSpec((rows, D_MODEL // 2), lambda i: (i, 0)),
            tok, tok, tok,
            pl.BlockSpec((N_EXPERTS, rows), lambda i: (0, i)),
            pl.BlockSpec((N_EXPERTS, rows), lambda i: (0, i)),
            pl.BlockSpec((1, 1, LANES), lambda i: (i, 0, 0)),
            pl.BlockSpec((1, LANES), lambda i: (0, 0)),
        ],
        out_shape=[
            jax.ShapeDtypeStruct((t, D_MODEL), F32),
            jax.ShapeDtypeStruct((t, D_MODEL // 2), jnp.uint32),
            jax.ShapeDtypeStruct((t, LANES), F32),
            jax.ShapeDtypeStruct((t, LANES), jnp.int32),
            jax.ShapeDtypeStruct((t, LANES), jnp.int32),
            jax.ShapeDtypeStruct((N_EXPERTS, t), F32),
            jax.ShapeDtypeStruct((N_EXPERTS, t), F32),
            jax.ShapeDtypeStruct((t // rows, 1, LANES), jnp.int32),
            jax.ShapeDtypeStruct((1, LANES), jnp.int32),
        ],
        scratch_shapes=[
            pltpu.VMEM((SUBLANES, D_MODEL), F32),
            pltpu.VMEM((1, LANES), F32),
        ],
        compiler_params=pltpu.CompilerParams(
            dimension_semantics=("arbitrary",), vmem_limit_bytes=VMEM_LIMIT),
        name="post_mix",
    )(y, u, u, u, u, u, x, nw, wa, csw, wb, bg, wo, g1, lng, lnb, sh2, sc2, wr, br)


def _invert_kernel(vt_ref, ve_ref, nv_ref, gs_ref, run_ref, cst_ref, sidt_ref, o_ref, ptr,
                   *, tile, tok_tile, n_tok_tiles):
    v = pl.program_id(0)
    i = vt_ref[v]
    e = ve_ref[v]
    vp = jnp.maximum(v - 1, 0)
    new_tile = jnp.logical_or(v == 0, vt_ref[vp] != i)
    new_exp = jnp.logical_or(v == 0, ve_ref[vp] != e)

    @pl.when(new_exp)
    def _():
        ptr[0] = 0

    @pl.when(new_tile)
    def _():
        o_ref[...] = jnp.zeros(o_ref.shape, jnp.int32)

    @pl.when(v < nv_ref[0])
    def _():
        g0 = gs_ref[e]
        row0 = i * tile
        ra = jnp.maximum(g0, row0) - g0
        rb = jnp.minimum(gs_ref[e + 1], row0 + tile) - g0
        b_lo = lax.while_loop(lambda b: run_ref[(b + 1) * N_EXPERTS + e] <= ra,
                              lambda b: b + 1, ptr[0])
        ptr[0] = b_lo
        b_hi = lax.while_loop(
            lambda b: jnp.logical_and(b < n_tok_tiles, run_ref[b * N_EXPERTS + e] < rb),
            lambda b: b + 1, b_lo)
        n_blk = tile // LANES
        row_in_blk = lax.broadcasted_iota(jnp.int32, (LANES, LANES), 0).astype(F32)
        firsts = [(row0 - g0 + 1 + k * LANES).astype(F32) for k in range(n_blk)]

        def body(b, accs):
            c0 = pl.multiple_of(b * tok_tile, tok_tile)
            cs_row = cst_ref[pl.ds(e, 1), pl.ds(c0, tok_tile)]
            sid_row = sidt_ref[pl.ds(e, 1), pl.ds(c0, tok_tile)]
            out = []
            for k in range(n_blk):
                rel = cs_row - firsts[k]
                acc = accs[k]
                for j in range(tok_tile // LANES):
                    lanes = slice(j * LANES, (j + 1) * LANES)
                    acc = acc + jnp.where(rel[:, lanes] == row_in_blk, sid_row[:, lanes], 0.0)
                out.append(acc)
            return tuple(out)

        accs = lax.fori_loop(b_lo, b_hi, body,
                             tuple(jnp.zeros((LANES, LANES), F32) for _ in range(n_blk)))
        for k in range(n_blk):
            contrib = jnp.sum(accs[k].T, axis=0, keepdims=True)
            lanes = slice(k * LANES, (k + 1) * LANES)
            o_ref[0, :, lanes] = o_ref[0, :, lanes] + contrib.astype(jnp.int32)


def _invert(v_tile, v_exp, n_visits, gs, runflat, cst, sidt, n_tiles, tile, tok_tile):
    t = cst.shape[1]
    n_vis = v_tile.shape[0]
    kern = functools.partial(_invert_kernel, tile=tile, tok_tile=tok_tile,
                             n_tok_tiles=t // tok_tile)
    return pl.pallas_call(
        kern,
        grid_spec=pltpu.PrefetchScalarGridSpec(
            num_scalar_prefetch=5,
            grid=(n_vis,),
            in_specs=[
                pl.BlockSpec((N_EXPERTS, t), lambda v, *_: (0, 0)),
                pl.BlockSpec((N_EXPERTS, t), lambda v, *_: (0, 0)),
            ],
            out_specs=pl.BlockSpec((1, 1, tile), lambda v, vt, *_: (vt[v], 0, 0)),
            scratch_shapes=[pltpu.SMEM((1,), jnp.int32)],
        ),
        out_shape=jax.ShapeDtypeStruct((n_tiles, 1, tile), jnp.int32),
        compiler_params=pltpu.CompilerParams(
            dimension_semantics=("arbitrary",), vmem_limit_bytes=VMEM_LIMIT),
        name="moe_invert",
    )(v_tile, v_exp, n_visits, gs, runflat, cst, sidt)


N_FF_BLK = 4
FF_BLK = EXPERT_FF // N_FF_BLK
N_STAGES = 2 * N_FF_BLK


def _experts_kernel(vt_ref, ve_ref, vn_ref, nv_ref, gs_ref,
                    ord_cur, ord_next, h_hbm, wgu_hbm, bgu_ref, wd_hbm, bd_ref,
                    o_ref, h_vmem, x_a, x_b, act, wgu_st, wd_st, wgu_bf, wd_bf,
                    hsem, wsem, *, tile, n_tok, layer):
    v = pl.program_id(0)
    nv = nv_ref[0]
    i = vt_ref[v]
    e = ve_ref[v]
    vp = jnp.maximum(v - 1, 0)
    active = v < nv
    new_tile = jnp.logical_or(v == 0, vt_ref[vp] != i)
    new_exp = jnp.logical_or(v == 0, ve_ref[vp] != e)
    even = lax.rem(i, 2) == 0
    half = D_MODEL // 2
    xs = (x_a, x_b)

    def weight_copies(expert):
        return (pltpu.make_async_copy(wgu_hbm.at[layer, expert], wgu_st, wsem.at[0]),
                pltpu.make_async_copy(wd_hbm.at[layer, expert], wd_st, wsem.at[1]))

    def fetch_row(order_ref, r, dst):
        sid = order_ref[r]
        tok = sid & (n_tok - 1) if n_tok & (n_tok - 1) == 0 else lax.rem(sid, n_tok)
        dst[pl.ds(r, 1), :] = h_vmem[pl.ds(tok, 1), :]

    @pl.when(jnp.logical_not(active))
    def _():
        o_ref[...] = jnp.zeros(o_ref.shape, jnp.uint32)

    @pl.when(v == 0)
    def _():
        tokens = pltpu.make_async_copy(h_hbm, h_vmem, hsem)
        tokens.start()
        for c in weight_copies(e):
            c.start()
        tokens.wait()
        for r in range(tile):
            fetch_row(ord_cur, r, x_a)

    @pl.when(jnp.logical_and(active, new_exp))
    def _():
        for c in weight_copies(e):
            c.wait()
        wgu_bf[...] = wgu_st[...].astype(BF16)
        wd_bf[...] = wd_st[...].astype(BF16)

        @pl.when(vn_ref[v] >= 0)
        def _():
            for c in weight_copies(vn_ref[v]):
                c.start()

    rowpos = i * tile + lax.broadcasted_iota(jnp.int32, (tile, 1), 0)
    mine = jnp.logical_and(rowpos >= gs_ref[e], rowpos < gs_ref[e + 1])

    def compute(x_ref, first, between):
        x_lo, x_hi = _unpack_bf16_pair(x_ref[...])
        xb = jnp.concatenate([x_lo.astype(BF16), x_hi.astype(BF16)], axis=1)
        for cb in range(N_FF_BLK):
            c0 = cb * FF_BLK
            g = _dot(xb, wgu_bf[:, c0:c0 + FF_BLK]) + bgu_ref[:, c0:c0 + FF_BLK]
            u = (_dot(xb, wgu_bf[:, EXPERT_FF + c0:EXPERT_FF + c0 + FF_BLK])
                 + bgu_ref[:, EXPERT_FF + c0:EXPERT_FF + c0 + FF_BLK])
            g = jnp.minimum(g, SWIGLU_LIMIT)
            u = jnp.clip(u, -SWIGLU_LIMIT, SWIGLU_LIMIT)
            act[:, c0:c0 + FF_BLK] = ((u + 1.0) * g * _sigmoid(SWIGLU_ALPHA * g)).astype(BF16)
            between(cb)
        for pb in range(N_FF_BLK // 2):
            c0 = pb * FF_BLK
            o_lo = _dot(act[...], wd_bf[:, c0:c0 + FF_BLK]) + bd_ref[:, c0:c0 + FF_BLK]
            between(N_FF_BLK + 2 * pb)
            o_hi = (_dot(act[...], wd_bf[:, half + c0:half + c0 + FF_BLK])
                    + bd_ref[:, half + c0:half + c0 + FF_BLK])
            packed = _pack_bf16_pair(o_lo, o_hi)
            keep = jnp.uint32(0) if first else o_ref[:, c0:c0 + FF_BLK]
            o_ref[:, c0:c0 + FF_BLK] = jnp.where(mine, packed, keep)
            between(N_FF_BLK + 2 * pb + 1)

    per_stage = tile // N_STAGES

    for p in range(2):
        x_cur, x_nxt = xs[p], xs[1 - p]
        on_parity = even if p == 0 else jnp.logical_not(even)

        @pl.when(jnp.logical_and(jnp.logical_and(active, new_tile), on_parity))
        def _(x_cur=x_cur, x_nxt=x_nxt):
            def between(j):
                for r in range(j * per_stage, (j + 1) * per_stage):
                    fetch_row(ord_next, r, x_nxt)

            compute(x_cur, True, between)

        @pl.when(jnp.logical_and(jnp.logical_and(active, jnp.logical_not(new_tile)), on_parity))
        def _(x_cur=x_cur):
            compute(x_cur, False, lambda j: None)


def _experts(v_tile, v_exp, v_next, n_visits, gs, order_ext, h2, w_gu, b_gu, w_down, b_down,
             layer, n_tiles, tile):
    n_vis = v_tile.shape[0]
    n_tok = h2.shape[0]
    ff2 = 2 * EXPERT_FF
    half = D_MODEL // 2
    kern = functools.partial(_experts_kernel, tile=tile, n_tok=n_tok, layer=layer)
    smem_blk = lambda off: pl.BlockSpec(
        (tile,), lambda v, vt, *_: (vt[v] + off,), memory_space=pltpu.SMEM)
    bias = lambda n: pl.BlockSpec(
        (None, None, 1, n), lambda v, vt, ve, *_: (layer, ve[v], 0, 0))
    hbm = pl.BlockSpec(memory_space=pl.ANY)
    return pl.pallas_call(
        kern,
        grid_spec=pltpu.PrefetchScalarGridSpec(
            num_scalar_prefetch=5,
            grid=(n_vis,),
            in_specs=[smem_blk(0), smem_blk(1), hbm, hbm, bias(ff2), hbm, bias(D_MODEL)],
            out_specs=pl.BlockSpec(
                (tile, half),
                lambda v, vt, ve, vn, nv, *_: (jnp.where(v < nv[0], vt[v], n_tiles), 0)),
            scratch_shapes=[
                pltpu.VMEM((n_tok, half), jnp.uint32),
                pltpu.VMEM((tile, half), jnp.uint32),
                pltpu.VMEM((tile, half), jnp.uint32),
                pltpu.VMEM((tile, EXPERT_FF), BF16),
                pltpu.VMEM((D_MODEL, ff2), F32),
                pltpu.VMEM((EXPERT_FF, D_MODEL), F32),
                pltpu.VMEM((D_MODEL, ff2), BF16),
                pltpu.VMEM((EXPERT_FF, D_MODEL), BF16),
                pltpu.SemaphoreType.DMA(()),
                pltpu.SemaphoreType.DMA((2,)),
            ],
        ),
        out_shape=jax.ShapeDtypeStruct(((n_tiles + 1) * tile, half), jnp.uint32),
        compiler_params=pltpu.CompilerParams(
            dimension_semantics=("arbitrary",), vmem_limit_bytes=EXPERTS_VMEM_LIMIT),
        name="moe_experts",
    )(v_tile, v_exp, v_next, n_visits, gs, order_ext, order_ext, h2,
      w_gu, b_gu[:, :, None, :], w_down, b_down[:, :, None, :])


SEG_CHUNK = 64


def _segment_layout(gs, run_lo, run_hi):
    n = run_hi - run_lo
    lead = (gs + run_lo) & (SUBLANES - 1)
    span = jnp.where(n > 0, n + lead, 0)
    log_chunk = SEG_CHUNK.bit_length() - 1
    padded = lax.shift_left(lax.shift_right_logical(span + (SEG_CHUNK - 1), log_chunk), log_chunk)
    return lead, span, padded


def _combine_kernel(gs_ref, run_ref, row_ref, os_hbm, prob_ref, x1_ref, g2_ref, lng_ref,
                    lnb_ref, x2_ref, seg, g_0, g_1, g_2, g_3, sem, *, rows, n_blocks):
    b = pl.program_id(0)
    slot = lax.rem(b, 2)
    max_chunks = rows // SEG_CHUNK + 1

    def segment_copies(blk, s, apply):
        off = jnp.int32(0)
        for e in range(N_EXPERTS):
            first = run_ref[blk * N_EXPERTS + e]
            lead, span, padded = _segment_layout(gs_ref[e], first,
                                                 run_ref[(blk + 1) * N_EXPERTS + e])
            src0 = gs_ref[e] + first - lead
            for c in range(max_chunks):
                @pl.when(c * SEG_CHUNK < span)
                def _(off=off, src0=src0, c=c):
                    src = pl.multiple_of(src0 + c * SEG_CHUNK, SUBLANES)
                    dst = pl.multiple_of(off + c * SEG_CHUNK, SEG_CHUNK)
                    apply(pltpu.make_async_copy(os_hbm.at[pl.ds(src, SEG_CHUNK)],
                                                seg.at[s, pl.ds(dst, SEG_CHUNK)], sem.at[s]))
            off = off + padded

    @pl.when(b == 0)
    def _():
        segment_copies(0, 0, lambda c: c.start())

    @pl.when(b + 1 < n_blocks)
    def _():
        segment_copies(b + 1, 1 - slot, lambda c: c.start())

    segment_copies(b, slot, lambda c: c.wait())

    bufs = (g_0, g_1, g_2, g_3)
    for t in range(rows):
        for k in range(TOP_K):
            bufs[k][pl.ds(t, 1), :] = seg[slot, pl.ds(row_ref[t * TOP_K + k], 1), :]

    prob = prob_ref[...]
    ffn_lo = jnp.zeros((rows, D_MODEL // 2), F32)
    ffn_hi = jnp.zeros((rows, D_MODEL // 2), F32)
    for k in range(TOP_K):
        lo, hi = _unpack_bf16_pair(bufs[k][...])
        ffn_lo = ffn_lo + prob[:, k:k + 1] * lo
        ffn_hi = ffn_hi + prob[:, k:k + 1] * hi
    ffn = jnp.concatenate([ffn_lo, ffn_hi], axis=1)
    x2_ref[...] = (_layer_norm(DN_ALPHA * x1_ref[...] + g2_ref[...] * ffn) * lng_ref[...]
                   + lnb_ref[...])


def _combine(gs, runflat, idx, lrank, out_sorted, probs, x1, g2, lng, lnb, rows):
    t = x1.shape[0]
    half = D_MODEL // 2
    n_blocks = t // rows
    run = runflat.reshape(n_blocks + 1, N_EXPERTS)
    lead, _, padded = _segment_layout(gs[None, :N_EXPERTS], run[:-1], run[1:])
    seg_start = jnp.cumsum(padded, axis=1) - padded + lead
    onehot = idx[:, :TOP_K, None] == jnp.arange(N_EXPERTS, dtype=jnp.int32)[None, None, :]
    start_tk = jnp.sum(jnp.where(onehot, jnp.repeat(seg_start, rows, axis=0)[:, None, :], 0),
                       axis=-1)
    row_flat = (start_tk + lrank[:, :TOP_K]).reshape(-1).astype(jnp.int32)
    seg_rows = rows * TOP_K + N_EXPERTS * (SEG_CHUNK + SUBLANES)
    kern = functools.partial(_combine_kernel, rows=rows, n_blocks=n_blocks)
    row = pl.BlockSpec((1, D_MODEL), lambda i, *_: (0, 0))
    smem = pl.BlockSpec((rows * TOP_K,), lambda i, *_: (i,), memory_space=pltpu.SMEM)
    return pl.pallas_call(
        kern,
        grid_spec=pltpu.PrefetchScalarGridSpec(
            num_scalar_prefetch=2,
            grid=(n_blocks,),
            in_specs=[
                smem,
                pl.BlockSpec(memory_space=pl.ANY),
                pl.BlockSpec((rows, LANES), lambda i, *_: (i, 0)),
                pl.BlockSpec((rows, D_MODEL), lambda i, *_: (i, 0)),
                row, row, row,
            ],
            out_specs=pl.BlockSpec((rows, D_MODEL), lambda i, *_: (i, 0)),
            scratch_shapes=[
                pltpu.VMEM((2, seg_rows, half), jnp.uint32),
                pltpu.VMEM((rows, half), jnp.uint32),
                pltpu.VMEM((rows, half), jnp.uint32),
                pltpu.VMEM((rows, half), jnp.uint32),
                pltpu.VMEM((rows, half), jnp.uint32),
                pltpu.SemaphoreType.DMA((2,)),
            ],
        ),
        out_shape=jax.ShapeDtypeStruct((t, D_MODEL), F32),
        compiler_params=pltpu.CompilerParams(
            dimension_semantics=("arbitrary",), vmem_limit_bytes=VMEM_LIMIT),
        name="moe_combine",
    )(gs, runflat, row_flat, out_sorted, probs, x1, g2, lng, lnb)


def _tile(t, pref):
    return pref if t % pref == 0 else t


def _visit_plan(counts, tile, n_vis):
    ge = jnp.cumsum(counts)
    gs = ge - counts
    t_lo = gs // tile
    nt = jnp.where(counts > 0, (ge - 1) // tile - t_lo + 1, 0)
    vend = jnp.cumsum(nt)
    vbase = vend - nt
    n_visits = vend[-1:]
    vc = jnp.minimum(jnp.arange(n_vis, dtype=jnp.int32), n_visits[0] - 1)
    v_exp = jnp.sum((vend[None, :] <= vc[:, None]).astype(jnp.int32), axis=1)
    onehot = (v_exp[:, None] == jnp.arange(N_EXPERTS, dtype=jnp.int32)[None, :]).astype(jnp.int32)
    v_tile = jnp.sum(onehot * (t_lo - vbase)[None, :], axis=1) + vc
    gs33 = jnp.concatenate([gs, ge[-1:]])
    after = jnp.sum(onehot * vend[None, :], axis=1)
    e_after = jnp.sum((vend[None, :] <= after[:, None]).astype(jnp.int32), axis=1)
    v_next = jnp.where(after < n_visits[0], e_after, -1)
    return (v_tile.astype(jnp.int32), v_exp.astype(jnp.int32), v_next.astype(jnp.int32),
            n_visits.astype(jnp.int32), gs33.astype(jnp.int32))


def kernel(x, c, w_ada, b_ada, w_in, conv_ssd_w, conv_ssd_b, dt_bias, a_log, d_skip, ssd_norm_w,
           w_ssd_out, conv_short_w, w_short_out, b_gate, w_o, ln1_g, ln1_b, w_router, b_router,
           w_gu, b_gu, w_down, b_down, ln2_g, ln2_b):
    batch, seq, d = x.shape
    assert batch == 1 and d == D_MODEL
    depth = w_in.shape[0]
    t = seq
    xt = x.reshape(t, d)

    tm_in = _tile(t, 1024)
    rows_ssd = _tile(t, 512)
    chunk = 128
    rows_post = 256
    tile_e = 256
    assert t % rows_post == 0
    n_tiles = t * TOP_K // tile_e
    n_vis = n_tiles + N_EXPERTS

    mods = _ada_mod(c, w_ada, b_ada)

    s0 = SSD_INNER
    s1 = s0 + SSD_INNER + 2 * SSD_GROUPS * SSD_STATE
    s2 = s1 + SSD_HEADS
    w_main = jnp.concatenate([w_in[:, :, :s1].astype(BF16), w_in[:, :, s2:].astype(BF16)], axis=-1)
    w_dt = jnp.pad(w_in[:, :, s1:s2], ((0, 0), (0, 0), (0, LANES - SSD_HEADS)))
    w_r = jnp.pad(w_router, ((0, 0), (0, 0), (0, LANES - N_EXPERTS)))
    b_r = jnp.pad(b_router, ((0, 0), (0, LANES - N_EXPERTS)), constant_values=NEG_BIG)
    wa_bf = w_ssd_out.astype(BF16)
    wb_bf = w_short_out.astype(BF16)
    wo_bf = w_o.astype(BF16)
    cool = jnp.zeros((tile_e,), jnp.int32)

    for l in range(depth):
        m = mods[l]
        sh1, sc1, g1, sh2, sc2, g2 = [m[:, k * d:(k + 1) * d] for k in range(N_ADA)]
        u, dt_raw = _in_proj(xt, sh1, sc1, w_main[l], w_dt[l], tm_in, 2048)
        y = _ssd(u, dt_raw, conv_ssd_w[l], conv_ssd_b[l], dt_bias[l], a_log[l], d_skip[l],
                 rows_ssd, chunk)
        vecs = (ssd_norm_w[l][None, :], conv_short_w[l], b_gate[l][None, :], g1,
                ln1_g[l][None, :], ln1_b[l][None, :], sh2, sc2, b_r[l][None, :])
        x1, h2, probs, idx, lrank, cst, sidt, runtab, cnt = _post(
            y, u, xt, vecs, (wa_bf[l], wb_bf[l], wo_bf[l], w_r[l]), rows_post)

        v_tile, v_exp, v_next, n_visits, gs = _visit_plan(cnt[0, :N_EXPERTS], tile_e, n_vis)
        runflat = jnp.concatenate([runtab[:, 0, :N_EXPERTS], cnt[:, :N_EXPERTS]]).reshape(-1)
        order = _invert(v_tile, v_exp, n_visits, gs, runflat, cst, sidt, n_tiles, tile_e,
                        rows_post)
        order_ext = jnp.concatenate([order.reshape(-1), cool])
        out_sorted = _experts(v_tile, v_exp, v_next, n_visits, gs, order_ext, h2, w_gu, b_gu,
                              w_down, b_down, l, n_tiles, tile_e)
        xt = _combine(gs, runflat, idx, lrank, out_sorted, probs, x1, g2, ln2_g[l][None, :],
                      ln2_b[l][None, :], rows_post)

    return xt.reshape(batch, seq, d)
```

```python
import functools

import jax
import jax.numpy as jnp
from jax import lax
from jax.experimental import pallas as pl
from jax.experimental.pallas import tpu as pltpu

F32 = jnp.float32
BF16 = jnp.bfloat16

D_MODEL = 1024
SSD_INNER = 2048
SSD_HEADS = 32
SSD_HEADDIM = 64
SSD_GROUPS = 4
SSD_STATE = 128
SSD_CONV = 4
SC_CONV = 3
N_EXPERTS = 32
TOP_K = 4
EXPERT_FF = 1024
SWIGLU_LIMIT = 7.0
SWIGLU_ALPHA = 1.702
DEPTH = 4
DN_ALPHA = (2.0 * DEPTH) ** 0.25
LN_EPS = 1e-5
RMS_EPS = 1e-5
N_ADA = 6

LANES = 128
SUBLANES = 8
U_MAIN = 10240
U_BLK = 2048
COL_Z = 0
COL_XS = 2048
COL_SCC = 4096
COL_SCX = 5120
COL_GATE = 6144
COL_B = 8192
COL_C = 8704
COL_SCB = 9216
NEG_BIG = -1e30

VMEM_LIMIT = 56 * 1024 * 1024
EXPERTS_VMEM_LIMIT = 60 * 1024 * 1024


def _sigmoid(v):
    return 1.0 / (1.0 + jnp.exp(-v))


def _softplus(v):
    return jnp.maximum(v, 0.0) + jnp.log(1.0 + jnp.exp(-jnp.abs(v)))


def _layer_norm(v):
    mu = jnp.mean(v, axis=-1, keepdims=True)
    vc = v - mu
    var = jnp.mean(vc * vc, axis=-1, keepdims=True)
    return vc * lax.rsqrt(var + LN_EPS)


def _split3(v):
    hi = v.astype(BF16)
    r1 = v - hi.astype(F32)
    mid = r1.astype(BF16)
    lo = (r1 - mid.astype(F32)).astype(BF16)
    return hi, mid, lo


def _dot(a, b):
    return jnp.dot(a, b, preferred_element_type=F32)


def _pack_bf16_pair(lo, hi):
    lo_bits = pltpu.bitcast(lo.astype(BF16).astype(F32), jnp.uint32)
    hi_bits = pltpu.bitcast(hi.astype(BF16).astype(F32), jnp.uint32)
    return lax.shift_right_logical(lo_bits, jnp.uint32(16)) | (hi_bits & jnp.uint32(0xFFFF0000))


def _unpack_bf16_pair(w):
    lo = pltpu.bitcast(lax.shift_left(w, jnp.uint32(16)), F32)
    hi = pltpu.bitcast(w & jnp.uint32(0xFFFF0000), F32)
    return lo, hi


def _dot_split(a, b):
    a_hi = a.astype(BF16)
    a_mid = (a - a_hi.astype(F32)).astype(BF16)
    b_hi = b.astype(BF16)
    b_mid = (b - b_hi.astype(F32)).astype(BF16)
    return _dot(a_hi, b_hi) + _dot(a_hi, b_mid) + _dot(a_mid, b_hi)


def _dot_exact_lhs(a_bf16, v):
    hi, mid, lo = _split3(v)
    return _dot(a_bf16, hi) + _dot(a_bf16, mid) + _dot(a_bf16, lo)


def _ada_kernel(c_ref, w_ref, b_ref, o_ref):
    c = c_ref[...]
    s = c * _sigmoid(c)
    o_ref[0] = jnp.sum(w_ref[0] * s, axis=0, keepdims=True) + b_ref[0]


def _ada_mod(c, w_ada, b_ada):
    depth, d, n = w_ada.shape
    tn = 1024
    return pl.pallas_call(
        _ada_kernel,
        grid=(depth, n // tn),
        in_specs=[
            pl.BlockSpec((d, 1), lambda l, j: (0, 0)),
            pl.BlockSpec((1, d, tn), lambda l, j: (l, 0, j)),
            pl.BlockSpec((1, 1, tn), lambda l, j: (l, 0, j)),
        ],
        out_specs=pl.BlockSpec((1, 1, tn), lambda l, j: (l, 0, j)),
        out_shape=jax.ShapeDtypeStruct((depth, 1, n), F32),
        compiler_params=pltpu.CompilerParams(
            dimension_semantics=("arbitrary", "arbitrary"), vmem_limit_bytes=VMEM_LIMIT),
        name="ada_mod",
    )(c.reshape(d, 1), w_ada, b_ada.reshape(depth, 1, n))


def _inproj_kernel(x_ref, sh_ref, sc_ref, wa_ref, wb_ref, wc_ref, wdt_ref, u_ref, dt_ref, h_scr):
    j = pl.program_id(1)

    @pl.when(j == 0)
    def _():
        h = _layer_norm(x_ref[...]) * (1.0 + sc_ref[...]) + sh_ref[...]
        h_scr[...] = h.astype(BF16)
        dt_ref[...] = _dot_split(h, wdt_ref[...])

    @pl.when(j < 2)
    def _():
        u_ref[...] = _dot(h_scr[...], wa_ref[...]).astype(BF16)

    @pl.when(jnp.logical_and(j >= 2, j < 4))
    def _():
        u_ref[...] = _dot(h_scr[...], wb_ref[...]).astype(BF16)

    @pl.when(j == 4)
    def _():
        u_ref[...] = _dot(h_scr[...], wc_ref[...]).astype(BF16)


def _in_proj(x, sh, sc, w_a, w_b, w_c, w_dt, tm):
    t, d = x.shape
    tn = U_BLK
    assert w_a.shape[1] == 2 * tn and w_b.shape[1] == 2 * tn and w_c.shape[1] == tn
    return pl.pallas_call(
        _inproj_kernel,
        grid=(t // tm, U_MAIN // tn),
        in_specs=[
            pl.BlockSpec((tm, d), lambda i, j: (i, 0)),
            pl.BlockSpec((1, d), lambda i, j: (0, 0)),
            pl.BlockSpec((1, d), lambda i, j: (0, 0)),
            pl.BlockSpec((d, tn), lambda i, j: (0, jnp.minimum(j, 1))),
            pl.BlockSpec((d, tn), lambda i, j: (0, jnp.clip(j - 2, 0, 1))),
            pl.BlockSpec((d, tn), lambda i, j: (0, 0)),
            pl.BlockSpec((d, LANES), lambda i, j: (0, 0)),
        ],
        out_specs=[
            pl.BlockSpec((tm, tn), lambda i, j: (i, j)),
            pl.BlockSpec((tm, LANES), lambda i, j: (i, 0)),
        ],
        out_shape=[
            jax.ShapeDtypeStruct((t, U_MAIN), BF16),
            jax.ShapeDtypeStruct((t, LANES), F32),
        ],
        scratch_shapes=[pltpu.VMEM((tm, d), BF16)],
        compiler_params=pltpu.CompilerParams(
            dimension_semantics=("arbitrary", "arbitrary"), vmem_limit_bytes=VMEM_LIMIT),
        name="in_proj",
    )(x, sh, sc, w_a, w_b, w_c, w_dt)


def _ssd_kernel(xs_ref, b_ref, c_ref, dt_ref, cwx_ref, cwb_ref, cwc_ref, cbx_ref, cbb_ref,
                cbc_ref, dtb_ref, alog_ref, dskip_ref, y_ref,
                xbuf, bbuf, cbuf, xcs, bcs, ccs, state, *, chunk, rows):
    @pl.when(pl.program_id(0) == 0)
    def _():
        xbuf[...] = jnp.zeros(xbuf.shape, F32)
        bbuf[...] = jnp.zeros(bbuf.shape, F32)
        cbuf[...] = jnp.zeros(cbuf.shape, F32)
        state[...] = jnp.zeros(state.shape, F32)

    def conv_silu(in_ref, tail, w_ref, bias_ref, out_scr):
        x = in_ref[...].astype(F32)
        ext = jnp.concatenate([tail[...], x], axis=0)
        acc = bias_ref[...] + w_ref[SSD_CONV - 1:SSD_CONV, :] * x
        for j in range(1, SSD_CONV):
            shifted = pltpu.roll(ext, j, axis=0)[SUBLANES:, :]
            acc = acc + w_ref[SSD_CONV - 1 - j:SSD_CONV - j, :] * shifted
        out_scr[...] = acc * _sigmoid(acc)
        tail[...] = x[rows - SUBLANES:, :]

    conv_silu(xs_ref, xbuf, cwx_ref, cbx_ref, xcs)
    conv_silu(b_ref, bbuf, cwb_ref, cbb_ref, bcs)
    conv_silu(c_ref, cbuf, cwc_ref, cbc_ref, ccs)

    li = lax.broadcasted_iota(jnp.int32, (chunk, chunk), 0)
    si = lax.broadcasted_iota(jnp.int32, (chunk, chunk), 1)
    causal = li >= si
    tri = jnp.where(causal, 1.0, 0.0).astype(BF16)
    first_half = lax.broadcasted_iota(jnp.int32, (1, LANES), 1) < SSD_HEADDIM
    a_row = -jnp.exp(alog_ref[...])
    heads_per_group = SSD_HEADS // SSD_GROUPS

    def chunk_body(ci, carry):
        r0 = pl.multiple_of(ci * chunk, chunk)
        dt = _softplus(dt_ref[pl.ds(r0, chunk), :] + dtb_ref[...])
        la = dt * a_row
        acum = _dot_exact_lhs(tri, la)
        acum_t = acum.T
        dt_t = dt.T
        last_t = acum_t[:, chunk - 1:chunk]
        w_t = jnp.exp(last_t - acum_t) * dt_t
        cdec_t = jnp.exp(last_t)

        cb = []
        bt = []
        cg = []
        for g in range(SSD_GROUPS):
            bg = bcs[pl.ds(r0, chunk), g * SSD_STATE:(g + 1) * SSD_STATE]
            cgv = ccs[pl.ds(r0, chunk), g * SSD_STATE:(g + 1) * SSD_STATE]
            btg = bg.T
            bt.append(btg)
            cg.append(cgv)
            cb.append(_dot(cgv.astype(BF16), btg.astype(BF16)))

        for pr in range(SSD_HEADS // 2):
            lo = pr * LANES
            xs_pair = xcs[pl.ds(r0, chunk), lo:lo + LANES]
            xs_bf = xs_pair.astype(BF16)
            prev = state[:, lo:lo + LANES]
            lhs_m, lhs_c, lhs_b, cd = [], [], [], []
            for h in (2 * pr, 2 * pr + 1):
                g = h // heads_per_group
                col = jnp.broadcast_to(acum[:, h:h + 1], (chunk, chunk))
                row = acum_t[h:h + 1, :]
                dec = jnp.exp(jnp.where(causal, col - row, NEG_BIG))
                lhs_m.append((cb[g] * dec * dt_t[h:h + 1, :]).astype(BF16))
                lhs_c.append((cg[g] * jnp.exp(col)).astype(BF16))
                lhs_b.append((bt[g] * w_t[h:h + 1, :]).astype(BF16))
                cd.append(cdec_t[h:h + 1, :])
            out = (_dot(jnp.concatenate(lhs_m, axis=0), xs_bf)
                   + _dot(jnp.concatenate(lhs_c, axis=0), prev.astype(BF16)))
            y_pair = jnp.where(first_half, out[0:chunk, :], out[chunk:2 * chunk, :])
            y_ref[pl.ds(r0, chunk), lo:lo + LANES] = (
                y_pair + xs_pair * dskip_ref[:, lo:lo + LANES]).astype(y_ref.dtype)
            st = _dot(jnp.concatenate(lhs_b, axis=0), xs_bf)
            cd_pair = jnp.where(first_half, cd[0], cd[1])
            state[:, lo:lo + LANES] = prev * cd_pair + jnp.where(
                first_half, st[0:SSD_STATE, :], st[SSD_STATE:2 * SSD_STATE, :])
        return carry

    lax.fori_loop(0, rows // chunk, chunk_body, 0)


def _ssd(u, dt_raw, cw, cb, dt_bias, a_log, d_skip, rows, chunk):
    t = u.shape[0]
    gn = SSD_GROUPS * SSD_STATE
    pad = LANES - SSD_HEADS
    assert chunk == SSD_STATE
    kern = functools.partial(_ssd_kernel, chunk=chunk, rows=rows)
    full = lambda shape: pl.BlockSpec(shape, lambda i: (0, 0))
    return pl.pallas_call(
        kern,
        grid=(t // rows,),
        in_specs=[
            pl.BlockSpec((rows, SSD_INNER), lambda i: (i, COL_XS // SSD_INNER)),
            pl.BlockSpec((rows, gn), lambda i: (i, COL_B // gn)),
            pl.BlockSpec((rows, gn), lambda i: (i, COL_C // gn)),
            pl.BlockSpec((rows, LANES), lambda i: (i, 0)),
            full((SSD_CONV, SSD_INNER)), full((SSD_CONV, gn)), full((SSD_CONV, gn)),
            full((1, SSD_INNER)), full((1, gn)), full((1, gn)),
            full((1, LANES)), full((1, LANES)), full((1, SSD_INNER)),
        ],
        out_specs=pl.BlockSpec((rows, SSD_INNER), lambda i: (i, 0)),
        out_shape=jax.ShapeDtypeStruct((t, SSD_INNER), BF16),
        scratch_shapes=[
            pltpu.VMEM((SUBLANES, SSD_INNER), F32),
            pltpu.VMEM((SUBLANES, gn), F32),
            pltpu.VMEM((SUBLANES, gn), F32),
            pltpu.VMEM((rows, SSD_INNER), F32),
            pltpu.VMEM((rows, gn), F32),
            pltpu.VMEM((rows, gn), F32),
            pltpu.VMEM((SSD_STATE, SSD_INNER), F32),
        ],
        compiler_params=pltpu.CompilerParams(
            dimension_semantics=("arbitrary",), vmem_limit_bytes=VMEM_LIMIT),
        name="ssd",
    )(u, u, u, dt_raw,
      cw[:, :SSD_INNER], cw[:, SSD_INNER:SSD_INNER + gn], cw[:, SSD_INNER + gn:],
      cb[None, :SSD_INNER], cb[None, SSD_INNER:SSD_INNER + gn], cb[None, SSD_INNER + gn:],
      jnp.pad(dt_bias, (0, pad))[None, :], jnp.pad(a_log, (0, pad))[None, :],
      jnp.repeat(d_skip, SSD_HEADDIM)[None, :])


def _post_kernel(y_ref, z_ref, scb_ref, scc_ref, scx_ref, gate_ref, x_ref,
                 nw_ref, wa_ref, csw_ref, wb_ref, bg_ref, wo_ref, g1_ref, lng_ref, lnb_ref,
                 sh2_ref, sc2_ref, wr_ref, br_ref,
                 x1_ref, h2_ref, prob_ref, idx_ref, lrank_ref, cst_ref, sidt_ref, runtab_ref, cnt_ref,
                 sbuf, run, *, rows, n_tok):
    @pl.when(pl.program_id(0) == 0)
    def _():
        sbuf[...] = jnp.zeros(sbuf.shape, F32)
        run[...] = jnp.zeros(run.shape, F32)

    z = z_ref[...].astype(F32)
    yg = y_ref[...].astype(F32) * (z * _sigmoid(z))
    ms = jnp.mean(yg * yg, axis=-1, keepdims=True)
    yn = yg * lax.rsqrt(ms + RMS_EPS) * nw_ref[...]
    u_a = _dot(yn.astype(BF16), wa_ref[...])

    cx = scc_ref[...].astype(F32) * scx_ref[...].astype(F32)
    ext = jnp.concatenate([sbuf[...], cx], axis=0)
    v = csw_ref[SC_CONV - 1:SC_CONV, :] * cx
    for j in range(1, SC_CONV):
        v = v + csw_ref[SC_CONV - 1 - j:SC_CONV - j, :] * pltpu.roll(ext, j, axis=0)[SUBLANES:, :]
    sbuf[...] = cx[rows - SUBLANES:, :]
    u_b = _dot((scb_ref[...].astype(F32) * v).astype(BF16), wb_ref[...])

    gl = gate_ref[...].astype(F32) + bg_ref[...]
    merged = _sigmoid(gl[:, :D_MODEL]) * u_a + _sigmoid(gl[:, D_MODEL:]) * u_b
    mix = _dot(merged.astype(BF16), wo_ref[...])
    x1 = _layer_norm(DN_ALPHA * x_ref[...] + g1_ref[...] * mix) * lng_ref[...] + lnb_ref[...]
    x1_ref[...] = x1
    h2 = _layer_norm(x1) * (1.0 + sc2_ref[...]) + sh2_ref[...]
    half = D_MODEL // 2
    h2_ref[...] = _pack_bf16_pair(h2[:, :half], h2[:, half:])

    logits = _dot_split(h2, wr_ref[...]) + br_ref[...]
    lane = lax.broadcasted_iota(jnp.int32, (rows, LANES), 1).astype(F32)
    work = logits
    onehots, vals = [], []
    idx_out = jnp.zeros((rows, LANES), F32)
    for k in range(TOP_K):
        m = jnp.max(work, axis=-1, keepdims=True)
        ik = jnp.min(jnp.where(work == m, lane, float(LANES)), axis=-1, keepdims=True)
        oh = lane == ik
        onehots.append(oh)
        vals.append(m)
        idx_out = jnp.where(lane == float(k), ik, idx_out)
        work = jnp.where(oh, -jnp.inf, work)
    es = [jnp.exp(vk - vals[0]) for vk in vals]
    denom = es[0] + es[1] + es[2] + es[3]
    prob_out = jnp.zeros((rows, LANES), F32)
    for k in range(TOP_K):
        prob_out = jnp.where(lane == float(k), es[k] / denom, prob_out)

    sel = jnp.zeros((rows, LANES), F32)
    kk = jnp.zeros((rows, LANES), F32)
    for k, oh in enumerate(onehots):
        sel = sel + jnp.where(oh, 1.0, 0.0)
        kk = kk + jnp.where(oh, float(k), 0.0)
    ri = lax.broadcasted_iota(jnp.int32, (rows, rows), 0)
    rj = lax.broadcasted_iota(jnp.int32, (rows, rows), 1)
    strict = jnp.where(ri > rj, 1.0, 0.0).astype(BF16)
    in_block = _dot(strict, sel.astype(BF16))
    base = in_block + run[...]
    lrank_out = jnp.zeros((rows, LANES), F32)
    for k in range(TOP_K):
        rk = jnp.sum(jnp.where(onehots[k], in_block, 0.0), axis=-1, keepdims=True)
        lrank_out = jnp.where(lane == float(k), rk, lrank_out)
    idx_ref[...] = idx_out.astype(jnp.int32)
    lrank_ref[...] = lrank_out.astype(jnp.int32)
    tok = (pl.program_id(0) * rows
           + lax.broadcasted_iota(jnp.int32, (rows, LANES), 0)).astype(F32)
    picked = sel > 0.0
    cs = jnp.where(picked, base + 1.0, 0.0)
    sid = jnp.where(picked, kk * float(n_tok) + tok, 0.0)
    cst_ref[...] = cs.T[:N_EXPERTS, :]
    sidt_ref[...] = sid.T[:N_EXPERTS, :]
    runtab_ref[0] = run[...].astype(jnp.int32)
    run[...] = run[...] + jnp.sum(sel, axis=0, keepdims=True)
    prob_ref[...] = prob_out
    cnt_ref[...] = run[...].astype(jnp.int32)


def _post(y, u, x, vecs, mats, rows):
    t = x.shape[0]
    kern = functools.partial(_post_kernel, rows=rows, n_tok=t)
    row = lambda w: pl.BlockSpec((1, w), lambda i: (0, 0))
    mat = lambda a, b: pl.BlockSpec((a, b), lambda i: (0, 0))
    ublk = lambda w, col: pl.BlockSpec((rows, w), lambda i: (i, col // w))
    nw, csw, bg, g1, lng, lnb, sh2, sc2, br = vecs
    wa, wb, wo, wr = mats
    tok = pl.BlockSpec((rows, LANES), lambda i: (i, 0))
    return pl.pallas_call(
        kern,
        grid=(t // rows,),
        in_specs=[
            pl.BlockSpec((rows, SSD_INNER), lambda i: (i, 0)),
            ublk(SSD_INNER, COL_Z), ublk(D_MODEL, COL_SCB), ublk(D_MODEL, COL_SCC),
            ublk(D_MODEL, COL_SCX), ublk(2 * D_MODEL, COL_GATE),
            pl.BlockSpec((rows, D_MODEL), lambda i: (i, 0)),
            row(SSD_INNER), mat(SSD_INNER, D_MODEL), mat(SC_CONV, D_MODEL), mat(D_MODEL, D_MODEL),
            row(2 * D_MODEL), mat(D_MODEL, D_MODEL), row(D_MODEL), row(D_MODEL), row(D_MODEL),
            row(D_MODEL), row(D_MODEL), mat(D_MODEL, LANES), row(LANES),
        ],
        out_specs=[
            pl.BlockSpec((rows, D_MODEL), lambda i: (i, 0)),
            pl.BlockSpec((rows, D_MODEL // 2), lambda i: (i, 0)),
            tok, tok, tok,
            pl.BlockSpec((N_EXPERTS, rows), lambda i: (0, i)),
            pl.BlockSpec((N_EXPERTS, rows), lambda i: (0, i)),
            pl.BlockSpec((1, 1, LANES), lambda i: (i, 0, 0)),
            pl.BlockSpec((1, LANES), lambda i: (0, 0)),
        ],
        out_shape=[
            jax.ShapeDtypeStruct((t, D_MODEL), F32),
            jax.ShapeDtypeStruct((t, D_MODEL // 2), jnp.uint32),
            jax.ShapeDtypeStruct((t, LANES), F32),
            jax.ShapeDtypeStruct((t, LANES), jnp.int32),
            jax.ShapeDtypeStruct((t, LANES), jnp.int32),
            jax.ShapeDtypeStruct((N_EXPERTS, t), F32),
            jax.ShapeDtypeStruct((N_EXPERTS, t), F32),
            jax.ShapeDtypeStruct((t // rows, 1, LANES), jnp.int32),
            jax.ShapeDtypeStruct((1, LANES), jnp.int32),
        ],
        scratch_shapes=[
            pltpu.VMEM((SUBLANES, D_MODEL), F32),
            pltpu.VMEM((1, LANES), F32),
        ],
        compiler_params=pltpu.CompilerParams(
            dimension_semantics=("arbitrary",), vmem_limit_bytes=VMEM_LIMIT),
        name="post_mix",
    )(y, u, u, u, u, u, x, nw, wa, csw, wb, bg, wo, g1, lng, lnb, sh2, sc2, wr, br)


def _invert_kernel(vt_ref, ve_ref, nv_ref, gs_ref, run_ref, cst_ref, sidt_ref, o_ref, ptr,
                   *, tile, tok_tile, n_tok_tiles):
    v = pl.program_id(0)
    i = vt_ref[v]
    e = ve_ref[v]
    vp = jnp.maximum(v - 1, 0)
    new_tile = jnp.logical_or(v == 0, vt_ref[vp] != i)
    new_exp = jnp.logical_or(v == 0, ve_ref[vp] != e)

    @pl.when(new_exp)
    def _():
        ptr[0] = 0

    @pl.when(new_tile)
    def _():
        o_ref[...] = jnp.zeros(o_ref.shape, jnp.int32)

    @pl.when(v < nv_ref[0])
    def _():
        g0 = gs_ref[e]
        row0 = i * tile
        ra = jnp.maximum(g0, row0) - g0
        rb = jnp.minimum(gs_ref[e + 1], row0 + tile) - g0
        b_lo = lax.while_loop(lambda b: run_ref[(b + 1) * N_EXPERTS + e] <= ra,
                              lambda b: b + 1, ptr[0])
        ptr[0] = b_lo
        b_hi = lax.while_loop(
            lambda b: jnp.logical_and(b < n_tok_tiles, run_ref[b * N_EXPERTS + e] < rb),
            lambda b: b + 1, b_lo)
        n_blk = tile // LANES
        row_in_blk = lax.broadcasted_iota(jnp.int32, (LANES, LANES), 0).astype(F32)
        firsts = [(row0 - g0 + 1 + k * LANES).astype(F32) for k in range(n_blk)]

        def body(b, accs):
            c0 = pl.multiple_of(b * tok_tile, tok_tile)
            cs_row = cst_ref[pl.ds(e, 1), pl.ds(c0, tok_tile)]
            sid_row = sidt_ref[pl.ds(e, 1), pl.ds(c0, tok_tile)]
            out = []
            for k in range(n_blk):
                rel = cs_row - firsts[k]
                acc = accs[k]
                for j in range(tok_tile // LANES):
                    lanes = slice(j * LANES, (j + 1) * LANES)
                    acc = acc + jnp.where(rel[:, lanes] == row_in_blk, sid_row[:, lanes], 0.0)
                out.append(acc)
            return tuple(out)

        accs = lax.fori_loop(b_lo, b_hi, body,
                             tuple(jnp.zeros((LANES, LANES), F32) for _ in range(n_blk)))
        for k in range(n_blk):
            contrib = jnp.sum(accs[k].T, axis=0, keepdims=True)
            lanes = slice(k * LANES, (k + 1) * LANES)
            o_ref[0, :, lanes] = o_ref[0, :, lanes] + contrib.astype(jnp.int32)


def _invert(v_tile, v_exp, n_visits, gs, runflat, cst, sidt, n_tiles, tile, tok_tile):
    t = cst.shape[1]
    n_vis = v_tile.shape[0]
    kern = functools.partial(_invert_kernel, tile=tile, tok_tile=tok_tile,
                             n_tok_tiles=t // tok_tile)
    return pl.pallas_call(
        kern,
        grid_spec=pltpu.PrefetchScalarGridSpec(
            num_scalar_prefetch=5,
            grid=(n_vis,),
            in_specs=[
                pl.BlockSpec((N_EXPERTS, t), lambda v, *_: (0, 0)),
                pl.BlockSpec((N_EXPERTS, t), lambda v, *_: (0, 0)),
            ],
            out_specs=pl.BlockSpec((1, 1, tile), lambda v, vt, *_: (vt[v], 0, 0)),
            scratch_shapes=[pltpu.SMEM((1,), jnp.int32)],
        ),
        out_shape=jax.ShapeDtypeStruct((n_tiles, 1, tile), jnp.int32),
        compiler_params=pltpu.CompilerParams(
            dimension_semantics=("arbitrary",), vmem_limit_bytes=VMEM_LIMIT),
        name="moe_invert",
    )(v_tile, v_exp, n_visits, gs, runflat, cst, sidt)


N_FF_BLK = 4
FF_BLK = EXPERT_FF // N_FF_BLK
N_STAGES = 2 * N_FF_BLK


def _experts_kernel(vt_ref, ve_ref, vn_ref, nv_ref, gs_ref,
                    ord_cur, ord_next, h_hbm, wgu_hbm, bgu_ref, wd_hbm, bd_ref,
                    o_ref, h_vmem, x_a, x_b, act, wgu_st, wd_st, wgu_bf, wd_bf,
                    hsem, wsem, *, tile, n_tok, layer):
    v = pl.program_id(0)
    nv = nv_ref[0]
    i = vt_ref[v]
    e = ve_ref[v]
    vp = jnp.maximum(v - 1, 0)
    active = v < nv
    new_tile = jnp.logical_or(v == 0, vt_ref[vp] != i)
    new_exp = jnp.logical_or(v == 0, ve_ref[vp] != e)
    even = lax.rem(i, 2) == 0
    half = D_MODEL // 2
    xs = (x_a, x_b)

    def weight_copies(expert):
        return (pltpu.make_async_copy(wgu_hbm.at[layer, expert], wgu_st, wsem.at[0]),
                pltpu.make_async_copy(wd_hbm.at[layer, expert], wd_st, wsem.at[1]))

    def fetch_row(order_ref, r, dst):
        sid = order_ref[r]
        tok = sid & (n_tok - 1) if n_tok & (n_tok - 1) == 0 else lax.rem(sid, n_tok)
        dst[pl.ds(r, 1), :] = h_vmem[pl.ds(tok, 1), :]

    @pl.when(jnp.logical_not(active))
    def _():
        o_ref[...] = jnp.zeros(o_ref.shape, jnp.uint32)

    @pl.when(v == 0)
    def _():
        tokens = pltpu.make_async_copy(h_hbm, h_vmem, hsem)
        tokens.start()
        for c in weight_copies(e):
            c.start()
        tokens.wait()
        for r in range(tile):
            fetch_row(ord_cur, r, x_a)

    @pl.when(jnp.logical_and(active, new_exp))
    def _():
        for c in weight_copies(e):
            c.wait()
        wgu_bf[...] = wgu_st[...].astype(BF16)
        wd_bf[...] = wd_st[...].astype(BF16)

        @pl.when(vn_ref[v] >= 0)
        def _():
            for c in weight_copies(vn_ref[v]):
                c.start()

    rowpos = i * tile + lax.broadcasted_iota(jnp.int32, (tile, 1), 0)
    mine = jnp.logical_and(rowpos >= gs_ref[e], rowpos < gs_ref[e + 1])

    def compute(x_ref, first, between):
        x_lo, x_hi = _unpack_bf16_pair(x_ref[...])
        xb = jnp.concatenate([x_lo.astype(BF16), x_hi.astype(BF16)], axis=1)
        for cb in range(N_FF_BLK):
            c0 = cb * FF_BLK
            g = _dot(xb, wgu_bf[:, c0:c0 + FF_BLK]) + bgu_ref[:, c0:c0 + FF_BLK]
            u = (_dot(xb, wgu_bf[:, EXPERT_FF + c0:EXPERT_FF + c0 + FF_BLK])
                 + bgu_ref[:, EXPERT_FF + c0:EXPERT_FF + c0 + FF_BLK])
            g = jnp.minimum(g, SWIGLU_LIMIT)
            u = jnp.clip(u, -SWIGLU_LIMIT, SWIGLU_LIMIT)
            act[:, c0:c0 + FF_BLK] = ((u + 1.0) * g * _sigmoid(SWIGLU_ALPHA * g)).astype(BF16)
            between(cb)
        for pb in range(N_FF_BLK // 2):
            c0 = pb * FF_BLK
            o_lo = _dot(act[...], wd_bf[:, c0:c0 + FF_BLK]) + bd_ref[:, c0:c0 + FF_BLK]
            between(N_FF_BLK + 2 * pb)
            o_hi = (_dot(act[...], wd_bf[:, half + c0:half + c0 + FF_BLK])
                    + bd_ref[:, half + c0:half + c0 + FF_BLK])
            packed = _pack_bf16_pair(o_lo, o_hi)
            keep = jnp.uint32(0) if first else o_ref[:, c0:c0 + FF_BLK]
            o_ref[:, c0:c0 + FF_BLK] = jnp.where(mine, packed, keep)
            between(N_FF_BLK + 2 * pb + 1)

    per_stage = tile // N_STAGES

    for p in range(2):
        x_cur, x_nxt = xs[p], xs[1 - p]
        on_parity = even if p == 0 else jnp.logical_not(even)

        @pl.when(jnp.logical_and(jnp.logical_and(active, new_tile), on_parity))
        def _(x_cur=x_cur, x_nxt=x_nxt):
            def between(j):
                for r in range(j * per_stage, (j + 1) * per_stage):
                    fetch_row(ord_next, r, x_nxt)

            compute(x_cur, True, between)

        @pl.when(jnp.logical_and(jnp.logical_and(active, jnp.logical_not(new_tile)), on_parity))
        def _(x_cur=x_cur):
            compute(x_cur, False, lambda j: None)


def _experts(v_tile, v_exp, v_next, n_visits, gs, order_ext, h2, w_gu, b_gu, w_down, b_down,
             layer, n_tiles, tile):
    n_vis = v_tile.shape[0]
    n_tok = h2.shape[0]
    ff2 = 2 * EXPERT_FF
    half = D_MODEL // 2
    kern = functools.partial(_experts_kernel, tile=tile, n_tok=n_tok, layer=layer)
    smem_blk = lambda off: pl.BlockSpec(
        (tile,), lambda v, vt, *_: (vt[v] + off,), memory_space=pltpu.SMEM)
    bias = lambda n: pl.BlockSpec(
        (None, None, 1, n), lambda v, vt, ve, *_: (layer, ve[v], 0, 0))
    hbm = pl.BlockSpec(memory_space=pl.ANY)
    return pl.pallas_call(
        kern,
        grid_spec=pltpu.PrefetchScalarGridSpec(
            num_scalar_prefetch=5,
            grid=(n_vis,),
            in_specs=[smem_blk(0), smem_blk(1), hbm, hbm, bias(ff2), hbm, bias(D_MODEL)],
            out_specs=pl.BlockSpec(
                (tile, half),
                lambda v, vt, ve, vn, nv, *_: (jnp.where(v < nv[0], vt[v], n_tiles), 0)),
            scratch_shapes=[
                pltpu.VMEM((n_tok, half), jnp.uint32),
                pltpu.VMEM((tile, half), jnp.uint32),
                pltpu.VMEM((tile, half), jnp.uint32),
                pltpu.VMEM((tile, EXPERT_FF), BF16),
                pltpu.VMEM((D_MODEL, ff2), F32),
                pltpu.VMEM((EXPERT_FF, D_MODEL), F32),
                pltpu.VMEM((D_MODEL, ff2), BF16),
                pltpu.VMEM((EXPERT_FF, D_MODEL), BF16),
                pltpu.SemaphoreType.DMA(()),
                pltpu.SemaphoreType.DMA((2,)),
            ],
        ),
        out_shape=jax.ShapeDtypeStruct(((n_tiles + 1) * tile, half), jnp.uint32),
        compiler_params=pltpu.CompilerParams(
            dimension_semantics=("arbitrary",), vmem_limit_bytes=EXPERTS_VMEM_LIMIT),
        name="moe_experts",
    )(v_tile, v_exp, v_next, n_visits, gs, order_ext, order_ext, h2,
      w_gu, b_gu[:, :, None, :], w_down, b_down[:, :, None, :])


SEG_CHUNK = 64


def _segment_layout(gs, run_lo, run_hi):
    n = run_hi - run_lo
    lead = (gs + run_lo) & (SUBLANES - 1)
    span = jnp.where(n > 0, n + lead, 0)
    log_chunk = SEG_CHUNK.bit_length() - 1
    padded = lax.shift_left(lax.shift_right_logical(span + (SEG_CHUNK - 1), log_chunk), log_chunk)
    return lead, span, padded


def _combine_kernel(gs_ref, run_ref, row_ref, os_hbm, prob_ref, x1_ref, g2_ref, lng_ref,
                    lnb_ref, x2_ref, seg, g_0, g_1, g_2, g_3, n_started, sem, *, rows, n_blocks):
    b = pl.program_id(0)
    slot = lax.rem(b, 2)
    max_chunks = rows // SEG_CHUNK + 1

    def chunk_copy(src, dst, s):
        return pltpu.make_async_copy(os_hbm.at[pl.ds(src, SEG_CHUNK)],
                                     seg.at[s, pl.ds(dst, SEG_CHUNK)], sem.at[s])

    def start_segments(blk, s):
        off = jnp.int32(0)
        started = jnp.int32(0)
        for e in range(N_EXPERTS):
            first = run_ref[blk * N_EXPERTS + e]
            lead, span, padded = _segment_layout(gs_ref[e], first,
                                                 run_ref[(blk + 1) * N_EXPERTS + e])
            src0 = gs_ref[e] + first - lead
            for c in range(max_chunks):
                @pl.when(c * SEG_CHUNK < span)
                def _(off=off, src0=src0, c=c):
                    chunk_copy(pl.multiple_of(src0 + c * SEG_CHUNK, SUBLANES),
                               pl.multiple_of(off + c * SEG_CHUNK, SEG_CHUNK), s).start()
            off = off + padded
            started = started + lax.shift_right_logical(padded, SEG_CHUNK.bit_length() - 1)
        n_started[s] = started

    @pl.when(b == 0)
    def _():
        start_segments(0, 0)

    @pl.when(b + 1 < n_blocks)
    def _():
        start_segments(b + 1, 1 - slot)

    def wait_one(_, carry):
        chunk_copy(0, 0, slot).wait()
        return carry

    lax.fori_loop(0, n_started[slot], wait_one, 0)

    bufs = (g_0, g_1, g_2, g_3)
    for t in range(rows):
        for k in range(TOP_K):
            bufs[k][pl.ds(t, 1), :] = seg[slot, pl.ds(row_ref[t * TOP_K + k], 1), :]

    prob = prob_ref[...]
    ffn_lo = jnp.zeros((rows, D_MODEL // 2), F32)
    ffn_hi = jnp.zeros((rows, D_MODEL // 2), F32)
    for k in range(TOP_K):
        lo, hi = _unpack_bf16_pair(bufs[k][...])
        ffn_lo = ffn_lo + prob[:, k:k + 1] * lo
        ffn_hi = ffn_hi + prob[:, k:k + 1] * hi
    ffn = jnp.concatenate([ffn_lo, ffn_hi], axis=1)
    x2_ref[...] = (_layer_norm(DN_ALPHA * x1_ref[...] + g2_ref[...] * ffn) * lng_ref[...]
                   + lnb_ref[...])


def _combine(gs, runflat, idx, lrank, out_sorted, probs, x1, g2, lng, lnb, rows):
    t = x1.shape[0]
    half = D_MODEL // 2
    n_blocks = t // rows
    run = runflat.reshape(n_blocks + 1, N_EXPERTS)
    lead, _, padded = _segment_layout(gs[None, :N_EXPERTS], run[:-1], run[1:])
    seg_start = jnp.cumsum(padded, axis=1) - padded + lead
    onehot = idx[:, :TOP_K, None] == jnp.arange(N_EXPERTS, dtype=jnp.int32)[None, None, :]
    start_tk = jnp.sum(jnp.where(onehot, jnp.repeat(seg_start, rows, axis=0)[:, None, :], 0),
                       axis=-1)
    row_flat = (start_tk + lrank[:, :TOP_K]).reshape(-1).astype(jnp.int32)
    seg_rows = rows * TOP_K + N_EXPERTS * (SEG_CHUNK + SUBLANES)
    kern = functools.partial(_combine_kernel, rows=rows, n_blocks=n_blocks)
    row = pl.BlockSpec((1, D_MODEL), lambda i, *_: (0, 0))
    smem = pl.BlockSpec((rows * TOP_K,), lambda i, *_: (i,), memory_space=pltpu.SMEM)
    return pl.pallas_call(
        kern,
        grid_spec=pltpu.PrefetchScalarGridSpec(
            num_scalar_prefetch=2,
            grid=(n_blocks,),
            in_specs=[
                smem,
                pl.BlockSpec(memory_space=pl.ANY),
                pl.BlockSpec((rows, LANES), lambda i, *_: (i, 0)),
                pl.BlockSpec((rows, D_MODEL), lambda i, *_: (i, 0)),
                row, row, row,
            ],
            out_specs=pl.BlockSpec((rows, D_MODEL), lambda i, *_: (i, 0)),
            scratch_shapes=[
                pltpu.VMEM((2, seg_rows, half), jnp.uint32),
                pltpu.VMEM((rows, half), jnp.uint32),
                pltpu.VMEM((rows, half), jnp.uint32),
                pltpu.VMEM((rows, half), jnp.uint32),
                pltpu.VMEM((rows, half), jnp.uint32),
                pltpu.SMEM((2,), jnp.int32),
                pltpu.SemaphoreType.DMA((2,)),
            ],
        ),
        out_shape=jax.ShapeDtypeStruct((t, D_MODEL), F32),
        compiler_params=pltpu.CompilerParams(
            dimension_semantics=("arbitrary",), vmem_limit_bytes=VMEM_LIMIT),
        name="moe_combine",
    )(gs, runflat, row_flat, out_sorted, probs, x1, g2, lng, lnb)


def _tile(t, pref):
    return pref if t % pref == 0 else t


def _visit_plan(counts, tile, n_vis):
    ge = jnp.cumsum(counts)
    gs = ge - counts
    t_lo = gs // tile
    nt = jnp.where(counts > 0, (ge - 1) // tile - t_lo + 1, 0)
    vend = jnp.cumsum(nt)
    vbase = vend - nt
    n_visits = vend[-1:]
    vc = jnp.minimum(jnp.arange(n_vis, dtype=jnp.int32), n_visits[0] - 1)
    v_exp = jnp.sum((vend[None, :] <= vc[:, None]).astype(jnp.int32), axis=1)
    onehot = (v_exp[:, None] == jnp.arange(N_EXPERTS, dtype=jnp.int32)[None, :]).astype(jnp.int32)
    v_tile = jnp.sum(onehot * (t_lo - vbase)[None, :], axis=1) + vc
    gs33 = jnp.concatenate([gs, ge[-1:]])
    after = jnp.sum(onehot * vend[None, :], axis=1)
    e_after = jnp.sum((vend[None, :] <= after[:, None]).astype(jnp.int32), axis=1)
    v_next = jnp.where(after < n_visits[0], e_after, -1)
    return (v_tile.astype(jnp.int32), v_exp.astype(jnp.int32), v_next.astype(jnp.int32),
            n_visits.astype(jnp.int32), gs33.astype(jnp.int32))


def kernel(x, c, w_ada, b_ada, w_in, conv_ssd_w, conv_ssd_b, dt_bias, a_log, d_skip, ssd_norm_w,
           w_ssd_out, conv_short_w, w_short_out, b_gate, w_o, ln1_g, ln1_b, w_router, b_router,
           w_gu, b_gu, w_down, b_down, ln2_g, ln2_b):
    batch, seq, d = x.shape
    assert batch == 1 and d == D_MODEL
    depth = w_in.shape[0]
    t = seq
    xt = x.reshape(t, d)

    tm_in = _tile(t, 1024)
    rows_ssd = _tile(t, 512)
    chunk = 128
    rows_post = 256
    tile_e = 256
    assert t % rows_post == 0
    n_tiles = t * TOP_K // tile_e
    n_vis = n_tiles + N_EXPERTS

    mods = _ada_mod(c, w_ada, b_ada)

    c_bc = 2 * SSD_INNER
    c_dt = c_bc + 2 * SSD_GROUPS * SSD_STATE
    c_scb = c_dt + SSD_HEADS
    c_scc = c_scb + D_MODEL
    w_ia = w_in[:, :, :c_bc].astype(BF16)
    w_ib = w_in[:, :, c_scc:].astype(BF16)
    w_ic = jnp.concatenate([w_in[:, :, c_bc:c_dt], w_in[:, :, c_scb:c_scc]], axis=-1).astype(BF16)
    w_dt = jnp.pad(w_in[:, :, c_dt:c_scb], ((0, 0), (0, 0), (0, LANES - SSD_HEADS)))
    w_r = jnp.pad(w_router, ((0, 0), (0, 0), (0, LANES - N_EXPERTS)))
    b_r = jnp.pad(b_router, ((0, 0), (0, LANES - N_EXPERTS)), constant_values=NEG_BIG)
    wa_bf = w_ssd_out.astype(BF16)
    wb_bf = w_short_out.astype(BF16)
    wo_bf = w_o.astype(BF16)
    cool = jnp.zeros((tile_e,), jnp.int32)

    for l in range(depth):
        m = mods[l]
        sh1, sc1, g1, sh2, sc2, g2 = [m[:, k * d:(k + 1) * d] for k in range(N_ADA)]
        u, dt_raw = _in_proj(xt, sh1, sc1, w_ia[l], w_ib[l], w_ic[l], w_dt[l], tm_in)
        y = _ssd(u, dt_raw, conv_ssd_w[l], conv_ssd_b[l], dt_bias[l], a_log[l], d_skip[l],
                 rows_ssd, chunk)
        vecs = (ssd_norm_w[l][None, :], conv_short_w[l], b_gate[l][None, :], g1,
                ln1_g[l][None, :], ln1_b[l][None, :], sh2, sc2, b_r[l][None, :])
        x1, h2, probs, idx, lrank, cst, sidt, runtab, cnt = _post(
            y, u, xt, vecs, (wa_bf[l], wb_bf[l], wo_bf[l], w_r[l]), rows_post)

        v_tile, v_exp, v_next, n_visits, gs = _visit_plan(cnt[0, :N_EXPERTS], tile_e, n_vis)
        runflat = jnp.concatenate([runtab[:, 0, :N_EXPERTS], cnt[:, :N_EXPERTS]]).reshape(-1)
        order = _invert(v_tile, v_exp, n_visits, gs, runflat, cst, sidt, n_tiles, tile_e,
                        rows_post)
        order_ext = jnp.concatenate([order.reshape(-1), cool])
        out_sorted = _experts(v_tile, v_exp, v_next, n_visits, gs, order_ext, h2, w_gu, b_gu,
                              w_down, b_down, l, n_tiles, tile_e)
        xt = _combine(gs, runflat, idx, lrank, out_sorted, probs, x1, g2, ln2_g[l][None, :],
                      ln2_b[l][None, :], rows_post)

    return xt.reshape(batch, seq, d)
```

```python
import functools

import jax
import jax.numpy as jnp
from jax import lax
from jax.experimental import pallas as pl
from jax.experimental.pallas import tpu as pltpu

F32 = jnp.float32
BF16 = jnp.bfloat16

D_MODEL = 1024
SSD_INNER = 2048
SSD_HEADS = 32
SSD_HEADDIM = 64
SSD_GROUPS = 4
SSD_STATE = 128
SSD_CONV = 4
SC_CONV = 3
N_EXPERTS = 32
TOP_K = 4
EXPERT_FF = 1024
SWIGLU_LIMIT = 7.0
SWIGLU_ALPHA = 1.702
DEPTH = 4
DN_ALPHA = (2.0 * DEPTH) ** 0.25
LN_EPS = 1e-5
RMS_EPS = 1e-5
N_ADA = 6

LANES = 128
SUBLANES = 8
U_MAIN = 10240
U_BLK = 2048
COL_Z = 0
COL_XS = 2048
COL_SCC = 4096
COL_SCX = 5120
COL_GATE = 6144
COL_B = 8192
COL_C = 8704
COL_SCB = 9216
NEG_BIG = -1e30

VMEM_LIMIT = 56 * 1024 * 1024
EXPERTS_VMEM_LIMIT = 60 * 1024 * 1024


def _sigmoid(v):
    return 1.0 / (1.0 + jnp.exp(-v))


def _softplus(v):
    return jnp.maximum(v, 0.0) + jnp.log(1.0 + jnp.exp(-jnp.abs(v)))


def _layer_norm(v):
    mu = jnp.mean(v, axis=-1, keepdims=True)
    vc = v - mu
    var = jnp.mean(vc * vc, axis=-1, keepdims=True)
    return vc * lax.rsqrt(var + LN_EPS)


def _split3(v):
    hi = v.astype(BF16)
    r1 = v - hi.astype(F32)
    mid = r1.astype(BF16)
    lo = (r1 - mid.astype(F32)).astype(BF16)
    return hi, mid, lo


def _dot(a, b):
    return jnp.dot(a, b, preferred_element_type=F32)


def _pack_bf16_pair(lo, hi):
    lo_bits = pltpu.bitcast(lo.astype(BF16).astype(F32), jnp.uint32)
    hi_bits = pltpu.bitcast(hi.astype(BF16).astype(F32), jnp.uint32)
    return lax.shift_right_logical(lo_bits, jnp.uint32(16)) | (hi_bits & jnp.uint32(0xFFFF0000))


def _unpack_bf16_pair(w):
    lo = pltpu.bitcast(lax.shift_left(w, jnp.uint32(16)), F32)
    hi = pltpu.bitcast(w & jnp.uint32(0xFFFF0000), F32)
    return lo, hi


def _dot_split(a, b):
    a_hi = a.astype(BF16)
    a_mid = (a - a_hi.astype(F32)).astype(BF16)
    b_hi = b.astype(BF16)
    b_mid = (b - b_hi.astype(F32)).astype(BF16)
    return _dot(a_hi, b_hi) + _dot(a_hi, b_mid) + _dot(a_mid, b_hi)


def _dot_exact_lhs(a_bf16, v):
    hi, mid, lo = _split3(v)
    return _dot(a_bf16, hi) + _dot(a_bf16, mid) + _dot(a_bf16, lo)


def _ada_kernel(c_ref, w_ref, b_ref, o_ref):
    c = c_ref[...]
    s = c * _sigmoid(c)
    o_ref[0] = jnp.sum(w_ref[0] * s, axis=0, keepdims=True) + b_ref[0]


def _ada_mod(c, w_ada, b_ada):
    depth, d, n = w_ada.shape
    tn = 1024
    return pl.pallas_call(
        _ada_kernel,
        grid=(depth, n // tn),
        in_specs=[
            pl.BlockSpec((d, 1), lambda l, j: (0, 0)),
            pl.BlockSpec((1, d, tn), lambda l, j: (l, 0, j)),
            pl.BlockSpec((1, 1, tn), lambda l, j: (l, 0, j)),
        ],
        out_specs=pl.BlockSpec((1, 1, tn), lambda l, j: (l, 0, j)),
        out_shape=jax.ShapeDtypeStruct((depth, 1, n), F32),
        compiler_params=pltpu.CompilerParams(
            dimension_semantics=("arbitrary", "arbitrary"), vmem_limit_bytes=VMEM_LIMIT),
        name="ada_mod",
    )(c.reshape(d, 1), w_ada, b_ada.reshape(depth, 1, n))


def _dot_nt(a, b):
    return lax.dot_general(a, b, (((1,), (1,)), ((), ())), preferred_element_type=F32)


def _inproj_kernel(x_ref, sh_ref, sc_ref, w_ref, wdt_ref, u_ref, dt_ref, h_scr):
    @pl.when(pl.program_id(1) == 0)
    def _():
        h = _layer_norm(x_ref[...]) * (1.0 + sc_ref[...]) + sh_ref[...]
        h_scr[...] = h.astype(BF16)
        w = wdt_ref[...]
        h_hi = h.astype(BF16)
        h_mid = (h - h_hi.astype(F32)).astype(BF16)
        w_hi = w.astype(BF16)
        w_mid = (w - w_hi.astype(F32)).astype(BF16)
        dt_ref[...] = _dot_nt(h_hi, w_hi) + _dot_nt(h_hi, w_mid) + _dot_nt(h_mid, w_hi)

    u_ref[...] = _dot_nt(h_scr[...], w_ref[...]).astype(BF16)


def _in_proj(x, sh, sc, w_main, w_dt, tm):
    t, d = x.shape
    tn = U_BLK
    assert w_main.shape == (U_MAIN, d) and w_dt.shape == (LANES, d)
    return pl.pallas_call(
        _inproj_kernel,
        grid=(t // tm, U_MAIN // tn),
        in_specs=[
            pl.BlockSpec((tm, d), lambda i, j: (i, 0)),
            pl.BlockSpec((1, d), lambda i, j: (0, 0)),
            pl.BlockSpec((1, d), lambda i, j: (0, 0)),
            pl.BlockSpec((tn, d), lambda i, j: (j, 0)),
            pl.BlockSpec((LANES, d), lambda i, j: (0, 0)),
        ],
        out_specs=[
            pl.BlockSpec((tm, tn), lambda i, j: (i, j)),
            pl.BlockSpec((tm, LANES), lambda i, j: (i, 0)),
        ],
        out_shape=[
            jax.ShapeDtypeStruct((t, U_MAIN), BF16),
            jax.ShapeDtypeStruct((t, LANES), F32),
        ],
        scratch_shapes=[pltpu.VMEM((tm, d), BF16)],
        compiler_params=pltpu.CompilerParams(
            dimension_semantics=("arbitrary", "arbitrary"), vmem_limit_bytes=VMEM_LIMIT),
        name="in_proj",
    )(x, sh, sc, w_main, w_dt)


def _ssd_kernel(xs_ref, b_ref, c_ref, dt_ref, cwx_ref, cwb_ref, cwc_ref, cbx_ref, cbb_ref,
                cbc_ref, dtb_ref, alog_ref, dskip_ref, y_ref,
                xbuf, bbuf, cbuf, xcs, bcs, ccs, state, *, chunk, rows):
    @pl.when(pl.program_id(0) == 0)
    def _():
        xbuf[...] = jnp.zeros(xbuf.shape, F32)
        bbuf[...] = jnp.zeros(bbuf.shape, F32)
        cbuf[...] = jnp.zeros(cbuf.shape, F32)
        state[...] = jnp.zeros(state.shape, F32)

    def conv_silu(in_ref, tail, w_ref, bias_ref, out_scr):
        x = in_ref[...].astype(F32)
        ext = jnp.concatenate([tail[...], x], axis=0)
        acc = bias_ref[...] + w_ref[SSD_CONV - 1:SSD_CONV, :] * x
        for j in range(1, SSD_CONV):
            shifted = pltpu.roll(ext, j, axis=0)[SUBLANES:, :]
            acc = acc + w_ref[SSD_CONV - 1 - j:SSD_CONV - j, :] * shifted
        out_scr[...] = acc * _sigmoid(acc)
        tail[...] = x[rows - SUBLANES:, :]

    conv_silu(xs_ref, xbuf, cwx_ref, cbx_ref, xcs)
    conv_silu(b_ref, bbuf, cwb_ref, cbb_ref, bcs)
    conv_silu(c_ref, cbuf, cwc_ref, cbc_ref, ccs)

    li = lax.broadcasted_iota(jnp.int32, (chunk, chunk), 0)
    si = lax.broadcasted_iota(jnp.int32, (chunk, chunk), 1)
    causal = li >= si
    tri = jnp.where(causal, 1.0, 0.0).astype(BF16)
    first_half = lax.broadcasted_iota(jnp.int32, (1, LANES), 1) < SSD_HEADDIM
    a_row = -jnp.exp(alog_ref[...])
    heads_per_group = SSD_HEADS // SSD_GROUPS

    def chunk_body(ci, carry):
        r0 = pl.multiple_of(ci * chunk, chunk)
        dt = _softplus(dt_ref[pl.ds(r0, chunk), :] + dtb_ref[...])
        la = dt * a_row
        acum = _dot_exact_lhs(tri, la)
        acum_t = acum.T
        dt_t = dt.T
        last_t = acum_t[:, chunk - 1:chunk]
        w_t = jnp.exp(last_t - acum_t) * dt_t
        cdec_t = jnp.exp(last_t)

        cb = []
        bt = []
        cg = []
        for g in range(SSD_GROUPS):
            bg = bcs[pl.ds(r0, chunk), g * SSD_STATE:(g + 1) * SSD_STATE]
            cgv = ccs[pl.ds(r0, chunk), g * SSD_STATE:(g + 1) * SSD_STATE]
            btg = bg.T
            bt.append(btg)
            cg.append(cgv)
            cb.append(_dot(cgv.astype(BF16), btg.astype(BF16)))

        for pr in range(SSD_HEADS // 2):
            lo = pr * LANES
            xs_pair = xcs[pl.ds(r0, chunk), lo:lo + LANES]
            xs_bf = xs_pair.astype(BF16)
            prev = state[:, lo:lo + LANES]
            lhs_m, lhs_c, lhs_b, cd = [], [], [], []
            for h in (2 * pr, 2 * pr + 1):
                g = h // heads_per_group
                col = jnp.broadcast_to(acum[:, h:h + 1], (chunk, chunk))
                row = acum_t[h:h + 1, :]
                dec = jnp.exp(jnp.where(causal, col - row, NEG_BIG))
                lhs_m.append((cb[g] * dec * dt_t[h:h + 1, :]).astype(BF16))
                lhs_c.append((cg[g] * jnp.exp(col)).astype(BF16))
                lhs_b.append((bt[g] * w_t[h:h + 1, :]).astype(BF16))
                cd.append(cdec_t[h:h + 1, :])
            out = (_dot(jnp.concatenate(lhs_m, axis=0), xs_bf)
                   + _dot(jnp.concatenate(lhs_c, axis=0), prev.astype(BF16)))
            y_pair = jnp.where(first_half, out[0:chunk, :], out[chunk:2 * chunk, :])
            y_ref[pl.ds(r0, chunk), lo:lo + LANES] = (
                y_pair + xs_pair * dskip_ref[:, lo:lo + LANES]).astype(y_ref.dtype)
            st = _dot(jnp.concatenate(lhs_b, axis=0), xs_bf)
            cd_pair = jnp.where(first_half, cd[0], cd[1])
            state[:, lo:lo + LANES] = prev * cd_pair + jnp.where(
                first_half, st[0:SSD_STATE, :], st[SSD_STATE:2 * SSD_STATE, :])
        return carry

    lax.fori_loop(0, rows // chunk, chunk_body, 0)


def _ssd(u, dt_raw, cw, cb, dt_bias, a_log, d_skip, rows, chunk):
    t = u.shape[0]
    gn = SSD_GROUPS * SSD_STATE
    pad = LANES - SSD_HEADS
    assert chunk == SSD_STATE
    kern = functools.partial(_ssd_kernel, chunk=chunk, rows=rows)
    full = lambda shape: pl.BlockSpec(shape, lambda i: (0, 0))
    return pl.pallas_call(
        kern,
        grid=(t // rows,),
        in_specs=[
            pl.BlockSpec((rows, SSD_INNER), lambda i: (i, COL_XS // SSD_INNER)),
            pl.BlockSpec((rows, gn), lambda i: (i, COL_B // gn)),
            pl.BlockSpec((rows, gn), lambda i: (i, COL_C // gn)),
            pl.BlockSpec((rows, LANES), lambda i: (i, 0)),
            full((SSD_CONV, SSD_INNER)), full((SSD_CONV, gn)), full((SSD_CONV, gn)),
            full((1, SSD_INNER)), full((1, gn)), full((1, gn)),
            full((1, LANES)), full((1, LANES)), full((1, SSD_INNER)),
        ],
        out_specs=pl.BlockSpec((rows, SSD_INNER), lambda i: (i, 0)),
        out_shape=jax.ShapeDtypeStruct((t, SSD_INNER), BF16),
        scratch_shapes=[
            pltpu.VMEM((SUBLANES, SSD_INNER), F32),
            pltpu.VMEM((SUBLANES, gn), F32),
            pltpu.VMEM((SUBLANES, gn), F32),
            pltpu.VMEM((rows, SSD_INNER), F32),
            pltpu.VMEM((rows, gn), F32),
            pltpu.VMEM((rows, gn), F32),
            pltpu.VMEM((SSD_STATE, SSD_INNER), F32),
        ],
        compiler_params=pltpu.CompilerParams(
            dimension_semantics=("arbitrary",), vmem_limit_bytes=VMEM_LIMIT),
        name="ssd",
    )(u, u, u, dt_raw,
      cw[:, :SSD_INNER], cw[:, SSD_INNER:SSD_INNER + gn], cw[:, SSD_INNER + gn:],
      cb[None, :SSD_INNER], cb[None, SSD_INNER:SSD_INNER + gn], cb[None, SSD_INNER + gn:],
      jnp.pad(dt_bias, (0, pad))[None, :], jnp.pad(a_log, (0, pad))[None, :],
      jnp.repeat(d_skip, SSD_HEADDIM)[None, :])


def _post_kernel(y_ref, z_ref, scb_ref, scc_ref, scx_ref, gate_ref, x_ref,
                 nw_ref, wa_ref, csw_ref, wb_ref, bg_ref, wo_ref, g1_ref, lng_ref, lnb_ref,
                 sh2_ref, sc2_ref, wr_ref, br_ref,
                 x1_ref, h2_ref, prob_ref, idx_ref, lrank_ref, cst_ref, sidt_ref, runtab_ref, cnt_ref,
                 sbuf, run, *, rows, n_tok):
    @pl.when(pl.program_id(0) == 0)
    def _():
        sbuf[...] = jnp.zeros(sbuf.shape, F32)
        run[...] = jnp.zeros(run.shape, F32)

    z = z_ref[...].astype(F32)
    yg = y_ref[...].astype(F32) * (z * _sigmoid(z))
    ms = jnp.mean(yg * yg, axis=-1, keepdims=True)
    yn = yg * lax.rsqrt(ms + RMS_EPS) * nw_ref[...]
    u_a = _dot(yn.astype(BF16), wa_ref[...])

    cx = scc_ref[...].astype(F32) * scx_ref[...].astype(F32)
    ext = jnp.concatenate([sbuf[...], cx], axis=0)
    v = csw_ref[SC_CONV - 1:SC_CONV, :] * cx
    for j in range(1, SC_CONV):
        v = v + csw_ref[SC_CONV - 1 - j:SC_CONV - j, :] * pltpu.roll(ext, j, axis=0)[SUBLANES:, :]
    sbuf[...] = cx[rows - SUBLANES:, :]
    u_b = _dot((scb_ref[...].astype(F32) * v).astype(BF16), wb_ref[...])

    gl = gate_ref[...].astype(F32) + bg_ref[...]
    merged = _sigmoid(gl[:, :D_MODEL]) * u_a + _sigmoid(gl[:, D_MODEL:]) * u_b
    mix = _dot(merged.astype(BF16), wo_ref[...])
    x1 = _layer_norm(DN_ALPHA * x_ref[...] + g1_ref[...] * mix) * lng_ref[...] + lnb_ref[...]
    x1_ref[...] = x1
    h2 = _layer_norm(x1) * (1.0 + sc2_ref[...]) + sh2_ref[...]
    half = D_MODEL // 2
    h2_ref[...] = _pack_bf16_pair(h2[:, :half], h2[:, half:])

    logits = _dot_split(h2, wr_ref[...]) + br_ref[...]
    lane = lax.broadcasted_iota(jnp.int32, (rows, LANES), 1).astype(F32)
    work = logits
    onehots, vals = [], []
    idx_out = jnp.zeros((rows, LANES), F32)
    for k in range(TOP_K):
        m = jnp.max(work, axis=-1, keepdims=True)
        ik = jnp.min(jnp.where(work == m, lane, float(LANES)), axis=-1, keepdims=True)
        oh = lane == ik
        onehots.append(oh)
        vals.append(m)
        idx_out = jnp.where(lane == float(k), ik, idx_out)
        work = jnp.where(oh, -jnp.inf, work)
    es = [jnp.exp(vk - vals[0]) for vk in vals]
    denom = es[0] + es[1] + es[2] + es[3]
    prob_out = jnp.zeros((rows, LANES), F32)
    for k in range(TOP_K):
        prob_out = jnp.where(lane == float(k), es[k] / denom, prob_out)

    sel = jnp.zeros((rows, LANES), F32)
    kk = jnp.zeros((rows, LANES), F32)
    for k, oh in enumerate(onehots):
        sel = sel + jnp.where(oh, 1.0, 0.0)
        kk = kk + jnp.where(oh, float(k), 0.0)
    ri = lax.broadcasted_iota(jnp.int32, (rows, rows), 0)
    rj = lax.broadcasted_iota(jnp.int32, (rows, rows), 1)
    strict = jnp.where(ri > rj, 1.0, 0.0).astype(BF16)
    in_block = _dot(strict, sel.astype(BF16))
    base = in_block + run[...]
    lrank_out = jnp.zeros((rows, LANES), F32)
    for k in range(TOP_K):
        rk = jnp.sum(jnp.where(onehots[k], in_block, 0.0), axis=-1, keepdims=True)
        lrank_out = jnp.where(lane == float(k), rk, lrank_out)
    idx_ref[...] = idx_out.astype(jnp.int32)
    lrank_ref[...] = lrank_out.astype(jnp.int32)
    tok = (pl.program_id(0) * rows
           + lax.broadcasted_iota(jnp.int32, (rows, LANES), 0)).astype(F32)
    picked = sel > 0.0
    cs = jnp.where(picked, base + 1.0, 0.0)
    sid = jnp.where(picked, kk * float(n_tok) + tok, 0.0)
    cst_ref[...] = cs.T[:N_EXPERTS, :]
    sidt_ref[...] = sid.T[:N_EXPERTS, :]
    runtab_ref[0] = run[...].astype(jnp.int32)
    run[...] = run[...] + jnp.sum(sel, axis=0, keepdims=True)
    prob_ref[...] = prob_out
    cnt_ref[...] = run[...].astype(jnp.int32)


def _post(y, u, x, vecs, mats, rows):
    t = x.shape[0]
    kern = functools.partial(_post_kernel, rows=rows, n_tok=t)
    row = lambda w: pl.BlockSpec((1, w), lambda i: (0, 0))
    mat = lambda a, b: pl.BlockSpec((a, b), lambda i: (0, 0))
    ublk = lambda w, col: pl.BlockSpec((rows, w), lambda i: (i, col // w))
    nw, csw, bg, g1, lng, lnb, sh2, sc2, br = vecs
    wa, wb, wo, wr = mats
    tok = pl.BlockSpec((rows, LANES), lambda i: (i, 0))
    return pl.pallas_call(
        kern,
        grid=(t // rows,),
        in_specs=[
            pl.BlockSpec((rows, SSD_INNER), lambda i: (i, 0)),
            ublk(SSD_INNER, COL_Z), ublk(D_MODEL, COL_SCB), ublk(D_MODEL, COL_SCC),
            ublk(D_MODEL, COL_SCX), ublk(2 * D_MODEL, COL_GATE),
            pl.BlockSpec((rows, D_MODEL), lambda i: (i, 0)),
            row(SSD_INNER), mat(SSD_INNER, D_MODEL), mat(SC_CONV, D_MODEL), mat(D_MODEL, D_MODEL),
            row(2 * D_MODEL), mat(D_MODEL, D_MODEL), row(D_MODEL), row(D_MODEL), row(D_MODEL),
            row(D_MODEL), row(D_MODEL), mat(D_MODEL, LANES), row(LANES),
        ],
        out_specs=[
            pl.BlockSpec((rows, D_MODEL), lambda i: (i, 0)),
            pl.BlockSpec((rows, D_MODEL // 2), lambda i: (i, 0)),
            tok, tok, tok,
            pl.BlockSpec((N_EXPERTS, rows), lambda i: (0, i)),
            pl.BlockSpec((N_EXPERTS, rows), lambda i: (0, i)),
            pl.BlockSpec((1, 1, LANES), lambda i: (i, 0, 0)),
            pl.BlockSpec((1, LANES), lambda i: (0, 0)),
        ],
        out_shape=[
            jax.ShapeDtypeStruct((t, D_MODEL), F32),
            jax.ShapeDtypeStruct((t, D_MODEL // 2), jnp.uint32),
            jax.ShapeDtypeStruct((t, LANES), F32),
            jax.ShapeDtypeStruct((t, LANES), jnp.int32),
            jax.ShapeDtypeStruct((t, LANES), jnp.int32),
            jax.ShapeDtypeStruct((N_EXPERTS, t), F32),
            jax.ShapeDtypeStruct((N_EXPERTS, t), F32),
            jax.ShapeDtypeStruct((t // rows, 1, LANES), jnp.int32),
            jax.ShapeDtypeStruct((1, LANES), jnp.int32),
        ],
        scratch_shapes=[
            pltpu.VMEM((SUBLANES, D_MODEL), F32),
            pltpu.VMEM((1, LANES), F32),
        ],
        compiler_params=pltpu.CompilerParams(
            dimension_semantics=("arbitrary",), vmem_limit_bytes=VMEM_LIMIT),
        name="post_mix",
    )(y, u, u, u, u, u, x, nw, wa, csw, wb, bg, wo, g1, lng, lnb, sh2, sc2, wr, br)


def _invert_kernel(vt_ref, ve_ref, nv_ref, gs_ref, run_ref, cst_ref, sidt_ref, o_ref, ptr,
                   *, tile, tok_tile, n_tok_tiles):
    v = pl.program_id(0)
    i = vt_ref[v]
    e = ve_ref[v]
    vp = jnp.maximum(v - 1, 0)
    new_tile = jnp.logical_or(v == 0, vt_ref[vp] != i)
    new_exp = jnp.logical_or(v == 0, ve_ref[vp] != e)

    @pl.when(new_exp)
    def _():
        ptr[0] = 0

    @pl.when(new_tile)
    def _():
        o_ref[...] = jnp.zeros(o_ref.shape, jnp.int32)

    @pl.when(v < nv_ref[0])
    def _():
        g0 = gs_ref[e]
        row0 = i * tile
        ra = jnp.maximum(g0, row0) - g0
        rb = jnp.minimum(gs_ref[e + 1], row0 + tile) - g0
        b_lo = lax.while_loop(lambda b: run_ref[(b + 1) * N_EXPERTS + e] <= ra,
                              lambda b: b + 1, ptr[0])
        ptr[0] = b_lo
        b_hi = lax.while_loop(
            lambda b: jnp.logical_and(b < n_tok_tiles, run_ref[b * N_EXPERTS + e] < rb),
            lambda b: b + 1, b_lo)
        n_blk = tile // LANES
        row_in_blk = lax.broadcasted_iota(jnp.int32, (LANES, LANES), 0).astype(F32)
        firsts = [(row0 - g0 + 1 + k * LANES).astype(F32) for k in range(n_blk)]

        def body(b, accs):
            c0 = pl.multiple_of(b * tok_tile, tok_tile)
            cs_row = cst_ref[pl.ds(e, 1), pl.ds(c0, tok_tile)]
            sid_row = sidt_ref[pl.ds(e, 1), pl.ds(c0, tok_tile)]
            out = []
            for k in range(n_blk):
                rel = cs_row - firsts[k]
                acc = accs[k]
                for j in range(tok_tile // LANES):
                    lanes = slice(j * LANES, (j + 1) * LANES)
                    acc = acc + jnp.where(rel[:, lanes] == row_in_blk, sid_row[:, lanes], 0.0)
                out.append(acc)
            return tuple(out)

        accs = lax.fori_loop(b_lo, b_hi, body,
                             tuple(jnp.zeros((LANES, LANES), F32) for _ in range(n_blk)))
        for k in range(n_blk):
            contrib = jnp.sum(accs[k].T, axis=0, keepdims=True)
            lanes = slice(k * LANES, (k + 1) * LANES)
            o_ref[0, :, lanes] = o_ref[0, :, lanes] + contrib.astype(jnp.int32)


def _invert(v_tile, v_exp, n_visits, gs, runflat, cst, sidt, n_tiles, tile, tok_tile):
    t = cst.shape[1]
    n_vis = v_tile.shape[0]
    kern = functools.partial(_invert_kernel, tile=tile, tok_tile=tok_tile,
                             n_tok_tiles=t // tok_tile)
    return pl.pallas_call(
        kern,
        grid_spec=pltpu.PrefetchScalarGridSpec(
            num_scalar_prefetch=5,
            grid=(n_vis,),
            in_specs=[
                pl.BlockSpec((N_EXPERTS, t), lambda v, *_: (0, 0)),
                pl.BlockSpec((N_EXPERTS, t), lambda v, *_: (0, 0)),
            ],
            out_specs=pl.BlockSpec((1, 1, tile), lambda v, vt, *_: (vt[v], 0, 0)),
            scratch_shapes=[pltpu.SMEM((1,), jnp.int32)],
        ),
        out_shape=jax.ShapeDtypeStruct((n_tiles, 1, tile), jnp.int32),
        compiler_params=pltpu.CompilerParams(
            dimension_semantics=("arbitrary",), vmem_limit_bytes=VMEM_LIMIT),
        name="moe_invert",
    )(v_tile, v_exp, n_visits, gs, runflat, cst, sidt)


N_FF_BLK = 4
FF_BLK = EXPERT_FF // N_FF_BLK
N_STAGES = 2 * N_FF_BLK


def _experts_kernel(vt_ref, ve_ref, vn_ref, nv_ref, gs_ref,
                    ord_cur, ord_next, h_hbm, wgu_hbm, bgu_ref, wd_hbm, bd_ref,
                    o_ref, h_vmem, x_a, x_b, act, wgu_st, wd_st, wgu_bf, wd_bf,
                    hsem, wsem, *, tile, n_tok, layer):
    v = pl.program_id(0)
    nv = nv_ref[0]
    i = vt_ref[v]
    e = ve_ref[v]
    vp = jnp.maximum(v - 1, 0)
    active = v < nv
    new_tile = jnp.logical_or(v == 0, vt_ref[vp] != i)
    new_exp = jnp.logical_or(v == 0, ve_ref[vp] != e)
    even = lax.rem(i, 2) == 0
    half = D_MODEL // 2
    xs = (x_a, x_b)

    def weight_copies(expert):
        return (pltpu.make_async_copy(wgu_hbm.at[layer, expert], wgu_st, wsem.at[0]),
                pltpu.make_async_copy(wd_hbm.at[layer, expert], wd_st, wsem.at[1]))

    def fetch_row(order_ref, r, dst):
        sid = order_ref[r]
        tok = sid & (n_tok - 1) if n_tok & (n_tok - 1) == 0 else lax.rem(sid, n_tok)
        dst[pl.ds(r, 1), :] = h_vmem[pl.ds(tok, 1), :]

    @pl.when(jnp.logical_not(active))
    def _():
        o_ref[...] = jnp.zeros(o_ref.shape, jnp.uint32)

    @pl.when(v == 0)
    def _():
        tokens = pltpu.make_async_copy(h_hbm, h_vmem, hsem)
        tokens.start()
        for c in weight_copies(e):
            c.start()
        tokens.wait()
        for r in range(tile):
            fetch_row(ord_cur, r, x_a)

    @pl.when(jnp.logical_and(active, new_exp))
    def _():
        for c in weight_copies(e):
            c.wait()
        wgu_bf[...] = wgu_st[...].astype(BF16)
        wd_bf[...] = wd_st[...].astype(BF16)

        @pl.when(vn_ref[v] >= 0)
        def _():
            for c in weight_copies(vn_ref[v]):
                c.start()

    rowpos = i * tile + lax.broadcasted_iota(jnp.int32, (tile, 1), 0)
    mine = jnp.logical_and(rowpos >= gs_ref[e], rowpos < gs_ref[e + 1])

    def compute(x_ref, first, between):
        x_lo, x_hi = _unpack_bf16_pair(x_ref[...])
        xb = jnp.concatenate([x_lo.astype(BF16), x_hi.astype(BF16)], axis=1)
        for cb in range(N_FF_BLK):
            c0 = cb * FF_BLK
            g = _dot(xb, wgu_bf[:, c0:c0 + FF_BLK]) + bgu_ref[:, c0:c0 + FF_BLK]
            u = (_dot(xb, wgu_bf[:, EXPERT_FF + c0:EXPERT_FF + c0 + FF_BLK])
                 + bgu_ref[:, EXPERT_FF + c0:EXPERT_FF + c0 + FF_BLK])
            g = jnp.minimum(g, SWIGLU_LIMIT)
            u = jnp.clip(u, -SWIGLU_LIMIT, SWIGLU_LIMIT)
            act[:, c0:c0 + FF_BLK] = ((u + 1.0) * g * _sigmoid(SWIGLU_ALPHA * g)).astype(BF16)
            between(cb)
        for pb in range(N_FF_BLK // 2):
            c0 = pb * FF_BLK
            o_lo = _dot(act[...], wd_bf[:, c0:c0 + FF_BLK]) + bd_ref[:, c0:c0 + FF_BLK]
            between(N_FF_BLK + 2 * pb)
            o_hi = (_dot(act[...], wd_bf[:, half + c0:half + c0 + FF_BLK])
                    + bd_ref[:, half + c0:half + c0 + FF_BLK])
            packed = _pack_bf16_pair(o_lo, o_hi)
            keep = jnp.uint32(0) if first else o_ref[:, c0:c0 + FF_BLK]
            o_ref[:, c0:c0 + FF_BLK] = jnp.where(mine, packed, keep)
            between(N_FF_BLK + 2 * pb + 1)

    per_stage = tile // N_STAGES

    for p in range(2):
        x_cur, x_nxt = xs[p], xs[1 - p]
        on_parity = even if p == 0 else jnp.logical_not(even)

        @pl.when(jnp.logical_and(jnp.logical_and(active, new_tile), on_parity))
        def _(x_cur=x_cur, x_nxt=x_nxt):
            def between(j):
                for r in range(j * per_stage, (j + 1) * per_stage):
                    fetch_row(ord_next, r, x_nxt)

            compute(x_cur, True, between)

        @pl.when(jnp.logical_and(jnp.logical_and(active, jnp.logical_not(new_tile)), on_parity))
        def _(x_cur=x_cur):
            compute(x_cur, False, lambda j: None)


def _experts(v_tile, v_exp, v_next, n_visits, gs, order_ext, h2, w_gu, b_gu, w_down, b_down,
             layer, n_tiles, tile):
    n_vis = v_tile.shape[0]
    n_tok = h2.shape[0]
    ff2 = 2 * EXPERT_FF
    half = D_MODEL // 2
    kern = functools.partial(_experts_kernel, tile=tile, n_tok=n_tok, layer=layer)
    smem_blk = lambda off: pl.BlockSpec(
        (tile,), lambda v, vt, *_: (vt[v] + off,), memory_space=pltpu.SMEM)
    bias = lambda n: pl.BlockSpec(
        (None, None, 1, n), lambda v, vt, ve, *_: (layer, ve[v], 0, 0))
    hbm = pl.BlockSpec(memory_space=pl.ANY)
    return pl.pallas_call(
        kern,
        grid_spec=pltpu.PrefetchScalarGridSpec(
            num_scalar_prefetch=5,
            grid=(n_vis,),
            in_specs=[smem_blk(0), smem_blk(1), hbm, hbm, bias(ff2), hbm, bias(D_MODEL)],
            out_specs=pl.BlockSpec(
                (tile, half),
                lambda v, vt, ve, vn, nv, *_: (jnp.where(v < nv[0], vt[v], n_tiles), 0)),
            scratch_shapes=[
                pltpu.VMEM((n_tok, half), jnp.uint32),
                pltpu.VMEM((tile, half), jnp.uint32),
                pltpu.VMEM((tile, half), jnp.uint32),
                pltpu.VMEM((tile, EXPERT_FF), BF16),
                pltpu.VMEM((D_MODEL, ff2), F32),
                pltpu.VMEM((EXPERT_FF, D_MODEL), F32),
                pltpu.VMEM((D_MODEL, ff2), BF16),
                pltpu.VMEM((EXPERT_FF, D_MODEL), BF16),
                pltpu.SemaphoreType.DMA(()),
                pltpu.SemaphoreType.DMA((2,)),
            ],
        ),
        out_shape=jax.ShapeDtypeStruct(((n_tiles + 1) * tile, half), jnp.uint32),
        compiler_params=pltpu.CompilerParams(
            dimension_semantics=("arbitrary",), vmem_limit_bytes=EXPERTS_VMEM_LIMIT),
        name="moe_experts",
    )(v_tile, v_exp, v_next, n_visits, gs, order_ext, order_ext, h2,
      w_gu, b_gu[:, :, None, :], w_down, b_down[:, :, None, :])


SEG_CHUNK = 64


def _segment_layout(gs, run_lo, run_hi):
    n = run_hi - run_lo
    lead = (gs + run_lo) & (SUBLANES - 1)
    span = jnp.where(n > 0, n + lead, 0)
    log_chunk = SEG_CHUNK.bit_length() - 1
    padded = lax.shift_left(lax.shift_right_logical(span + (SEG_CHUNK - 1), log_chunk), log_chunk)
    return lead, span, padded


def _combine_kernel(gs_ref, run_ref, row_ref, os_hbm, prob_ref, x1_ref, g2_ref, lng_ref,
                    lnb_ref, x2_ref, seg, g_0, g_1, g_2, g_3, n_started, sem, *, rows, n_blocks):
    b = pl.program_id(0)
    slot = lax.rem(b, 2)
    max_chunks = rows // SEG_CHUNK + 1

    def chunk_copy(src, dst, s):
        return pltpu.make_async_copy(os_hbm.at[pl.ds(src, SEG_CHUNK)],
                                     seg.at[s, pl.ds(dst, SEG_CHUNK)], sem.at[s])

    def start_segments(blk, s):
        off = jnp.int32(0)
        started = jnp.int32(0)
        for e in range(N_EXPERTS):
            first = run_ref[blk * N_EXPERTS + e]
            lead, span, padded = _segment_layout(gs_ref[e], first,
                                                 run_ref[(blk + 1) * N_EXPERTS + e])
            src0 = gs_ref[e] + first - lead
            for c in range(max_chunks):
                @pl.when(c * SEG_CHUNK < span)
                def _(off=off, src0=src0, c=c):
                    chunk_copy(pl.multiple_of(src0 + c * SEG_CHUNK, SUBLANES),
                               pl.multiple_of(off + c * SEG_CHUNK, SEG_CHUNK), s).start()
            off = off + padded
            started = started + lax.shift_right_logical(padded, SEG_CHUNK.bit_length() - 1)
        n_started[s] = started

    @pl.when(b == 0)
    def _():
        start_segments(0, 0)

    @pl.when(b + 1 < n_blocks)
    def _():
        start_segments(b + 1, 1 - slot)

    def wait_one(_, carry):
        chunk_copy(0, 0, slot).wait()
        return carry

    lax.fori_loop(0, n_started[slot], wait_one, 0)

    bufs = (g_0, g_1, g_2, g_3)
    for t in range(rows):
        for k in range(TOP_K):
            bufs[k][pl.ds(t, 1), :] = seg[slot, pl.ds(row_ref[t * TOP_K + k], 1), :]

    prob = prob_ref[...]
    ffn_lo = jnp.zeros((rows, D_MODEL // 2), F32)
    ffn_hi = jnp.zeros((rows, D_MODEL // 2), F32)
    for k in range(TOP_K):
        lo, hi = _unpack_bf16_pair(bufs[k][...])
        ffn_lo = ffn_lo + prob[:, k:k + 1] * lo
        ffn_hi = ffn_hi + prob[:, k:k + 1] * hi
    ffn = jnp.concatenate([ffn_lo, ffn_hi], axis=1)
    x2_ref[...] = (_layer_norm(DN_ALPHA * x1_ref[...] + g2_ref[...] * ffn) * lng_ref[...]
                   + lnb_ref[...])


def _combine(gs, runflat, idx, lrank, out_sorted, probs, x1, g2, lng, lnb, rows):
    t = x1.shape[0]
    half = D_MODEL // 2
    n_blocks = t // rows
    run = runflat.reshape(n_blocks + 1, N_EXPERTS)
    lead, _, padded = _segment_layout(gs[None, :N_EXPERTS], run[:-1], run[1:])
    seg_start = jnp.cumsum(padded, axis=1) - padded + lead
    onehot = idx[:, :TOP_K, None] == jnp.arange(N_EXPERTS, dtype=jnp.int32)[None, None, :]
    start_tk = jnp.sum(jnp.where(onehot, jnp.repeat(seg_start, rows, axis=0)[:, None, :], 0),
                       axis=-1)
    row_flat = (start_tk + lrank[:, :TOP_K]).reshape(-1).astype(jnp.int32)
    seg_rows = rows * TOP_K + N_EXPERTS * (SEG_CHUNK + SUBLANES)
    kern = functools.partial(_combine_kernel, rows=rows, n_blocks=n_blocks)
    row = pl.BlockSpec((1, D_MODEL), lambda i, *_: (0, 0))
    smem = pl.BlockSpec((rows * TOP_K,), lambda i, *_: (i,), memory_space=pltpu.SMEM)
    return pl.pallas_call(
        kern,
        grid_spec=pltpu.PrefetchScalarGridSpec(
            num_scalar_prefetch=2,
            grid=(n_blocks,),
            in_specs=[
                smem,
                pl.BlockSpec(memory_space=pl.ANY),
                pl.BlockSpec((rows, LANES), lambda i, *_: (i, 0)),
                pl.BlockSpec((rows, D_MODEL), lambda i, *_: (i, 0)),
                row, row, row,
            ],
            out_specs=pl.BlockSpec((rows, D_MODEL), lambda i, *_: (i, 0)),
            scratch_shapes=[
                pltpu.VMEM((2, seg_rows, half), jnp.uint32),
                pltpu.VMEM((rows, half), jnp.uint32),
                pltpu.VMEM((rows, half), jnp.uint32),
                pltpu.VMEM((rows, half), jnp.uint32),
                pltpu.VMEM((rows, half), jnp.uint32),
                pltpu.SMEM((2,), jnp.int32),
                pltpu.SemaphoreType.DMA((2,)),
            ],
        ),
        out_shape=jax.ShapeDtypeStruct((t, D_MODEL), F32),
        compiler_params=pltpu.CompilerParams(
            dimension_semantics=("arbitrary",), vmem_limit_bytes=VMEM_LIMIT),
        name="moe_combine",
    )(gs, runflat, row_flat, out_sorted, probs, x1, g2, lng, lnb)


def _tile(t, pref):
    return pref if t % pref == 0 else t


def _visit_plan(counts, tile, n_vis):
    ge = jnp.cumsum(counts)
    gs = ge - counts
    t_lo = gs // tile
    nt = jnp.where(counts > 0, (ge - 1) // tile - t_lo + 1, 0)
    vend = jnp.cumsum(nt)
    vbase = vend - nt
    n_visits = vend[-1:]
    vc = jnp.minimum(jnp.arange(n_vis, dtype=jnp.int32), n_visits[0] - 1)
    v_exp = jnp.sum((vend[None, :] <= vc[:, None]).astype(jnp.int32), axis=1)
    onehot = (v_exp[:, None] == jnp.arange(N_EXPERTS, dtype=jnp.int32)[None, :]).astype(jnp.int32)
    v_tile = jnp.sum(onehot * (t_lo - vbase)[None, :], axis=1) + vc
    gs33 = jnp.concatenate([gs, ge[-1:]])
    after = jnp.sum(onehot * vend[None, :], axis=1)
    e_after = jnp.sum((vend[None, :] <= after[:, None]).astype(jnp.int32), axis=1)
    v_next = jnp.where(after < n_visits[0], e_after, -1)
    return (v_tile.astype(jnp.int32), v_exp.astype(jnp.int32), v_next.astype(jnp.int32),
            n_visits.astype(jnp.int32), gs33.astype(jnp.int32))


def kernel(x, c, w_ada, b_ada, w_in, conv_ssd_w, conv_ssd_b, dt_bias, a_log, d_skip, ssd_norm_w,
           w_ssd_out, conv_short_w, w_short_out, b_gate, w_o, ln1_g, ln1_b, w_router, b_router,
           w_gu, b_gu, w_down, b_down, ln2_g, ln2_b):
    batch, seq, d = x.shape
    assert batch == 1 and d == D_MODEL
    depth = w_in.shape[0]
    t = seq
    xt = x.reshape(t, d)

    tm_in = _tile(t, 1024)
    rows_ssd = _tile(t, 512)
    chunk = 128
    rows_post = 256
    tile_e = 256
    assert t % rows_post == 0
    n_tiles = t * TOP_K // tile_e
    n_vis = n_tiles + N_EXPERTS

    mods = _ada_mod(c, w_ada, b_ada)

    c_bc = 2 * SSD_INNER
    c_dt = c_bc + 2 * SSD_GROUPS * SSD_STATE
    c_scb = c_dt + SSD_HEADS
    c_scc = c_scb + D_MODEL
    w_t = jnp.swapaxes(w_in, 1, 2)
    w_main = jnp.concatenate(
        [w_t[:, :c_bc], w_t[:, c_scc:], w_t[:, c_bc:c_dt], w_t[:, c_scb:c_scc]],
        axis=1).astype(BF16)
    w_dt = jnp.pad(w_t[:, c_dt:c_scb], ((0, 0), (0, LANES - SSD_HEADS), (0, 0)))
    w_r = jnp.pad(w_router, ((0, 0), (0, 0), (0, LANES - N_EXPERTS)))
    b_r = jnp.pad(b_router, ((0, 0), (0, LANES - N_EXPERTS)), constant_values=NEG_BIG)
    wa_bf = w_ssd_out.astype(BF16)
    wb_bf = w_short_out.astype(BF16)
    wo_bf = w_o.astype(BF16)
    cool = jnp.zeros((tile_e,), jnp.int32)

    for l in range(depth):
        m = mods[l]
        sh1, sc1, g1, sh2, sc2, g2 = [m[:, k * d:(k + 1) * d] for k in range(N_ADA)]
        u, dt_raw = _in_proj(xt, sh1, sc1, w_main[l], w_dt[l], tm_in)
        y = _ssd(u, dt_raw, conv_ssd_w[l], conv_ssd_b[l], dt_bias[l], a_log[l], d_skip[l],
                 rows_ssd, chunk)
        vecs = (ssd_norm_w[l][None, :], conv_short_w[l], b_gate[l][None, :], g1,
                ln1_g[l][None, :], ln1_b[l][None, :], sh2, sc2, b_r[l][None, :])
        x1, h2, probs, idx, lrank, cst, sidt, runtab, cnt = _post(
            y, u, xt, vecs, (wa_bf[l], wb_bf[l], wo_bf[l], w_r[l]), rows_post)

        v_tile, v_exp, v_next, n_visits, gs = _visit_plan(cnt[0, :N_EXPERTS], tile_e, n_vis)
        runflat = jnp.concatenate([runtab[:, 0, :N_EXPERTS], cnt[:, :N_EXPERTS]]).reshape(-1)
        order = _invert(v_tile, v_exp, n_visits, gs, runflat, cst, sidt, n_tiles, tile_e,
                        rows_post)
        order_ext = jnp.concatenate([order.reshape(-1), cool])
        out_sorted = _experts(v_tile, v_exp, v_next, n_visits, gs, order_ext, h2, w_gu, b_gu,
                              w_down, b_down, l, n_tiles, tile_e)
        xt = _combine(gs, runflat, idx, lrank, out_sorted, probs, x1, g2, ln2_g[l][None, :],
                      ln2_b[l][None, :], rows_post)

    return xt.reshape(batch, seq, d)
```

```python
import functools

import jax
import jax.numpy as jnp
from jax import lax
from jax.experimental import pallas as pl
from jax.experimental.pallas import tpu as pltpu

F32 = jnp.float32
BF16 = jnp.bfloat16

D_MODEL = 1024
SSD_INNER = 2048
SSD_HEADS = 32
SSD_HEADDIM = 64
SSD_GROUPS = 4
SSD_STATE = 128
SSD_CONV = 4
SC_CONV = 3
N_EXPERTS = 32
TOP_K = 4
EXPERT_FF = 1024
SWIGLU_LIMIT = 7.0
SWIGLU_ALPHA = 1.702
DEPTH = 4
DN_ALPHA = (2.0 * DEPTH) ** 0.25
LN_EPS = 1e-5
RMS_EPS = 1e-5
N_ADA = 6

LANES = 128
SUBLANES = 8
U_MAIN = 10240
U_BLK = 2048
_C_BC = 2 * SSD_INNER
_C_SCB = _C_BC + 2 * SSD_GROUPS * SSD_STATE + SSD_HEADS
_C_SCC = _C_SCB + D_MODEL
U_SRC = ((0,), (SSD_INNER,), (_C_SCC,), (_C_SCC + U_BLK,), (_C_BC, _C_SCB))
COL_Z = 0
COL_XS = 2048
COL_SCC = 4096
COL_SCX = 5120
COL_GATE = 6144
COL_B = 8192
COL_C = 8704
COL_SCB = 9216
NEG_BIG = -1e30

VMEM_LIMIT = 56 * 1024 * 1024
EXPERTS_VMEM_LIMIT = 60 * 1024 * 1024


def _sigmoid(v):
    return 1.0 / (1.0 + jnp.exp(-v))


def _softplus(v):
    return jnp.maximum(v, 0.0) + jnp.log(1.0 + jnp.exp(-jnp.abs(v)))


def _layer_norm(v):
    mu = jnp.mean(v, axis=-1, keepdims=True)
    vc = v - mu
    var = jnp.mean(vc * vc, axis=-1, keepdims=True)
    return vc * lax.rsqrt(var + LN_EPS)


def _split3(v):
    hi = v.astype(BF16)
    r1 = v - hi.astype(F32)
    mid = r1.astype(BF16)
    lo = (r1 - mid.astype(F32)).astype(BF16)
    return hi, mid, lo


def _dot(a, b):
    return jnp.dot(a, b, preferred_element_type=F32)


def _pack_bf16_pair(lo, hi):
    lo_bits = pltpu.bitcast(lo.astype(BF16).astype(F32), jnp.uint32)
    hi_bits = pltpu.bitcast(hi.astype(BF16).astype(F32), jnp.uint32)
    return lax.shift_right_logical(lo_bits, jnp.uint32(16)) | (hi_bits & jnp.uint32(0xFFFF0000))


def _unpack_bf16_pair(w):
    lo = pltpu.bitcast(lax.shift_left(w, jnp.uint32(16)), F32)
    hi = pltpu.bitcast(w & jnp.uint32(0xFFFF0000), F32)
    return lo, hi


def _dot_split(a, b):
    a_hi = a.astype(BF16)
    a_mid = (a - a_hi.astype(F32)).astype(BF16)
    b_hi = b.astype(BF16)
    b_mid = (b - b_hi.astype(F32)).astype(BF16)
    return _dot(a_hi, b_hi) + _dot(a_hi, b_mid) + _dot(a_mid, b_hi)


def _dot_exact_lhs(a_bf16, v):
    hi, mid, lo = _split3(v)
    return _dot(a_bf16, hi) + _dot(a_bf16, mid) + _dot(a_bf16, lo)


def _ada_kernel(c_ref, w_ref, b_ref, o_ref):
    c = c_ref[...]
    s = c * _sigmoid(c)
    o_ref[0] = jnp.sum(w_ref[0] * s, axis=0, keepdims=True) + b_ref[0]


def _ada_mod(c, w_ada, b_ada):
    depth, d, n = w_ada.shape
    tn = 1024
    return pl.pallas_call(
        _ada_kernel,
        grid=(depth, n // tn),
        in_specs=[
            pl.BlockSpec((d, 1), lambda l, j: (0, 0)),
            pl.BlockSpec((1, d, tn), lambda l, j: (l, 0, j)),
            pl.BlockSpec((1, 1, tn), lambda l, j: (l, 0, j)),
        ],
        out_specs=pl.BlockSpec((1, 1, tn), lambda l, j: (l, 0, j)),
        out_shape=jax.ShapeDtypeStruct((depth, 1, n), F32),
        compiler_params=pltpu.CompilerParams(
            dimension_semantics=("arbitrary", "arbitrary"), vmem_limit_bytes=VMEM_LIMIT),
        name="ada_mod",
    )(c.reshape(d, 1), w_ada, b_ada.reshape(depth, 1, n))


def _dot_nt(a, b):
    return lax.dot_general(a, b, (((1,), (1,)), ((), ())), preferred_element_type=F32)


def _inproj_kernel(x_ref, sh_ref, sc_ref, w_hbm, wdt_ref, u_ref, dt_ref, h_scr, wbuf, wsem,
                   *, layer, n_steps):
    i = pl.program_id(0)
    j = pl.program_id(1)
    n_blk = len(U_SRC)
    step = i * n_blk + j
    slot = lax.rem(step, 2)

    def block_copies(blk, s):
        pieces = U_SRC[blk]
        rows_per = U_BLK // len(pieces)
        return [pltpu.make_async_copy(w_hbm.at[layer, pl.ds(src, rows_per)],
                                      wbuf.at[s, pl.ds(p * rows_per, rows_per)], wsem.at[s])
                for p, src in enumerate(pieces)]

    @pl.when(step == 0)
    def _():
        for c in block_copies(0, 0):
            c.start()

    for blk in range(n_blk):
        @pl.when(jnp.logical_and(j == blk, step + 1 < n_steps))
        def _(blk=blk):
            for c in block_copies((blk + 1) % n_blk, 1 - slot):
                c.start()

    @pl.when(j == 0)
    def _():
        h = _layer_norm(x_ref[...]) * (1.0 + sc_ref[...]) + sh_ref[...]
        h_scr[...] = h.astype(BF16)
        w = wdt_ref[...]
        h_hi = h.astype(BF16)
        h_mid = (h - h_hi.astype(F32)).astype(BF16)
        w_hi = w.astype(BF16)
        w_mid = (w - w_hi.astype(F32)).astype(BF16)
        dt_ref[...] = _dot_nt(h_hi, w_hi) + _dot_nt(h_hi, w_mid) + _dot_nt(h_mid, w_hi)

    for blk in range(n_blk):
        @pl.when(j == blk)
        def _(blk=blk):
            for c in block_copies(blk, slot):
                c.wait()

    u_ref[...] = _dot_nt(h_scr[...], wbuf[slot]).astype(BF16)


def _in_proj(x, sh, sc, w_t, w_dt, layer, tm):
    t, d = x.shape
    tn = U_BLK
    assert w_t.shape[2] == d and w_dt.shape == (LANES, d) and len(U_SRC) * tn == U_MAIN
    kern = functools.partial(_inproj_kernel, layer=layer, n_steps=(t // tm) * len(U_SRC))
    return pl.pallas_call(
        kern,
        grid=(t // tm, U_MAIN // tn),
        in_specs=[
            pl.BlockSpec((tm, d), lambda i, j: (i, 0)),
            pl.BlockSpec((1, d), lambda i, j: (0, 0)),
            pl.BlockSpec((1, d), lambda i, j: (0, 0)),
            pl.BlockSpec(memory_space=pl.ANY),
            pl.BlockSpec((LANES, d), lambda i, j: (0, 0)),
        ],
        out_specs=[
            pl.BlockSpec((tm, tn), lambda i, j: (i, j)),
            pl.BlockSpec((tm, LANES), lambda i, j: (i, 0)),
        ],
        out_shape=[
            jax.ShapeDtypeStruct((t, U_MAIN), BF16),
            jax.ShapeDtypeStruct((t, LANES), F32),
        ],
        scratch_shapes=[
            pltpu.VMEM((tm, d), BF16),
            pltpu.VMEM((2, tn, d), BF16),
            pltpu.SemaphoreType.DMA((2,)),
        ],
        compiler_params=pltpu.CompilerParams(
            dimension_semantics=("arbitrary", "arbitrary"), vmem_limit_bytes=VMEM_LIMIT),
        name="in_proj",
    )(x, sh, sc, w_t, w_dt)


def _ssd_kernel(xs_ref, b_ref, c_ref, dt_ref, cwx_ref, cwb_ref, cwc_ref, cbx_ref, cbb_ref,
                cbc_ref, dtb_ref, alog_ref, dskip_ref, y_ref,
                xbuf, bbuf, cbuf, xcs, bcs, ccs, state, *, chunk, rows):
    @pl.when(pl.program_id(0) == 0)
    def _():
        xbuf[...] = jnp.zeros(xbuf.shape, F32)
        bbuf[...] = jnp.zeros(bbuf.shape, F32)
        cbuf[...] = jnp.zeros(cbuf.shape, F32)
        state[...] = jnp.zeros(state.shape, F32)

    def conv_silu(in_ref, tail, w_ref, bias_ref, out_scr):
        x = in_ref[...].astype(F32)
        ext = jnp.concatenate([tail[...], x], axis=0)
        acc = bias_ref[...] + w_ref[SSD_CONV - 1:SSD_CONV, :] * x
        for j in range(1, SSD_CONV):
            shifted = pltpu.roll(ext, j, axis=0)[SUBLANES:, :]
            acc = acc + w_ref[SSD_CONV - 1 - j:SSD_CONV - j, :] * shifted
        out_scr[...] = acc * _sigmoid(acc)
        tail[...] = x[rows - SUBLANES:, :]

    conv_silu(xs_ref, xbuf, cwx_ref, cbx_ref, xcs)
    conv_silu(b_ref, bbuf, cwb_ref, cbb_ref, bcs)
    conv_silu(c_ref, cbuf, cwc_ref, cbc_ref, ccs)

    li = lax.broadcasted_iota(jnp.int32, (chunk, chunk), 0)
    si = lax.broadcasted_iota(jnp.int32, (chunk, chunk), 1)
    causal = li >= si
    tri = jnp.where(causal, 1.0, 0.0).astype(BF16)
    first_half = lax.broadcasted_iota(jnp.int32, (1, LANES), 1) < SSD_HEADDIM
    a_row = -jnp.exp(alog_ref[...])
    heads_per_group = SSD_HEADS // SSD_GROUPS

    def chunk_body(ci, carry):
        r0 = pl.multiple_of(ci * chunk, chunk)
        dt = _softplus(dt_ref[pl.ds(r0, chunk), :] + dtb_ref[...])
        la = dt * a_row
        acum = _dot_exact_lhs(tri, la)
        acum_t = acum.T
        dt_t = dt.T
        last_t = acum_t[:, chunk - 1:chunk]
        w_t = jnp.exp(last_t - acum_t) * dt_t
        cdec_t = jnp.exp(last_t)

        cb = []
        bt = []
        cg = []
        for g in range(SSD_GROUPS):
            bg = bcs[pl.ds(r0, chunk), g * SSD_STATE:(g + 1) * SSD_STATE]
            cgv = ccs[pl.ds(r0, chunk), g * SSD_STATE:(g + 1) * SSD_STATE]
            btg = bg.T
            bt.append(btg)
            cg.append(cgv)
            cb.append(_dot(cgv.astype(BF16), btg.astype(BF16)))

        for pr in range(SSD_HEADS // 2):
            lo = pr * LANES
            xs_pair = xcs[pl.ds(r0, chunk), lo:lo + LANES]
            xs_bf = xs_pair.astype(BF16)
            prev = state[:, lo:lo + LANES]
            lhs_m, lhs_c, lhs_b, cd = [], [], [], []
            for h in (2 * pr, 2 * pr + 1):
                g = h // heads_per_group
                col = jnp.broadcast_to(acum[:, h:h + 1], (chunk, chunk))
                row = acum_t[h:h + 1, :]
                dec = jnp.exp(jnp.where(causal, col - row, NEG_BIG))
                lhs_m.append((cb[g] * dec * dt_t[h:h + 1, :]).astype(BF16))
                lhs_c.append((cg[g] * jnp.exp(col)).astype(BF16))
                lhs_b.append((bt[g] * w_t[h:h + 1, :]).astype(BF16))
                cd.append(cdec_t[h:h + 1, :])
            out = (_dot(jnp.concatenate(lhs_m, axis=0), xs_bf)
                   + _dot(jnp.concatenate(lhs_c, axis=0), prev.astype(BF16)))
            y_pair = jnp.where(first_half, out[0:chunk, :], out[chunk:2 * chunk, :])
            y_ref[pl.ds(r0, chunk), lo:lo + LANES] = (
                y_pair + xs_pair * dskip_ref[:, lo:lo + LANES]).astype(y_ref.dtype)
            st = _dot(jnp.concatenate(lhs_b, axis=0), xs_bf)
            cd_pair = jnp.where(first_half, cd[0], cd[1])
            state[:, lo:lo + LANES] = prev * cd_pair + jnp.where(
                first_half, st[0:SSD_STATE, :], st[SSD_STATE:2 * SSD_STATE, :])
        return carry

    lax.fori_loop(0, rows // chunk, chunk_body, 0)


def _ssd(u, dt_raw, cw, cb, dt_bias, a_log, d_skip, rows, chunk):
    t = u.shape[0]
    gn = SSD_GROUPS * SSD_STATE
    pad = LANES - SSD_HEADS
    assert chunk == SSD_STATE
    kern = functools.partial(_ssd_kernel, chunk=chunk, rows=rows)
    full = lambda shape: pl.BlockSpec(shape, lambda i: (0, 0))
    return pl.pallas_call(
        kern,
        grid=(t // rows,),
        in_specs=[
            pl.BlockSpec((rows, SSD_INNER), lambda i: (i, COL_XS // SSD_INNER)),
            pl.BlockSpec((rows, gn), lambda i: (i, COL_B // gn)),
            pl.BlockSpec((rows, gn), lambda i: (i, COL_C // gn)),
            pl.BlockSpec((rows, LANES), lambda i: (i, 0)),
            full((SSD_CONV, SSD_INNER)), full((SSD_CONV, gn)), full((SSD_CONV, gn)),
            full((1, SSD_INNER)), full((1, gn)), full((1, gn)),
            full((1, LANES)), full((1, LANES)), full((1, SSD_INNER)),
        ],
        out_specs=pl.BlockSpec((rows, SSD_INNER), lambda i: (i, 0)),
        out_shape=jax.ShapeDtypeStruct((t, SSD_INNER), BF16),
        scratch_shapes=[
            pltpu.VMEM((SUBLANES, SSD_INNER), F32),
            pltpu.VMEM((SUBLANES, gn), F32),
            pltpu.VMEM((SUBLANES, gn), F32),
            pltpu.VMEM((rows, SSD_INNER), F32),
            pltpu.VMEM((rows, gn), F32),
            pltpu.VMEM((rows, gn), F32),
            pltpu.VMEM((SSD_STATE, SSD_INNER), F32),
        ],
        compiler_params=pltpu.CompilerParams(
            dimension_semantics=("arbitrary",), vmem_limit_bytes=VMEM_LIMIT),
        name="ssd",
    )(u, u, u, dt_raw,
      cw[:, :SSD_INNER], cw[:, SSD_INNER:SSD_INNER + gn], cw[:, SSD_INNER + gn:],
      cb[None, :SSD_INNER], cb[None, SSD_INNER:SSD_INNER + gn], cb[None, SSD_INNER + gn:],
      jnp.pad(dt_bias, (0, pad))[None, :], jnp.pad(a_log, (0, pad))[None, :],
      jnp.repeat(d_skip, SSD_HEADDIM)[None, :])


def _post_kernel(y_ref, z_ref, scb_ref, scc_ref, scx_ref, gate_ref, x_ref,
                 nw_ref, wa_ref, csw_ref, wb_ref, bg_ref, wo_ref, g1_ref, lng_ref, lnb_ref,
                 sh2_ref, sc2_ref, wr_ref, br_ref,
                 x1_ref, h2_ref, prob_ref, idx_ref, lrank_ref, cst_ref, sidt_ref, runtab_ref, cnt_ref,
                 sbuf, run, *, rows, n_tok):
    @pl.when(pl.program_id(0) == 0)
    def _():
        sbuf[...] = jnp.zeros(sbuf.shape, F32)
        run[...] = jnp.zeros(run.shape, F32)

    z = z_ref[...].astype(F32)
    yg = y_ref[...].astype(F32) * (z * _sigmoid(z))
    ms = jnp.mean(yg * yg, axis=-1, keepdims=True)
    yn = yg * lax.rsqrt(ms + RMS_EPS) * nw_ref[...]
    u_a = _dot(yn.astype(BF16), wa_ref[...])

    cx = scc_ref[...].astype(F32) * scx_ref[...].astype(F32)
    ext = jnp.concatenate([sbuf[...], cx], axis=0)
    v = csw_ref[SC_CONV - 1:SC_CONV, :] * cx
    for j in range(1, SC_CONV):
        v = v + csw_ref[SC_CONV - 1 - j:SC_CONV - j, :] * pltpu.roll(ext, j, axis=0)[SUBLANES:, :]
    sbuf[...] = cx[rows - SUBLANES:, :]
    u_b = _dot((scb_ref[...].astype(F32) * v).astype(BF16), wb_ref[...])

    gl = gate_ref[...].astype(F32) + bg_ref[...]
    merged = _sigmoid(gl[:, :D_MODEL]) * u_a + _sigmoid(gl[:, D_MODEL:]) * u_b
    mix = _dot(merged.astype(BF16), wo_ref[...])
    x1 = _layer_norm(DN_ALPHA * x_ref[...] + g1_ref[...] * mix) * lng_ref[...] + lnb_ref[...]
    x1_ref[...] = x1
    h2 = _layer_norm(x1) * (1.0 + sc2_ref[...]) + sh2_ref[...]
    half = D_MODEL // 2
    h2_ref[...] = _pack_bf16_pair(h2[:, :half], h2[:, half:])

    logits = _dot_split(h2, wr_ref[...]) + br_ref[...]
    lane = lax.broadcasted_iota(jnp.int32, (rows, LANES), 1).astype(F32)
    work = logits
    onehots, vals = [], []
    idx_out = jnp.zeros((rows, LANES), F32)
    for k in range(TOP_K):
        m = jnp.max(work, axis=-1, keepdims=True)
        ik = jnp.min(jnp.where(work == m, lane, float(LANES)), axis=-1, keepdims=True)
        oh = lane == ik
        onehots.append(oh)
        vals.append(m)
        idx_out = jnp.where(lane == float(k), ik, idx_out)
        work = jnp.where(oh, -jnp.inf, work)
    es = [jnp.exp(vk - vals[0]) for vk in vals]
    denom = es[0] + es[1] + es[2] + es[3]
    prob_out = jnp.zeros((rows, LANES), F32)
    for k in range(TOP_K):
        prob_out = jnp.where(lane == float(k), es[k] / denom, prob_out)

    sel = jnp.zeros((rows, LANES), F32)
    kk = jnp.zeros((rows, LANES), F32)
    for k, oh in enumerate(onehots):
        sel = sel + jnp.where(oh, 1.0, 0.0)
        kk = kk + jnp.where(oh, float(k), 0.0)
    ri = lax.broadcasted_iota(jnp.int32, (rows, rows), 0)
    rj = lax.broadcasted_iota(jnp.int32, (rows, rows), 1)
    strict = jnp.where(ri > rj, 1.0, 0.0).astype(BF16)
    in_block = _dot(strict, sel.astype(BF16))
    base = in_block + run[...]
    lrank_out = jnp.zeros((rows, LANES), F32)
    for k in range(TOP_K):
        rk = jnp.sum(jnp.where(onehots[k], in_block, 0.0), axis=-1, keepdims=True)
        lrank_out = jnp.where(lane == float(k), rk, lrank_out)
    idx_ref[...] = idx_out.astype(jnp.int32)
    lrank_ref[...] = lrank_out.astype(jnp.int32)
    tok = (pl.program_id(0) * rows
           + lax.broadcasted_iota(jnp.int32, (rows, LANES), 0)).astype(F32)
    picked = sel > 0.0
    cs = jnp.where(picked, base + 1.0, 0.0)
    sid = jnp.where(picked, kk * float(n_tok) + tok, 0.0)
    cst_ref[...] = cs.T[:N_EXPERTS, :]
    sidt_ref[...] = sid.T[:N_EXPERTS, :]
    runtab_ref[0] = run[...].astype(jnp.int32)
    run[...] = run[...] + jnp.sum(sel, axis=0, keepdims=True)
    prob_ref[...] = prob_out
    cnt_ref[...] = run[...].astype(jnp.int32)


def _post(y, u, x, vecs, mats, rows):
    t = x.shape[0]
    kern = functools.partial(_post_kernel, rows=rows, n_tok=t)
    row = lambda w: pl.BlockSpec((1, w), lambda i: (0, 0))
    mat = lambda a, b: pl.BlockSpec((a, b), lambda i: (0, 0))
    ublk = lambda w, col: pl.BlockSpec((rows, w), lambda i: (i, col // w))
    nw, csw, bg, g1, lng, lnb, sh2, sc2, br = vecs
    wa, wb, wo, wr = mats
    tok = pl.BlockSpec((rows, LANES), lambda i: (i, 0))
    return pl.pallas_call(
        kern,
        grid=(t // rows,),
        in_specs=[
            pl.BlockSpec((rows, SSD_INNER), lambda i: (i, 0)),
            ublk(SSD_INNER, COL_Z), ublk(D_MODEL, COL_SCB), ublk(D_MODEL, COL_SCC),
            ublk(D_MODEL, COL_SCX), ublk(2 * D_MODEL, COL_GATE),
            pl.BlockSpec((rows, D_MODEL), lambda i: (i, 0)),
            row(SSD_INNER), mat(SSD_INNER, D_MODEL), mat(SC_CONV, D_MODEL), mat(D_MODEL, D_MODEL),
            row(2 * D_MODEL), mat(D_MODEL, D_MODEL), row(D_MODEL), row(D_MODEL), row(D_MODEL),
            row(D_MODEL), row(D_MODEL), mat(D_MODEL, LANES), row(LANES),
        ],
        out_specs=[
            pl.BlockSpec((rows, D_MODEL), lambda i: (i, 0)),
            pl.BlockSpec((rows, D_MODEL // 2), lambda i: (i, 0)),
            tok, tok, tok,
            pl.BlockSpec((N_EXPERTS, rows), lambda i: (0, i)),
            pl.BlockSpec((N_EXPERTS, rows), lambda i: (0, i)),
            pl.BlockSpec((1, 1, LANES), lambda i: (i, 0, 0)),
            pl.BlockSpec((1, LANES), lambda i: (0, 0)),
        ],
        out_shape=[
            jax.ShapeDtypeStruct((t, D_MODEL), F32),
            jax.ShapeDtypeStruct((t, D_MODEL // 2), jnp.uint32),
            jax.ShapeDtypeStruct((t, LANES), F32),
            jax.ShapeDtypeStruct((t, LANES), jnp.int32),
            jax.ShapeDtypeStruct((t, LANES), jnp.int32),
            jax.ShapeDtypeStruct((N_EXPERTS, t), F32),
            jax.ShapeDtypeStruct((N_EXPERTS, t), F32),
            jax.ShapeDtypeStruct((t // rows, 1, LANES), jnp.int32),
            jax.ShapeDtypeStruct((1, LANES), jnp.int32),
        ],
        scratch_shapes=[
            pltpu.VMEM((SUBLANES, D_MODEL), F32),
            pltpu.VMEM((1, LANES), F32),
        ],
        compiler_params=pltpu.CompilerParams(
            dimension_semantics=("arbitrary",), vmem_limit_bytes=VMEM_LIMIT),
        name="post_mix",
    )(y, u, u, u, u, u, x, nw, wa, csw, wb, bg, wo, g1, lng, lnb, sh2, sc2, wr, br)


def _invert_kernel(vt_ref, ve_ref, nv_ref, gs_ref, run_ref, cst_ref, sidt_ref, o_ref, ptr,
                   *, tile, tok_tile, n_tok_tiles):
    v = pl.program_id(0)
    i = vt_ref[v]
    e = ve_ref[v]
    vp = jnp.maximum(v - 1, 0)
    new_tile = jnp.logical_or(v == 0, vt_ref[vp] != i)
    new_exp = jnp.logical_or(v == 0, ve_ref[vp] != e)

    @pl.when(new_exp)
    def _():
        ptr[0] = 0

    @pl.when(new_tile)
    def _():
        o_ref[...] = jnp.zeros(o_ref.shape, jnp.int32)

    @pl.when(v < nv_ref[0])
    def _():
        g0 = gs_ref[e]
        row0 = i * tile
        ra = jnp.maximum(g0, row0) - g0
        rb = jnp.minimum(gs_ref[e + 1], row0 + tile) - g0
        b_lo = lax.while_loop(lambda b: run_ref[(b + 1) * N_EXPERTS + e] <= ra,
                              lambda b: b + 1, ptr[0])
        ptr[0] = b_lo
        b_hi = lax.while_loop(
            lambda b: jnp.logical_and(b < n_tok_tiles, run_ref[b * N_EXPERTS + e] < rb),
            lambda b: b + 1, b_lo)
        n_blk = tile // LANES
        row_in_blk = lax.broadcasted_iota(jnp.int32, (LANES, LANES), 0).astype(F32)
        firsts = [(row0 - g0 + 1 + k * LANES).astype(F32) for k in range(n_blk)]

        def body(b, accs):
            c0 = pl.multiple_of(b * tok_tile, tok_tile)
            cs_row = cst_ref[pl.ds(e, 1), pl.ds(c0, tok_tile)]
            sid_row = sidt_ref[pl.ds(e, 1), pl.ds(c0, tok_tile)]
            out = []
            for k in range(n_blk):
                rel = cs_row - firsts[k]
                acc = accs[k]
                for j in range(tok_tile // LANES):
                    lanes = slice(j * LANES, (j + 1) * LANES)
                    acc = acc + jnp.where(rel[:, lanes] == row_in_blk, sid_row[:, lanes], 0.0)
                out.append(acc)
            return tuple(out)

        accs = lax.fori_loop(b_lo, b_hi, body,
                             tuple(jnp.zeros((LANES, LANES), F32) for _ in range(n_blk)))
        for k in range(n_blk):
            contrib = jnp.sum(accs[k].T, axis=0, keepdims=True)
            lanes = slice(k * LANES, (k + 1) * LANES)
            o_ref[0, :, lanes] = o_ref[0, :, lanes] + contrib.astype(jnp.int32)


def _invert(v_tile, v_exp, n_visits, gs, runflat, cst, sidt, n_tiles, tile, tok_tile):
    t = cst.shape[1]
    n_vis = v_tile.shape[0]
    kern = functools.partial(_invert_kernel, tile=tile, tok_tile=tok_tile,
                             n_tok_tiles=t // tok_tile)
    return pl.pallas_call(
        kern,
        grid_spec=pltpu.PrefetchScalarGridSpec(
            num_scalar_prefetch=5,
            grid=(n_vis,),
            in_specs=[
                pl.BlockSpec((N_EXPERTS, t), lambda v, *_: (0, 0)),
                pl.BlockSpec((N_EXPERTS, t), lambda v, *_: (0, 0)),
            ],
            out_specs=pl.BlockSpec((1, 1, tile), lambda v, vt, *_: (vt[v], 0, 0)),
            scratch_shapes=[pltpu.SMEM((1,), jnp.int32)],
        ),
        out_shape=jax.ShapeDtypeStruct((n_tiles, 1, tile), jnp.int32),
        compiler_params=pltpu.CompilerParams(
            dimension_semantics=("arbitrary",), vmem_limit_bytes=VMEM_LIMIT),
        name="moe_invert",
    )(v_tile, v_exp, n_visits, gs, runflat, cst, sidt)


N_FF_BLK = 4
FF_BLK = EXPERT_FF // N_FF_BLK
N_STAGES = 2 * N_FF_BLK


def _experts_kernel(vt_ref, ve_ref, vn_ref, nv_ref, gs_ref,
                    ord_cur, ord_next, h_hbm, wgu_hbm, bgu_ref, wd_hbm, bd_ref,
                    o_ref, h_vmem, x_a, x_b, act, wgu_st, wd_st, wgu_bf, wd_bf,
                    hsem, wsem, *, tile, n_tok, layer):
    v = pl.program_id(0)
    nv = nv_ref[0]
    i = vt_ref[v]
    e = ve_ref[v]
    vp = jnp.maximum(v - 1, 0)
    active = v < nv
    new_tile = jnp.logical_or(v == 0, vt_ref[vp] != i)
    new_exp = jnp.logical_or(v == 0, ve_ref[vp] != e)
    even = lax.rem(i, 2) == 0
    half = D_MODEL // 2
    xs = (x_a, x_b)

    def weight_copies(expert):
        return (pltpu.make_async_copy(wgu_hbm.at[layer, expert], wgu_st, wsem.at[0]),
                pltpu.make_async_copy(wd_hbm.at[layer, expert], wd_st, wsem.at[1]))

    def fetch_row(order_ref, r, dst):
        sid = order_ref[r]
        tok = sid & (n_tok - 1) if n_tok & (n_tok - 1) == 0 else lax.rem(sid, n_tok)
        dst[pl.ds(r, 1), :] = h_vmem[pl.ds(tok, 1), :]

    @pl.when(jnp.logical_not(active))
    def _():
        o_ref[...] = jnp.zeros(o_ref.shape, jnp.uint32)

    @pl.when(v == 0)
    def _():
        tokens = pltpu.make_async_copy(h_hbm, h_vmem, hsem)
        tokens.start()
        for c in weight_copies(e):
            c.start()
        tokens.wait()
        for r in range(tile):
            fetch_row(ord_cur, r, x_a)

    @pl.when(jnp.logical_and(active, new_exp))
    def _():
        for c in weight_copies(e):
            c.wait()
        wgu_bf[...] = wgu_st[...].astype(BF16)
        wd_bf[...] = wd_st[...].astype(BF16)

        @pl.when(vn_ref[v] >= 0)
        def _():
            for c in weight_copies(vn_ref[v]):
                c.start()

    rowpos = i * tile + lax.broadcasted_iota(jnp.int32, (tile, 1), 0)
    mine = jnp.logical_and(rowpos >= gs_ref[e], rowpos < gs_ref[e + 1])

    def compute(x_ref, first, between):
        x_lo, x_hi = _unpack_bf16_pair(x_ref[...])
        xb = jnp.concatenate([x_lo.astype(BF16), x_hi.astype(BF16)], axis=1)
        for cb in range(N_FF_BLK):
            c0 = cb * FF_BLK
            g = _dot(xb, wgu_bf[:, c0:c0 + FF_BLK]) + bgu_ref[:, c0:c0 + FF_BLK]
            u = (_dot(xb, wgu_bf[:, EXPERT_FF + c0:EXPERT_FF + c0 + FF_BLK])
                 + bgu_ref[:, EXPERT_FF + c0:EXPERT_FF + c0 + FF_BLK])
            g = jnp.minimum(g, SWIGLU_LIMIT)
            u = jnp.clip(u, -SWIGLU_LIMIT, SWIGLU_LIMIT)
            act[:, c0:c0 + FF_BLK] = ((u + 1.0) * g * _sigmoid(SWIGLU_ALPHA * g)).astype(BF16)
            between(cb)
        for pb in range(N_FF_BLK // 2):
            c0 = pb * FF_BLK
            o_lo = _dot(act[...], wd_bf[:, c0:c0 + FF_BLK]) + bd_ref[:, c0:c0 + FF_BLK]
            between(N_FF_BLK + 2 * pb)
            o_hi = (_dot(act[...], wd_bf[:, half + c0:half + c0 + FF_BLK])
                    + bd_ref[:, half + c0:half + c0 + FF_BLK])
            packed = _pack_bf16_pair(o_lo, o_hi)
            keep = jnp.uint32(0) if first else o_ref[:, c0:c0 + FF_BLK]
            o_ref[:, c0:c0 + FF_BLK] = jnp.where(mine, packed, keep)
            between(N_FF_BLK + 2 * pb + 1)

    per_stage = tile // N_STAGES

    for p in range(2):
        x_cur, x_nxt = xs[p], xs[1 - p]
        on_parity = even if p == 0 else jnp.logical_not(even)

        @pl.when(jnp.logical_and(jnp.logical_and(active, new_tile), on_parity))
        def _(x_cur=x_cur, x_nxt=x_nxt):
            def between(j):
                for r in range(j * per_stage, (j + 1) * per_stage):
                    fetch_row(ord_next, r, x_nxt)

            compute(x_cur, True, between)

        @pl.when(jnp.logical_and(jnp.logical_and(active, jnp.logical_not(new_tile)), on_parity))
        def _(x_cur=x_cur):
            compute(x_cur, False, lambda j: None)


def _experts(v_tile, v_exp, v_next, n_visits, gs, order_ext, h2, w_gu, b_gu, w_down, b_down,
             layer, n_tiles, tile):
    n_vis = v_tile.shape[0]
    n_tok = h2.shape[0]
    ff2 = 2 * EXPERT_FF
    half = D_MODEL // 2
    kern = functools.partial(_experts_kernel, tile=tile, n_tok=n_tok, layer=layer)
    smem_blk = lambda off: pl.BlockSpec(
        (tile,), lambda v, vt, *_: (vt[v] + off,), memory_space=pltpu.SMEM)
    bias = lambda n: pl.BlockSpec(
        (None, None, 1, n), lambda v, vt, ve, *_: (layer, ve[v], 0, 0))
    hbm = pl.BlockSpec(memory_space=pl.ANY)
    return pl.pallas_call(
        kern,
        grid_spec=pltpu.PrefetchScalarGridSpec(
            num_scalar_prefetch=5,
            grid=(n_vis,),
            in_specs=[smem_blk(0), smem_blk(1), hbm, hbm, bias(ff2), hbm, bias(D_MODEL)],
            out_specs=pl.BlockSpec(
                (tile, half),
                lambda v, vt, ve, vn, nv, *_: (jnp.where(v < nv[0], vt[v], n_tiles), 0)),
            scratch_shapes=[
                pltpu.VMEM((n_tok, half), jnp.uint32),
                pltpu.VMEM((tile, half), jnp.uint32),
                pltpu.VMEM((tile, half), jnp.uint32),
                pltpu.VMEM((tile, EXPERT_FF), BF16),
                pltpu.VMEM((D_MODEL, ff2), F32),
                pltpu.VMEM((EXPERT_FF, D_MODEL), F32),
                pltpu.VMEM((D_MODEL, ff2), BF16),
                pltpu.VMEM((EXPERT_FF, D_MODEL), BF16),
                pltpu.SemaphoreType.DMA(()),
                pltpu.SemaphoreType.DMA((2,)),
            ],
        ),
        out_shape=jax.ShapeDtypeStruct(((n_tiles + 1) * tile, half), jnp.uint32),
        compiler_params=pltpu.CompilerParams(
            dimension_semantics=("arbitrary",), vmem_limit_bytes=EXPERTS_VMEM_LIMIT),
        name="moe_experts",
    )(v_tile, v_exp, v_next, n_visits, gs, order_ext, order_ext, h2,
      w_gu, b_gu[:, :, None, :], w_down, b_down[:, :, None, :])


SEG_CHUNK = 64


def _segment_layout(gs, run_lo, run_hi):
    n = run_hi - run_lo
    lead = (gs + run_lo) & (SUBLANES - 1)
    span = jnp.where(n > 0, n + lead, 0)
    log_chunk = SEG_CHUNK.bit_length() - 1
    padded = lax.shift_left(lax.shift_right_logical(span + (SEG_CHUNK - 1), log_chunk), log_chunk)
    return lead, span, padded


def _combine_kernel(gs_ref, run_ref, row_ref, os_hbm, prob_ref, x1_ref, g2_ref, lng_ref,
                    lnb_ref, x2_ref, seg, g_0, g_1, g_2, g_3, n_started, sem, *, rows, n_blocks):
    b = pl.program_id(0)
    slot = lax.rem(b, 2)
    max_chunks = rows // SEG_CHUNK + 1

    def chunk_copy(src, dst, s):
        return pltpu.make_async_copy(os_hbm.at[pl.ds(src, SEG_CHUNK)],
                                     seg.at[s, pl.ds(dst, SEG_CHUNK)], sem.at[s])

    def start_segments(blk, s):
        off = jnp.int32(0)
        started = jnp.int32(0)
        for e in range(N_EXPERTS):
            first = run_ref[blk * N_EXPERTS + e]
            lead, span, padded = _segment_layout(gs_ref[e], first,
                                                 run_ref[(blk + 1) * N_EXPERTS + e])
            src0 = gs_ref[e] + first - lead
            for c in range(max_chunks):
                @pl.when(c * SEG_CHUNK < span)
                def _(off=off, src0=src0, c=c):
                    chunk_copy(pl.multiple_of(src0 + c * SEG_CHUNK, SUBLANES),
                               pl.multiple_of(off + c * SEG_CHUNK, SEG_CHUNK), s).start()
            off = off + padded
            started = started + lax.shift_right_logical(padded, SEG_CHUNK.bit_length() - 1)
        n_started[s] = started

    @pl.when(b == 0)
    def _():
        start_segments(0, 0)

    @pl.when(b + 1 < n_blocks)
    def _():
        start_segments(b + 1, 1 - slot)

    def wait_one(_, carry):
        chunk_copy(0, 0, slot).wait()
        return carry

    lax.fori_loop(0, n_started[slot], wait_one, 0)

    bufs = (g_0, g_1, g_2, g_3)
    for t in range(rows):
        for k in range(TOP_K):
            bufs[k][pl.ds(t, 1), :] = seg[slot, pl.ds(row_ref[t * TOP_K + k], 1), :]

    prob = prob_ref[...]
    ffn_lo = jnp.zeros((rows, D_MODEL // 2), F32)
    ffn_hi = jnp.zeros((rows, D_MODEL // 2), F32)
    for k in range(TOP_K):
        lo, hi = _unpack_bf16_pair(bufs[k][...])
        ffn_lo = ffn_lo + prob[:, k:k + 1] * lo
        ffn_hi = ffn_hi + prob[:, k:k + 1] * hi
    ffn = jnp.concatenate([ffn_lo, ffn_hi], axis=1)
    x2_ref[...] = (_layer_norm(DN_ALPHA * x1_ref[...] + g2_ref[...] * ffn) * lng_ref[...]
                   + lnb_ref[...])


def _combine(gs, runflat, idx, lrank, out_sorted, probs, x1, g2, lng, lnb, rows):
    t = x1.shape[0]
    half = D_MODEL // 2
    n_blocks = t // rows
    run = runflat.reshape(n_blocks + 1, N_EXPERTS)
    lead, _, padded = _segment_layout(gs[None, :N_EXPERTS], run[:-1], run[1:])
    seg_start = jnp.cumsum(padded, axis=1) - padded + lead
    onehot = idx[:, :TOP_K, None] == jnp.arange(N_EXPERTS, dtype=jnp.int32)[None, None, :]
    start_tk = jnp.sum(jnp.where(onehot, jnp.repeat(seg_start, rows, axis=0)[:, None, :], 0),
                       axis=-1)
    row_flat = (start_tk + lrank[:, :TOP_K]).reshape(-1).astype(jnp.int32)
    seg_rows = rows * TOP_K + N_EXPERTS * (SEG_CHUNK + SUBLANES)
    kern = functools.partial(_combine_kernel, rows=rows, n_blocks=n_blocks)
    row = pl.BlockSpec((1, D_MODEL), lambda i, *_: (0, 0))
    smem = pl.BlockSpec((rows * TOP_K,), lambda i, *_: (i,), memory_space=pltpu.SMEM)
    return pl.pallas_call(
        kern,
        grid_spec=pltpu.PrefetchScalarGridSpec(
            num_scalar_prefetch=2,
            grid=(n_blocks,),
            in_specs=[
                smem,
                pl.BlockSpec(memory_space=pl.ANY),
                pl.BlockSpec((rows, LANES), lambda i, *_: (i, 0)),
                pl.BlockSpec((rows, D_MODEL), lambda i, *_: (i, 0)),
                row, row, row,
            ],
            out_specs=pl.BlockSpec((rows, D_MODEL), lambda i, *_: (i, 0)),
            scratch_shapes=[
                pltpu.VMEM((2, seg_rows, half), jnp.uint32),
                pltpu.VMEM((rows, half), jnp.uint32),
                pltpu.VMEM((rows, half), jnp.uint32),
                pltpu.VMEM((rows, half), jnp.uint32),
                pltpu.VMEM((rows, half), jnp.uint32),
                pltpu.SMEM((2,), jnp.int32),
                pltpu.SemaphoreType.DMA((2,)),
            ],
        ),
        out_shape=jax.ShapeDtypeStruct((t, D_MODEL), F32),
        compiler_params=pltpu.CompilerParams(
            dimension_semantics=("arbitrary",), vmem_limit_bytes=VMEM_LIMIT),
        name="moe_combine",
    )(gs, runflat, row_flat, out_sorted, probs, x1, g2, lng, lnb)


def _tile(t, pref):
    return pref if t % pref == 0 else t


def _visit_plan(counts, tile, n_vis):
    ge = jnp.cumsum(counts)
    gs = ge - counts
    t_lo = gs // tile
    nt = jnp.where(counts > 0, (ge - 1) // tile - t_lo + 1, 0)
    vend = jnp.cumsum(nt)
    vbase = vend - nt
    n_visits = vend[-1:]
    vc = jnp.minimum(jnp.arange(n_vis, dtype=jnp.int32), n_visits[0] - 1)
    v_exp = jnp.sum((vend[None, :] <= vc[:, None]).astype(jnp.int32), axis=1)
    onehot = (v_exp[:, None] == jnp.arange(N_EXPERTS, dtype=jnp.int32)[None, :]).astype(jnp.int32)
    v_tile = jnp.sum(onehot * (t_lo - vbase)[None, :], axis=1) + vc
    gs33 = jnp.concatenate([gs, ge[-1:]])
    after = jnp.sum(onehot * vend[None, :], axis=1)
    e_after = jnp.sum((vend[None, :] <= after[:, None]).astype(jnp.int32), axis=1)
    v_next = jnp.where(after < n_visits[0], e_after, -1)
    return (v_tile.astype(jnp.int32), v_exp.astype(jnp.int32), v_next.astype(jnp.int32),
            n_visits.astype(jnp.int32), gs33.astype(jnp.int32))


def kernel(x, c, w_ada, b_ada, w_in, conv_ssd_w, conv_ssd_b, dt_bias, a_log, d_skip, ssd_norm_w,
           w_ssd_out, conv_short_w, w_short_out, b_gate, w_o, ln1_g, ln1_b, w_router, b_router,
           w_gu, b_gu, w_down, b_down, ln2_g, ln2_b):
    batch, seq, d = x.shape
    assert batch == 1 and d == D_MODEL
    depth = w_in.shape[0]
    t = seq
    xt = x.reshape(t, d)

    tm_in = _tile(t, 1024)
    rows_ssd = _tile(t, 512)
    chunk = 128
    rows_post = 256
    tile_e = 256
    assert t % rows_post == 0
    n_tiles = t * TOP_K // tile_e
    n_vis = n_tiles + N_EXPERTS

    mods = _ada_mod(c, w_ada, b_ada)

    w_t = jnp.swapaxes(w_in, 1, 2)
    w_t_bf = w_t.astype(BF16)
    c_dt = _C_SCB - SSD_HEADS
    w_dt = jnp.pad(w_t[:, c_dt:_C_SCB], ((0, 0), (0, LANES - SSD_HEADS), (0, 0)))
    w_r = jnp.pad(w_router, ((0, 0), (0, 0), (0, LANES - N_EXPERTS)))
    b_r = jnp.pad(b_router, ((0, 0), (0, LANES - N_EXPERTS)), constant_values=NEG_BIG)
    wa_bf = w_ssd_out.astype(BF16)
    wb_bf = w_short_out.astype(BF16)
    wo_bf = w_o.astype(BF16)
    cool = jnp.zeros((tile_e,), jnp.int32)

    for l in range(depth):
        m = mods[l]
        sh1, sc1, g1, sh2, sc2, g2 = [m[:, k * d:(k + 1) * d] for k in range(N_ADA)]
        u, dt_raw = _in_proj(xt, sh1, sc1, w_t_bf, w_dt[l], l, tm_in)
        y = _ssd(u, dt_raw, conv_ssd_w[l], conv_ssd_b[l], dt_bias[l], a_log[l], d_skip[l],
                 rows_ssd, chunk)
        vecs = (ssd_norm_w[l][None, :], conv_short_w[l], b_gate[l][None, :], g1,
                ln1_g[l][None, :], ln1_b[l][None, :], sh2, sc2, b_r[l][None, :])
        x1, h2, probs, idx, lrank, cst, sidt, runtab, cnt = _post(
            y, u, xt, vecs, (wa_bf[l], wb_bf[l], wo_bf[l], w_r[l]), rows_post)

        v_tile, v_exp, v_next, n_visits, gs = _visit_plan(cnt[0, :N_EXPERTS], tile_e, n_vis)
        runflat = jnp.concatenate([runtab[:, 0, :N_EXPERTS], cnt[:, :N_EXPERTS]]).reshape(-1)
        order = _invert(v_tile, v_exp, n_visits, gs, runflat, cst, sidt, n_tiles, tile_e,
                        rows_post)
        order_ext = jnp.concatenate([order.reshape(-1), cool])
        out_sorted = _experts(v_tile, v_exp, v_next, n_visits, gs, order_ext, h2, w_gu, b_gu,
                              w_down, b_down, l, n_tiles, tile_e)
        xt = _combine(gs, runflat, idx, lrank, out_sorted, probs, x1, g2, ln2_g[l][None, :],
                      ln2_b[l][None, :], rows_post)

    return xt.reshape(batch, seq, d)
```

```python
import functools

import jax
import jax.numpy as jnp
from jax import lax
from jax.experimental import pallas as pl
from jax.experimental.pallas import tpu as pltpu

F32 = jnp.float32
BF16 = jnp.bfloat16

D_MODEL = 1024
SSD_INNER = 2048
SSD_HEADS = 32
SSD_HEADDIM = 64
SSD_GROUPS = 4
SSD_STATE = 128
SSD_CONV = 4
SC_CONV = 3
N_EXPERTS = 32
TOP_K = 4
EXPERT_FF = 1024
SWIGLU_LIMIT = 7.0
SWIGLU_ALPHA = 1.702
DEPTH = 4
DN_ALPHA = (2.0 * DEPTH) ** 0.25
LN_EPS = 1e-5
RMS_EPS = 1e-5
N_ADA = 6

LANES = 128
SUBLANES = 8
U_MAIN = 10240
U_BLK = 2048
IN_PROJ_SUB = 512
_C_BC = 2 * SSD_INNER
_C_SCB = _C_BC + 2 * SSD_GROUPS * SSD_STATE + SSD_HEADS
_C_SCC = _C_SCB + D_MODEL
U_SRC = ((0,), (SSD_INNER,), (_C_SCC,), (_C_SCC + U_BLK,), (_C_BC, _C_SCB))
COL_Z = 0
COL_XS = 2048
COL_SCC = 4096
COL_SCX = 5120
COL_GATE = 6144
COL_B = 8192
COL_C = 8704
COL_SCB = 9216
NEG_BIG = -1e30

VMEM_LIMIT = 56 * 1024 * 1024
EXPERTS_VMEM_LIMIT = 60 * 1024 * 1024


def _sigmoid(v):
    return 1.0 / (1.0 + jnp.exp(-v))


def _softplus(v):
    return jnp.maximum(v, 0.0) + jnp.log(1.0 + jnp.exp(-jnp.abs(v)))


def _layer_norm(v):
    mu = jnp.mean(v, axis=-1, keepdims=True)
    vc = v - mu
    var = jnp.mean(vc * vc, axis=-1, keepdims=True)
    return vc * lax.rsqrt(var + LN_EPS)


def _split3(v):
    hi = v.astype(BF16)
    r1 = v - hi.astype(F32)
    mid = r1.astype(BF16)
    lo = (r1 - mid.astype(F32)).astype(BF16)
    return hi, mid, lo


def _dot(a, b):
    return jnp.dot(a, b, preferred_element_type=F32)


def _pack_bf16_pair(lo, hi):
    lo_bits = pltpu.bitcast(lo.astype(BF16).astype(F32), jnp.uint32)
    hi_bits = pltpu.bitcast(hi.astype(BF16).astype(F32), jnp.uint32)
    return lax.shift_right_logical(lo_bits, jnp.uint32(16)) | (hi_bits & jnp.uint32(0xFFFF0000))


def _unpack_bf16_pair(w):
    lo = pltpu.bitcast(lax.shift_left(w, jnp.uint32(16)), F32)
    hi = pltpu.bitcast(w & jnp.uint32(0xFFFF0000), F32)
    return lo, hi


def _dot_split(a, b):
    a_hi = a.astype(BF16)
    a_mid = (a - a_hi.astype(F32)).astype(BF16)
    b_hi = b.astype(BF16)
    b_mid = (b - b_hi.astype(F32)).astype(BF16)
    return _dot(a_hi, b_hi) + _dot(a_hi, b_mid) + _dot(a_mid, b_hi)


def _dot_exact_lhs(a_bf16, v):
    hi, mid, lo = _split3(v)
    return _dot(a_bf16, hi) + _dot(a_bf16, mid) + _dot(a_bf16, lo)


def _ada_kernel(c_ref, w_ref, b_ref, o_ref):
    c = c_ref[...]
    s = c * _sigmoid(c)
    o_ref[0] = jnp.sum(w_ref[0] * s, axis=0, keepdims=True) + b_ref[0]


def _ada_mod(c, w_ada, b_ada):
    depth, d, n = w_ada.shape
    tn = 1024
    return pl.pallas_call(
        _ada_kernel,
        grid=(depth, n // tn),
        in_specs=[
            pl.BlockSpec((d, 1), lambda l, j: (0, 0)),
            pl.BlockSpec((1, d, tn), lambda l, j: (l, 0, j)),
            pl.BlockSpec((1, 1, tn), lambda l, j: (l, 0, j)),
        ],
        out_specs=pl.BlockSpec((1, 1, tn), lambda l, j: (l, 0, j)),
        out_shape=jax.ShapeDtypeStruct((depth, 1, n), F32),
        compiler_params=pltpu.CompilerParams(
            dimension_semantics=("arbitrary", "arbitrary"), vmem_limit_bytes=VMEM_LIMIT),
        name="ada_mod",
    )(c.reshape(d, 1), w_ada, b_ada.reshape(depth, 1, n))


def _dot_nt(a, b):
    return lax.dot_general(a, b, (((1,), (1,)), ((), ())), preferred_element_type=F32)


def _inproj_kernel(x_ref, sh_ref, sc_ref, w_hbm, wdt_ref, u_ref, dt_ref, h_scr, wbuf, wsem,
                   *, layer, n_steps):
    i = pl.program_id(0)
    j = pl.program_id(1)
    n_blk = len(U_SRC)
    step = i * n_blk + j
    slot = lax.rem(step, 2)

    def block_copies(blk, s):
        pieces = U_SRC[blk]
        rows_per = U_BLK // len(pieces)
        return [pltpu.make_async_copy(w_hbm.at[layer, pl.ds(src, rows_per)],
                                      wbuf.at[s, pl.ds(p * rows_per, rows_per)], wsem.at[s])
                for p, src in enumerate(pieces)]

    @pl.when(step == 0)
    def _():
        for c in block_copies(0, 0):
            c.start()

    for blk in range(n_blk):
        @pl.when(jnp.logical_and(j == blk, step + 1 < n_steps))
        def _(blk=blk):
            for c in block_copies((blk + 1) % n_blk, 1 - slot):
                c.start()

    tm = x_ref.shape[0]
    sub = min(tm, IN_PROJ_SUB)

    @pl.when(j == 0)
    def _():
        w = wdt_ref[...]
        w_hi = w.astype(BF16)
        w_mid = (w - w_hi.astype(F32)).astype(BF16)
        for r in range(0, tm, sub):
            h = _layer_norm(x_ref[r:r + sub, :]) * (1.0 + sc_ref[...]) + sh_ref[...]
            h_hi = h.astype(BF16)
            h_scr[r:r + sub, :] = h_hi
            h_mid = (h - h_hi.astype(F32)).astype(BF16)
            dt_ref[r:r + sub, :] = (_dot_nt(h_hi, w_hi) + _dot_nt(h_hi, w_mid)
                                    + _dot_nt(h_mid, w_hi))

    for blk in range(n_blk):
        @pl.when(j == blk)
        def _(blk=blk):
            for c in block_copies(blk, slot):
                c.wait()

    for r in range(0, tm, sub):
        u_ref[r:r + sub, :] = _dot_nt(h_scr[r:r + sub, :], wbuf[slot]).astype(BF16)


def _in_proj(x, sh, sc, w_t, w_dt, layer, tm):
    t, d = x.shape
    tn = U_BLK
    assert w_t.shape[2] == d and w_dt.shape == (LANES, d) and len(U_SRC) * tn == U_MAIN
    kern = functools.partial(_inproj_kernel, layer=layer, n_steps=(t // tm) * len(U_SRC))
    return pl.pallas_call(
        kern,
        grid=(t // tm, U_MAIN // tn),
        in_specs=[
            pl.BlockSpec((tm, d), lambda i, j: (i, 0)),
            pl.BlockSpec((1, d), lambda i, j: (0, 0)),
            pl.BlockSpec((1, d), lambda i, j: (0, 0)),
            pl.BlockSpec(memory_space=pl.ANY),
            pl.BlockSpec((LANES, d), lambda i, j: (0, 0)),
        ],
        out_specs=[
            pl.BlockSpec((tm, tn), lambda i, j: (i, j)),
            pl.BlockSpec((tm, LANES), lambda i, j: (i, 0)),
        ],
        out_shape=[
            jax.ShapeDtypeStruct((t, U_MAIN), BF16),
            jax.ShapeDtypeStruct((t, LANES), F32),
        ],
        scratch_shapes=[
            pltpu.VMEM((tm, d), BF16),
            pltpu.VMEM((2, tn, d), BF16),
            pltpu.SemaphoreType.DMA((2,)),
        ],
        compiler_params=pltpu.CompilerParams(
            dimension_semantics=("arbitrary", "arbitrary"), vmem_limit_bytes=VMEM_LIMIT),
        name="in_proj",
    )(x, sh, sc, w_t, w_dt)


def _ssd_kernel(xs_ref, b_ref, c_ref, dt_ref, cwx_ref, cwb_ref, cwc_ref, cbx_ref, cbb_ref,
                cbc_ref, dtb_ref, alog_ref, dskip_ref, y_ref,
                xbuf, bbuf, cbuf, xcs, bcs, ccs, state, *, chunk, rows):
    @pl.when(pl.program_id(0) == 0)
    def _():
        xbuf[...] = jnp.zeros(xbuf.shape, F32)
        bbuf[...] = jnp.zeros(bbuf.shape, F32)
        cbuf[...] = jnp.zeros(cbuf.shape, F32)
        state[...] = jnp.zeros(state.shape, F32)

    def conv_silu(in_ref, tail, w_ref, bias_ref, out_scr):
        x = in_ref[...].astype(F32)
        ext = jnp.concatenate([tail[...], x], axis=0)
        acc = bias_ref[...] + w_ref[SSD_CONV - 1:SSD_CONV, :] * x
        for j in range(1, SSD_CONV):
            shifted = pltpu.roll(ext, j, axis=0)[SUBLANES:, :]
            acc = acc + w_ref[SSD_CONV - 1 - j:SSD_CONV - j, :] * shifted
        out_scr[...] = acc * _sigmoid(acc)
        tail[...] = x[rows - SUBLANES:, :]

    conv_silu(xs_ref, xbuf, cwx_ref, cbx_ref, xcs)
    conv_silu(b_ref, bbuf, cwb_ref, cbb_ref, bcs)
    conv_silu(c_ref, cbuf, cwc_ref, cbc_ref, ccs)

    li = lax.broadcasted_iota(jnp.int32, (chunk, chunk), 0)
    si = lax.broadcasted_iota(jnp.int32, (chunk, chunk), 1)
    causal = li >= si
    tri = jnp.where(causal, 1.0, 0.0).astype(BF16)
    first_half = lax.broadcasted_iota(jnp.int32, (1, LANES), 1) < SSD_HEADDIM
    a_row = -jnp.exp(alog_ref[...])
    heads_per_group = SSD_HEADS // SSD_GROUPS

    def chunk_body(ci, carry):
        r0 = pl.multiple_of(ci * chunk, chunk)
        dt = _softplus(dt_ref[pl.ds(r0, chunk), :] + dtb_ref[...])
        la = dt * a_row
        acum = _dot_exact_lhs(tri, la)
        acum_t = acum.T
        dt_t = dt.T
        last_t = acum_t[:, chunk - 1:chunk]
        w_t = jnp.exp(last_t - acum_t) * dt_t
        cdec_t = jnp.exp(last_t)

        cb = []
        bt = []
        cg = []
        for g in range(SSD_GROUPS):
            bg = bcs[pl.ds(r0, chunk), g * SSD_STATE:(g + 1) * SSD_STATE]
            cgv = ccs[pl.ds(r0, chunk), g * SSD_STATE:(g + 1) * SSD_STATE]
            btg = bg.T
            bt.append(btg)
            cg.append(cgv)
            cb.append(_dot(cgv.astype(BF16), btg.astype(BF16)))

        for pr in range(SSD_HEADS // 2):
            lo = pr * LANES
            xs_pair = xcs[pl.ds(r0, chunk), lo:lo + LANES]
            xs_bf = xs_pair.astype(BF16)
            prev = state[:, lo:lo + LANES]
            lhs_m, lhs_c, lhs_b, cd = [], [], [], []
            for h in (2 * pr, 2 * pr + 1):
                g = h // heads_per_group
                col = jnp.broadcast_to(acum[:, h:h + 1], (chunk, chunk))
                row = acum_t[h:h + 1, :]
                dec = jnp.exp(jnp.where(causal, col - row, NEG_BIG))
                lhs_m.append((cb[g] * dec * dt_t[h:h + 1, :]).astype(BF16))
                lhs_c.append((cg[g] * jnp.exp(col)).astype(BF16))
                lhs_b.append((bt[g] * w_t[h:h + 1, :]).astype(BF16))
                cd.append(cdec_t[h:h + 1, :])
            out = (_dot(jnp.concatenate(lhs_m, axis=0), xs_bf)
                   + _dot(jnp.concatenate(lhs_c, axis=0), prev.astype(BF16)))
            y_pair = jnp.where(first_half, out[0:chunk, :], out[chunk:2 * chunk, :])
            y_ref[pl.ds(r0, chunk), lo:lo + LANES] = (
                y_pair + xs_pair * dskip_ref[:, lo:lo + LANES]).astype(y_ref.dtype)
            st = _dot(jnp.concatenate(lhs_b, axis=0), xs_bf)
            cd_pair = jnp.where(first_half, cd[0], cd[1])
            state[:, lo:lo + LANES] = prev * cd_pair + jnp.where(
                first_half, st[0:SSD_STATE, :], st[SSD_STATE:2 * SSD_STATE, :])
        return carry

    lax.fori_loop(0, rows // chunk, chunk_body, 0)


def _ssd(u, dt_raw, cw, cb, dt_bias, a_log, d_skip, rows, chunk):
    t = u.shape[0]
    gn = SSD_GROUPS * SSD_STATE
    pad = LANES - SSD_HEADS
    assert chunk == SSD_STATE
    kern = functools.partial(_ssd_kernel, chunk=chunk, rows=rows)
    full = lambda shape: pl.BlockSpec(shape, lambda i: (0, 0))
    return pl.pallas_call(
        kern,
        grid=(t // rows,),
        in_specs=[
            pl.BlockSpec((rows, SSD_INNER), lambda i: (i, COL_XS // SSD_INNER)),
            pl.BlockSpec((rows, gn), lambda i: (i, COL_B // gn)),
            pl.BlockSpec((rows, gn), lambda i: (i, COL_C // gn)),
            pl.BlockSpec((rows, LANES), lambda i: (i, 0)),
            full((SSD_CONV, SSD_INNER)), full((SSD_CONV, gn)), full((SSD_CONV, gn)),
            full((1, SSD_INNER)), full((1, gn)), full((1, gn)),
            full((1, LANES)), full((1, LANES)), full((1, SSD_INNER)),
        ],
        out_specs=pl.BlockSpec((rows, SSD_INNER), lambda i: (i, 0)),
        out_shape=jax.ShapeDtypeStruct((t, SSD_INNER), BF16),
        scratch_shapes=[
            pltpu.VMEM((SUBLANES, SSD_INNER), F32),
            pltpu.VMEM((SUBLANES, gn), F32),
            pltpu.VMEM((SUBLANES, gn), F32),
            pltpu.VMEM((rows, SSD_INNER), F32),
            pltpu.VMEM((rows, gn), F32),
            pltpu.VMEM((rows, gn), F32),
            pltpu.VMEM((SSD_STATE, SSD_INNER), F32),
        ],
        compiler_params=pltpu.CompilerParams(
            dimension_semantics=("arbitrary",), vmem_limit_bytes=VMEM_LIMIT),
        name="ssd",
    )(u, u, u, dt_raw,
      cw[:, :SSD_INNER], cw[:, SSD_INNER:SSD_INNER + gn], cw[:, SSD_INNER + gn:],
      cb[None, :SSD_INNER], cb[None, SSD_INNER:SSD_INNER + gn], cb[None, SSD_INNER + gn:],
      jnp.pad(dt_bias, (0, pad))[None, :], jnp.pad(a_log, (0, pad))[None, :],
      jnp.repeat(d_skip, SSD_HEADDIM)[None, :])


def _post_kernel(y_ref, z_ref, scb_ref, scc_ref, scx_ref, gate_ref, x_ref,
                 nw_ref, wa_ref, csw_ref, wb_ref, bg_ref, wo_ref, g1_ref, lng_ref, lnb_ref,
                 sh2_ref, sc2_ref, wr_ref, br_ref,
                 x1_ref, h2_ref, prob_ref, idx_ref, lrank_ref, cst_ref, sidt_ref, runtab_ref, cnt_ref,
                 sbuf, run, *, rows, n_tok):
    @pl.when(pl.program_id(0) == 0)
    def _():
        sbuf[...] = jnp.zeros(sbuf.shape, F32)
        run[...] = jnp.zeros(run.shape, F32)

    z = z_ref[...].astype(F32)
    yg = y_ref[...].astype(F32) * (z * _sigmoid(z))
    ms = jnp.mean(yg * yg, axis=-1, keepdims=True)
    yn = yg * lax.rsqrt(ms + RMS_EPS) * nw_ref[...]
    u_a = _dot(yn.astype(BF16), wa_ref[...])

    cx = scc_ref[...].astype(F32) * scx_ref[...].astype(F32)
    ext = jnp.concatenate([sbuf[...], cx], axis=0)
    v = csw_ref[SC_CONV - 1:SC_CONV, :] * cx
    for j in range(1, SC_CONV):
        v = v + csw_ref[SC_CONV - 1 - j:SC_CONV - j, :] * pltpu.roll(ext, j, axis=0)[SUBLANES:, :]
    sbuf[...] = cx[rows - SUBLANES:, :]
    u_b = _dot((scb_ref[...].astype(F32) * v).astype(BF16), wb_ref[...])

    gl = gate_ref[...].astype(F32) + bg_ref[...]
    merged = _sigmoid(gl[:, :D_MODEL]) * u_a + _sigmoid(gl[:, D_MODEL:]) * u_b
    mix = _dot(merged.astype(BF16), wo_ref[...])
    x1 = _layer_norm(DN_ALPHA * x_ref[...] + g1_ref[...] * mix) * lng_ref[...] + lnb_ref[...]
    x1_ref[...] = x1
    h2 = _layer_norm(x1) * (1.0 + sc2_ref[...]) + sh2_ref[...]
    half = D_MODEL // 2
    h2_ref[...] = _pack_bf16_pair(h2[:, :half], h2[:, half:])

    logits = _dot_split(h2, wr_ref[...]) + br_ref[...]
    lane = lax.broadcasted_iota(jnp.int32, (rows, LANES), 1).astype(F32)
    work = logits
    onehots, vals = [], []
    idx_out = jnp.zeros((rows, LANES), F32)
    for k in range(TOP_K):
        m = jnp.max(work, axis=-1, keepdims=True)
        ik = jnp.min(jnp.where(work == m, lane, float(LANES)), axis=-1, keepdims=True)
        oh = lane == ik
        onehots.append(oh)
        vals.append(m)
        idx_out = jnp.where(lane == float(k), ik, idx_out)
        work = jnp.where(oh, -jnp.inf, work)
    es = [jnp.exp(vk - vals[0]) for vk in vals]
    denom = es[0] + es[1] + es[2] + es[3]
    prob_out = jnp.zeros((rows, LANES), F32)
    for k in range(TOP_K):
        prob_out = jnp.where(lane == float(k), es[k] / denom, prob_out)

    sel = jnp.zeros((rows, LANES), F32)
    kk = jnp.zeros((rows, LANES), F32)
    for k, oh in enumerate(onehots):
        sel = sel + jnp.where(oh, 1.0, 0.0)
        kk = kk + jnp.where(oh, float(k), 0.0)
    ri = lax.broadcasted_iota(jnp.int32, (rows, rows), 0)
    rj = lax.broadcasted_iota(jnp.int32, (rows, rows), 1)
    strict = jnp.where(ri > rj, 1.0, 0.0).astype(BF16)
    in_block = _dot(strict, sel.astype(BF16))
    base = in_block + run[...]
    lrank_out = jnp.zeros((rows, LANES), F32)
    for k in range(TOP_K):
        rk = jnp.sum(jnp.where(onehots[k], in_block, 0.0), axis=-1, keepdims=True)
        lrank_out = jnp.where(lane == float(k), rk, lrank_out)
    idx_ref[...] = idx_out.astype(jnp.int32)
    lrank_ref[...] = lrank_out.astype(jnp.int32)
    tok = (pl.program_id(0) * rows
           + lax.broadcasted_iota(jnp.int32, (rows, LANES), 0)).astype(F32)
    picked = sel > 0.0
    cs = jnp.where(picked, base + 1.0, 0.0)
    sid = jnp.where(picked, kk * float(n_tok) + tok, 0.0)
    cst_ref[...] = cs.T[:N_EXPERTS, :]
    sidt_ref[...] = sid.T[:N_EXPERTS, :]
    runtab_ref[0] = run[...].astype(jnp.int32)
    run[...] = run[...] + jnp.sum(sel, axis=0, keepdims=True)
    prob_ref[...] = prob_out
    cnt_ref[...] = run[...].astype(jnp.int32)


def _post(y, u, x, vecs, mats, rows):
    t = x.shape[0]
    kern = functools.partial(_post_kernel, rows=rows, n_tok=t)
    row = lambda w: pl.BlockSpec((1, w), lambda i: (0, 0))
    mat = lambda a, b: pl.BlockSpec((a, b), lambda i: (0, 0))
    ublk = lambda w, col: pl.BlockSpec((rows, w), lambda i: (i, col // w))
    nw, csw, bg, g1, lng, lnb, sh2, sc2, br = vecs
    wa, wb, wo, wr = mats
    tok = pl.BlockSpec((rows, LANES), lambda i: (i, 0))
    return pl.pallas_call(
        kern,
        grid=(t // rows,),
        in_specs=[
            pl.BlockSpec((rows, SSD_INNER), lambda i: (i, 0)),
            ublk(SSD_INNER, COL_Z), ublk(D_MODEL, COL_SCB), ublk(D_MODEL, COL_SCC),
            ublk(D_MODEL, COL_SCX), ublk(2 * D_MODEL, COL_GATE),
            pl.BlockSpec((rows, D_MODEL), lambda i: (i, 0)),
            row(SSD_INNER), mat(SSD_INNER, D_MODEL), mat(SC_CONV, D_MODEL), mat(D_MODEL, D_MODEL),
            row(2 * D_MODEL), mat(D_MODEL, D_MODEL), row(D_MODEL), row(D_MODEL), row(D_MODEL),
            row(D_MODEL), row(D_MODEL), mat(D_MODEL, LANES), row(LANES),
        ],
        out_specs=[
            pl.BlockSpec((rows, D_MODEL), lambda i: (i, 0)),
            pl.BlockSpec((rows, D_MODEL // 2), lambda i: (i, 0)),
            tok, tok, tok,
            pl.BlockSpec((N_EXPERTS, rows), lambda i: (0, i)),
            pl.BlockSpec((N_EXPERTS, rows), lambda i: (0, i)),
            pl.BlockSpec((1, 1, LANES), lambda i: (i, 0, 0)),
            pl.BlockSpec((1, LANES), lambda i: (0, 0)),
        ],
        out_shape=[
            jax.ShapeDtypeStruct((t, D_MODEL), F32),
            jax.ShapeDtypeStruct((t, D_MODEL // 2), jnp.uint32),
            jax.ShapeDtypeStruct((t, LANES), F32),
            jax.ShapeDtypeStruct((t, LANES), jnp.int32),
            jax.ShapeDtypeStruct((t, LANES), jnp.int32),
            jax.ShapeDtypeStruct((N_EXPERTS, t), F32),
            jax.ShapeDtypeStruct((N_EXPERTS, t), F32),
            jax.ShapeDtypeStruct((t // rows, 1, LANES), jnp.int32),
            jax.ShapeDtypeStruct((1, LANES), jnp.int32),
        ],
        scratch_shapes=[
            pltpu.VMEM((SUBLANES, D_MODEL), F32),
            pltpu.VMEM((1, LANES), F32),
        ],
        compiler_params=pltpu.CompilerParams(
            dimension_semantics=("arbitrary",), vmem_limit_bytes=VMEM_LIMIT),
        name="post_mix",
    )(y, u, u, u, u, u, x, nw, wa, csw, wb, bg, wo, g1, lng, lnb, sh2, sc2, wr, br)


def _invert_kernel(vt_ref, ve_ref, nv_ref, gs_ref, run_ref, cst_ref, sidt_ref, o_ref, ptr,
                   *, tile, tok_tile, n_tok_tiles):
    v = pl.program_id(0)
    i = vt_ref[v]
    e = ve_ref[v]
    vp = jnp.maximum(v - 1, 0)
    new_tile = jnp.logical_or(v == 0, vt_ref[vp] != i)
    new_exp = jnp.logical_or(v == 0, ve_ref[vp] != e)

    @pl.when(new_exp)
    def _():
        ptr[0] = 0

    @pl.when(new_tile)
    def _():
        o_ref[...] = jnp.zeros(o_ref.shape, jnp.int32)

    @pl.when(v < nv_ref[0])
    def _():
        g0 = gs_ref[e]
        row0 = i * tile
        ra = jnp.maximum(g0, row0) - g0
        rb = jnp.minimum(gs_ref[e + 1], row0 + tile) - g0
        b_lo = lax.while_loop(lambda b: run_ref[(b + 1) * N_EXPERTS + e] <= ra,
                              lambda b: b + 1, ptr[0])
        ptr[0] = b_lo
        b_hi = lax.while_loop(
            lambda b: jnp.logical_and(b < n_tok_tiles, run_ref[b * N_EXPERTS + e] < rb),
            lambda b: b + 1, b_lo)
        n_blk = tile // LANES
        row_in_blk = lax.broadcasted_iota(jnp.int32, (LANES, LANES), 0).astype(F32)
        firsts = [(row0 - g0 + 1 + k * LANES).astype(F32) for k in range(n_blk)]

        def body(b, accs):
            c0 = pl.multiple_of(b * tok_tile, tok_tile)
            cs_row = cst_ref[pl.ds(e, 1), pl.ds(c0, tok_tile)]
            sid_row = sidt_ref[pl.ds(e, 1), pl.ds(c0, tok_tile)]
            out = []
            for k in range(n_blk):
                rel = cs_row - firsts[k]
                acc = accs[k]
                for j in range(tok_tile // LANES):
                    lanes = slice(j * LANES, (j + 1) * LANES)
                    acc = acc + jnp.where(rel[:, lanes] == row_in_blk, sid_row[:, lanes], 0.0)
                out.append(acc)
            return tuple(out)

        accs = lax.fori_loop(b_lo, b_hi, body,
                             tuple(jnp.zeros((LANES, LANES), F32) for _ in range(n_blk)))
        for k in range(n_blk):
            contrib = jnp.sum(accs[k].T, axis=0, keepdims=True)
            lanes = slice(k * LANES, (k + 1) * LANES)
            o_ref[0, :, lanes] = o_ref[0, :, lanes] + contrib.astype(jnp.int32)


def _invert(v_tile, v_exp, n_visits, gs, runflat, cst, sidt, n_tiles, tile, tok_tile):
    t = cst.shape[1]
    n_vis = v_tile.shape[0]
    kern = functools.partial(_invert_kernel, tile=tile, tok_tile=tok_tile,
                             n_tok_tiles=t // tok_tile)
    return pl.pallas_call(
        kern,
        grid_spec=pltpu.PrefetchScalarGridSpec(
            num_scalar_prefetch=5,
            grid=(n_vis,),
            in_specs=[
                pl.BlockSpec((N_EXPERTS, t), lambda v, *_: (0, 0)),
                pl.BlockSpec((N_EXPERTS, t), lambda v, *_: (0, 0)),
            ],
            out_specs=pl.BlockSpec((1, 1, tile), lambda v, vt, *_: (vt[v], 0, 0)),
            scratch_shapes=[pltpu.SMEM((1,), jnp.int32)],
        ),
        out_shape=jax.ShapeDtypeStruct((n_tiles, 1, tile), jnp.int32),
        compiler_params=pltpu.CompilerParams(
            dimension_semantics=("arbitrary",), vmem_limit_bytes=VMEM_LIMIT),
        name="moe_invert",
    )(v_tile, v_exp, n_visits, gs, runflat, cst, sidt)


N_FF_BLK = 4
FF_BLK = EXPERT_FF // N_FF_BLK
N_STAGES = 2 * N_FF_BLK


def _experts_kernel(vt_ref, ve_ref, vn_ref, nv_ref, gs_ref,
                    ord_cur, ord_next, h_hbm, wgu_hbm, bgu_ref, wd_hbm, bd_ref,
                    o_ref, h_vmem, x_a, x_b, act, wgu_st, wd_st, wgu_bf, wd_bf,
                    hsem, wsem, *, tile, n_tok, layer):
    v = pl.program_id(0)
    nv = nv_ref[0]
    i = vt_ref[v]
    e = ve_ref[v]
    vp = jnp.maximum(v - 1, 0)
    active = v < nv
    new_tile = jnp.logical_or(v == 0, vt_ref[vp] != i)
    new_exp = jnp.logical_or(v == 0, ve_ref[vp] != e)
    even = lax.rem(i, 2) == 0
    half = D_MODEL // 2
    xs = (x_a, x_b)

    def weight_copies(expert):
        return (pltpu.make_async_copy(wgu_hbm.at[layer, expert], wgu_st, wsem.at[0]),
                pltpu.make_async_copy(wd_hbm.at[layer, expert], wd_st, wsem.at[1]))

    def fetch_row(order_ref, r, dst):
        sid = order_ref[r]
        tok = sid & (n_tok - 1) if n_tok & (n_tok - 1) == 0 else lax.rem(sid, n_tok)
        dst[pl.ds(r, 1), :] = h_vmem[pl.ds(tok, 1), :]

    @pl.when(jnp.logical_not(active))
    def _():
        o_ref[...] = jnp.zeros(o_ref.shape, jnp.uint32)

    @pl.when(v == 0)
    def _():
        tokens = pltpu.make_async_copy(h_hbm, h_vmem, hsem)
        tokens.start()
        for c in weight_copies(e):
            c.start()
        tokens.wait()
        for r in range(tile):
            fetch_row(ord_cur, r, x_a)

    @pl.when(jnp.logical_and(active, new_exp))
    def _():
        for c in weight_copies(e):
            c.wait()
        wgu_bf[...] = wgu_st[...].astype(BF16)
        wd_bf[...] = wd_st[...].astype(BF16)

        @pl.when(vn_ref[v] >= 0)
        def _():
            for c in weight_copies(vn_ref[v]):
                c.start()

    rowpos = i * tile + lax.broadcasted_iota(jnp.int32, (tile, 1), 0)
    mine = jnp.logical_and(rowpos >= gs_ref[e], rowpos < gs_ref[e + 1])

    def compute(x_ref, first, between):
        x_lo, x_hi = _unpack_bf16_pair(x_ref[...])
        xb = jnp.concatenate([x_lo.astype(BF16), x_hi.astype(BF16)], axis=1)
        for cb in range(N_FF_BLK):
            c0 = cb * FF_BLK
            g = _dot(xb, wgu_bf[:, c0:c0 + FF_BLK]) + bgu_ref[:, c0:c0 + FF_BLK]
            u = (_dot(xb, wgu_bf[:, EXPERT_FF + c0:EXPERT_FF + c0 + FF_BLK])
                 + bgu_ref[:, EXPERT_FF + c0:EXPERT_FF + c0 + FF_BLK])
            g = jnp.minimum(g, SWIGLU_LIMIT)
            u = jnp.clip(u, -SWIGLU_LIMIT, SWIGLU_LIMIT)
            act[:, c0:c0 + FF_BLK] = ((u + 1.0) * g * _sigmoid(SWIGLU_ALPHA * g)).astype(BF16)
            between(cb)
        for pb in range(N_FF_BLK // 2):
            c0 = pb * FF_BLK
            o_lo = _dot(act[...], wd_bf[:, c0:c0 + FF_BLK]) + bd_ref[:, c0:c0 + FF_BLK]
            between(N_FF_BLK + 2 * pb)
            o_hi = (_dot(act[...], wd_bf[:, half + c0:half + c0 + FF_BLK])
                    + bd_ref[:, half + c0:half + c0 + FF_BLK])
            packed = _pack_bf16_pair(o_lo, o_hi)
            keep = jnp.uint32(0) if first else o_ref[:, c0:c0 + FF_BLK]
            o_ref[:, c0:c0 + FF_BLK] = jnp.where(mine, packed, keep)
            between(N_FF_BLK + 2 * pb + 1)

    per_stage = tile // N_STAGES

    for p in range(2):
        x_cur, x_nxt = xs[p], xs[1 - p]
        on_parity = even if p == 0 else jnp.logical_not(even)

        @pl.when(jnp.logical_and(jnp.logical_and(active, new_tile), on_parity))
        def _(x_cur=x_cur, x_nxt=x_nxt):
            def between(j):
                for r in range(j * per_stage, (j + 1) * per_stage):
                    fetch_row(ord_next, r, x_nxt)

            compute(x_cur, True, between)

        @pl.when(jnp.logical_and(jnp.logical_and(active, jnp.logical_not(new_tile)), on_parity))
        def _(x_cur=x_cur):
            compute(x_cur, False, lambda j: None)


def _experts(v_tile, v_exp, v_next, n_visits, gs, order_ext, h2, w_gu, b_gu, w_down, b_down,
             layer, n_tiles, tile):
    n_vis = v_tile.shape[0]
    n_tok = h2.shape[0]
    ff2 = 2 * EXPERT_FF
    half = D_MODEL // 2
    kern = functools.partial(_experts_kernel, tile=tile, n_tok=n_tok, layer=layer)
    smem_blk = lambda off: pl.BlockSpec(
        (tile,), lambda v, vt, *_: (vt[v] + off,), memory_space=pltpu.SMEM)
    bias = lambda n: pl.BlockSpec(
        (None, None, 1, n), lambda v, vt, ve, *_: (layer, ve[v], 0, 0))
    hbm = pl.BlockSpec(memory_space=pl.ANY)
    return pl.pallas_call(
        kern,
        grid_spec=pltpu.PrefetchScalarGridSpec(
            num_scalar_prefetch=5,
            grid=(n_vis,),
            in_specs=[smem_blk(0), smem_blk(1), hbm, hbm, bias(ff2), hbm, bias(D_MODEL)],
            out_specs=pl.BlockSpec(
                (tile, half),
                lambda v, vt, ve, vn, nv, *_: (jnp.where(v < nv[0], vt[v], n_tiles), 0)),
            scratch_shapes=[
                pltpu.VMEM((n_tok, half), jnp.uint32),
                pltpu.VMEM((tile, half), jnp.uint32),
                pltpu.VMEM((tile, half), jnp.uint32),
                pltpu.VMEM((tile, EXPERT_FF), BF16),
                pltpu.VMEM((D_MODEL, ff2), F32),
                pltpu.VMEM((EXPERT_FF, D_MODEL), F32),
                pltpu.VMEM((D_MODEL, ff2), BF16),
                pltpu.VMEM((EXPERT_FF, D_MODEL), BF16),
                pltpu.SemaphoreType.DMA(()),
                pltpu.SemaphoreType.DMA((2,)),
            ],
        ),
        out_shape=jax.ShapeDtypeStruct(((n_tiles + 1) * tile, half), jnp.uint32),
        compiler_params=pltpu.CompilerParams(
            dimension_semantics=("arbitrary",), vmem_limit_bytes=EXPERTS_VMEM_LIMIT),
        name="moe_experts",
    )(v_tile, v_exp, v_next, n_visits, gs, order_ext, order_ext, h2,
      w_gu, b_gu[:, :, None, :], w_down, b_down[:, :, None, :])


SEG_CHUNK = 64


def _segment_layout(gs, run_lo, run_hi):
    n = run_hi - run_lo
    lead = (gs + run_lo) & (SUBLANES - 1)
    span = jnp.where(n > 0, n + lead, 0)
    log_chunk = SEG_CHUNK.bit_length() - 1
    padded = lax.shift_left(lax.shift_right_logical(span + (SEG_CHUNK - 1), log_chunk), log_chunk)
    return lead, span, padded


def _combine_kernel(gs_ref, run_ref, row_ref, os_hbm, prob_ref, x1_ref, g2_ref, lng_ref,
                    lnb_ref, x2_ref, seg, g_0, g_1, g_2, g_3, n_started, sem, *, rows, n_blocks):
    b = pl.program_id(0)
    slot = lax.rem(b, 2)
    max_chunks = rows // SEG_CHUNK + 1

    def chunk_copy(src, dst, s):
        return pltpu.make_async_copy(os_hbm.at[pl.ds(src, SEG_CHUNK)],
                                     seg.at[s, pl.ds(dst, SEG_CHUNK)], sem.at[s])

    def start_segments(blk, s):
        off = jnp.int32(0)
        started = jnp.int32(0)
        for e in range(N_EXPERTS):
            first = run_ref[blk * N_EXPERTS + e]
            lead, span, padded = _segment_layout(gs_ref[e], first,
                                                 run_ref[(blk + 1) * N_EXPERTS + e])
            src0 = gs_ref[e] + first - lead
            for c in range(max_chunks):
                @pl.when(c * SEG_CHUNK < span)
                def _(off=off, src0=src0, c=c):
                    chunk_copy(pl.multiple_of(src0 + c * SEG_CHUNK, SUBLANES),
                               pl.multiple_of(off + c * SEG_CHUNK, SEG_CHUNK), s).start()
            off = off + padded
            started = started + lax.shift_right_logical(padded, SEG_CHUNK.bit_length() - 1)
        n_started[s] = started

    @pl.when(b == 0)
    def _():
        start_segments(0, 0)

    @pl.when(b + 1 < n_blocks)
    def _():
        start_segments(b + 1, 1 - slot)

    def wait_one(_, carry):
        chunk_copy(0, 0, slot).wait()
        return carry

    lax.fori_loop(0, n_started[slot], wait_one, 0)

    bufs = (g_0, g_1, g_2, g_3)
    for t in range(rows):
        for k in range(TOP_K):
            bufs[k][pl.ds(t, 1), :] = seg[slot, pl.ds(row_ref[t * TOP_K + k], 1), :]

    prob = prob_ref[...]
    ffn_lo = jnp.zeros((rows, D_MODEL // 2), F32)
    ffn_hi = jnp.zeros((rows, D_MODEL // 2), F32)
    for k in range(TOP_K):
        lo, hi = _unpack_bf16_pair(bufs[k][...])
        ffn_lo = ffn_lo + prob[:, k:k + 1] * lo
        ffn_hi = ffn_hi + prob[:, k:k + 1] * hi
    ffn = jnp.concatenate([ffn_lo, ffn_hi], axis=1)
    x2_ref[...] = (_layer_norm(DN_ALPHA * x1_ref[...] + g2_ref[...] * ffn) * lng_ref[...]
                   + lnb_ref[...])


def _combine(gs, runflat, idx, lrank, out_sorted, probs, x1, g2, lng, lnb, rows):
    t = x1.shape[0]
    half = D_MODEL // 2
    n_blocks = t // rows
    run = runflat.reshape(n_blocks + 1, N_EXPERTS)
    lead, _, padded = _segment_layout(gs[None, :N_EXPERTS], run[:-1], run[1:])
    seg_start = jnp.cumsum(padded, axis=1) - padded + lead
    onehot = idx[:, :TOP_K, None] == jnp.arange(N_EXPERTS, dtype=jnp.int32)[None, None, :]
    start_tk = jnp.sum(jnp.where(onehot, jnp.repeat(seg_start, rows, axis=0)[:, None, :], 0),
                       axis=-1)
    row_flat = (start_tk + lrank[:, :TOP_K]).reshape(-1).astype(jnp.int32)
    seg_rows = rows * TOP_K + N_EXPERTS * (SEG_CHUNK + SUBLANES)
    kern = functools.partial(_combine_kernel, rows=rows, n_blocks=n_blocks)
    row = pl.BlockSpec((1, D_MODEL), lambda i, *_: (0, 0))
    smem = pl.BlockSpec((rows * TOP_K,), lambda i, *_: (i,), memory_space=pltpu.SMEM)
    return pl.pallas_call(
        kern,
        grid_spec=pltpu.PrefetchScalarGridSpec(
            num_scalar_prefetch=2,
            grid=(n_blocks,),
            in_specs=[
                smem,
                pl.BlockSpec(memory_space=pl.ANY),
                pl.BlockSpec((rows, LANES), lambda i, *_: (i, 0)),
                pl.BlockSpec((rows, D_MODEL), lambda i, *_: (i, 0)),
                row, row, row,
            ],
            out_specs=pl.BlockSpec((rows, D_MODEL), lambda i, *_: (i, 0)),
            scratch_shapes=[
                pltpu.VMEM((2, seg_rows, half), jnp.uint32),
                pltpu.VMEM((rows, half), jnp.uint32),
                pltpu.VMEM((rows, half), jnp.uint32),
                pltpu.VMEM((rows, half), jnp.uint32),
                pltpu.VMEM((rows, half), jnp.uint32),
                pltpu.SMEM((2,), jnp.int32),
                pltpu.SemaphoreType.DMA((2,)),
            ],
        ),
        out_shape=jax.ShapeDtypeStruct((t, D_MODEL), F32),
        compiler_params=pltpu.CompilerParams(
            dimension_semantics=("arbitrary",), vmem_limit_bytes=VMEM_LIMIT),
        name="moe_combine",
    )(gs, runflat, row_flat, out_sorted, probs, x1, g2, lng, lnb)


def _tile(t, pref):
    return pref if t % pref == 0 else t


def _visit_plan(counts, tile, n_vis):
    ge = jnp.cumsum(counts)
    gs = ge - counts
    t_lo = gs // tile
    nt = jnp.where(counts > 0, (ge - 1) // tile - t_lo + 1, 0)
    vend = jnp.cumsum(nt)
    vbase = vend - nt
    n_visits = vend[-1:]
    vc = jnp.minimum(jnp.arange(n_vis, dtype=jnp.int32), n_visits[0] - 1)
    v_exp = jnp.sum((vend[None, :] <= vc[:, None]).astype(jnp.int32), axis=1)
    onehot = (v_exp[:, None] == jnp.arange(N_EXPERTS, dtype=jnp.int32)[None, :]).astype(jnp.int32)
    v_tile = jnp.sum(onehot * (t_lo - vbase)[None, :], axis=1) + vc
    gs33 = jnp.concatenate([gs, ge[-1:]])
    after = jnp.sum(onehot * vend[None, :], axis=1)
    e_after = jnp.sum((vend[None, :] <= after[:, None]).astype(jnp.int32), axis=1)
    v_next = jnp.where(after < n_visits[0], e_after, -1)
    return (v_tile.astype(jnp.int32), v_exp.astype(jnp.int32), v_next.astype(jnp.int32),
            n_visits.astype(jnp.int32), gs33.astype(jnp.int32))


def kernel(x, c, w_ada, b_ada, w_in, conv_ssd_w, conv_ssd_b, dt_bias, a_log, d_skip, ssd_norm_w,
           w_ssd_out, conv_short_w, w_short_out, b_gate, w_o, ln1_g, ln1_b, w_router, b_router,
           w_gu, b_gu, w_down, b_down, ln2_g, ln2_b):
    batch, seq, d = x.shape
    assert batch == 1 and d == D_MODEL
    depth = w_in.shape[0]
    t = seq
    xt = x.reshape(t, d)

    tm_in = _tile(t, 2048)
    rows_ssd = _tile(t, 512)
    chunk = 128
    rows_post = 256
    tile_e = 256
    assert t % rows_post == 0
    n_tiles = t * TOP_K // tile_e
    n_vis = n_tiles + N_EXPERTS

    mods = _ada_mod(c, w_ada, b_ada)

    w_t = jnp.swapaxes(w_in, 1, 2)
    w_t_bf = w_t.astype(BF16)
    c_dt = _C_SCB - SSD_HEADS
    w_dt = jnp.pad(w_t[:, c_dt:_C_SCB], ((0, 0), (0, LANES - SSD_HEADS), (0, 0)))
    w_r = jnp.pad(w_router, ((0, 0), (0, 0), (0, LANES - N_EXPERTS)))
    b_r = jnp.pad(b_router, ((0, 0), (0, LANES - N_EXPERTS)), constant_values=NEG_BIG)
    wa_bf = w_ssd_out.astype(BF16)
    wb_bf = w_short_out.astype(BF16)
    wo_bf = w_o.astype(BF16)
    cool = jnp.zeros((tile_e,), jnp.int32)

    for l in range(depth):
        m = mods[l]
        sh1, sc1, g1, sh2, sc2, g2 = [m[:, k * d:(k + 1) * d] for k in range(N_ADA)]
        u, dt_raw = _in_proj(xt, sh1, sc1, w_t_bf, w_dt[l], l, tm_in)
        y = _ssd(u, dt_raw, conv_ssd_w[l], conv_ssd_b[l], dt_bias[l], a_log[l], d_skip[l],
                 rows_ssd, chunk)
        vecs = (ssd_norm_w[l][None, :], conv_short_w[l], b_gate[l][None, :], g1,
                ln1_g[l][None, :], ln1_b[l][None, :], sh2, sc2, b_r[l][None, :])
        x1, h2, probs, idx, lrank, cst, sidt, runtab, cnt = _post(
            y, u, xt, vecs, (wa_bf[l], wb_bf[l], wo_bf[l], w_r[l]), rows_post)

        v_tile, v_exp, v_next, n_visits, gs = _visit_plan(cnt[0, :N_EXPERTS], tile_e, n_vis)
        runflat = jnp.concatenate([runtab[:, 0, :N_EXPERTS], cnt[:, :N_EXPERTS]]).reshape(-1)
        order = _invert(v_tile, v_exp, n_visits, gs, runflat, cst, sidt, n_tiles, tile_e,
                        rows_post)
        order_ext = jnp.concatenate([order.reshape(-1), cool])
        out_sorted = _experts(v_tile, v_exp, v_next, n_visits, gs, order_ext, h2, w_gu, b_gu,
                              w_down, b_down, l, n_tiles, tile_e)
        xt = _combine(gs, runflat, idx, lrank, out_sorted, probs, x1, g2, ln2_g[l][None, :],
                      ln2_b[l][None, :], rows_post)

    return xt.reshape(batch, seq, d)
```

```python
import functools

import jax
import jax.numpy as jnp
from jax import lax
from jax.experimental import pallas as pl
from jax.experimental.pallas import tpu as pltpu

F32 = jnp.float32
BF16 = jnp.bfloat16

D_MODEL = 1024
SSD_INNER = 2048
SSD_HEADS = 32
SSD_HEADDIM = 64
SSD_GROUPS = 4
SSD_STATE = 128
SSD_CONV = 4
SC_CONV = 3
N_EXPERTS = 32
TOP_K = 4
EXPERT_FF = 1024
SWIGLU_LIMIT = 7.0
SWIGLU_ALPHA = 1.702
DEPTH = 4
DN_ALPHA = (2.0 * DEPTH) ** 0.25
LN_EPS = 1e-5
RMS_EPS = 1e-5
N_ADA = 6

LANES = 128
SUBLANES = 8
U_MAIN = 10240
U_BLK = 2048
IN_PROJ_SUB = 512
_C_BC = 2 * SSD_INNER
_C_SCB = _C_BC + 2 * SSD_GROUPS * SSD_STATE + SSD_HEADS
_C_SCC = _C_SCB + D_MODEL
U_SRC = ((0,), (SSD_INNER,), (_C_SCC,), (_C_SCC + U_BLK,), (_C_BC, _C_SCB))
COL_Z = 0
COL_XS = 2048
COL_SCC = 4096
COL_SCX = 5120
COL_GATE = 6144
COL_B = 8192
COL_C = 8704
COL_SCB = 9216
NEG_BIG = -1e30

VMEM_LIMIT = 56 * 1024 * 1024
EXPERTS_VMEM_LIMIT = 60 * 1024 * 1024


def _sigmoid(v):
    return 1.0 / (1.0 + jnp.exp(-v))


def _softplus(v):
    return jnp.maximum(v, 0.0) + jnp.log(1.0 + jnp.exp(-jnp.abs(v)))


def _layer_norm(v):
    mu = jnp.mean(v, axis=-1, keepdims=True)
    vc = v - mu
    var = jnp.mean(vc * vc, axis=-1, keepdims=True)
    return vc * lax.rsqrt(var + LN_EPS)


def _split3(v):
    hi = v.astype(BF16)
    r1 = v - hi.astype(F32)
    mid = r1.astype(BF16)
    lo = (r1 - mid.astype(F32)).astype(BF16)
    return hi, mid, lo


def _dot(a, b):
    return jnp.dot(a, b, preferred_element_type=F32)


def _pack_bf16_pair(lo, hi):
    lo_bits = pltpu.bitcast(lo.astype(BF16).astype(F32), jnp.uint32)
    hi_bits = pltpu.bitcast(hi.astype(BF16).astype(F32), jnp.uint32)
    return lax.shift_right_logical(lo_bits, jnp.uint32(16)) | (hi_bits & jnp.uint32(0xFFFF0000))


def _unpack_bf16_pair(w):
    lo = pltpu.bitcast(lax.shift_left(w, jnp.uint32(16)), F32)
    hi = pltpu.bitcast(w & jnp.uint32(0xFFFF0000), F32)
    return lo, hi


def _dot_split(a, b):
    a_hi = a.astype(BF16)
    a_mid = (a - a_hi.astype(F32)).astype(BF16)
    b_hi = b.astype(BF16)
    b_mid = (b - b_hi.astype(F32)).astype(BF16)
    return _dot(a_hi, b_hi) + _dot(a_hi, b_mid) + _dot(a_mid, b_hi)


def _dot_exact_lhs(a_bf16, v):
    hi, mid, lo = _split3(v)
    return _dot(a_bf16, hi) + _dot(a_bf16, mid) + _dot(a_bf16, lo)


def _ada_kernel(c_ref, w_ref, b_ref, o_ref):
    c = c_ref[...]
    s = c * _sigmoid(c)
    o_ref[0] = jnp.sum(w_ref[0] * s, axis=0, keepdims=True) + b_ref[0]


def _ada_mod(c, w_ada, b_ada):
    depth, d, n = w_ada.shape
    tn = 1024
    return pl.pallas_call(
        _ada_kernel,
        grid=(depth, n // tn),
        in_specs=[
            pl.BlockSpec((d, 1), lambda l, j: (0, 0)),
            pl.BlockSpec((1, d, tn), lambda l, j: (l, 0, j)),
            pl.BlockSpec((1, 1, tn), lambda l, j: (l, 0, j)),
        ],
        out_specs=pl.BlockSpec((1, 1, tn), lambda l, j: (l, 0, j)),
        out_shape=jax.ShapeDtypeStruct((depth, 1, n), F32),
        compiler_params=pltpu.CompilerParams(
            dimension_semantics=("arbitrary", "arbitrary"), vmem_limit_bytes=VMEM_LIMIT),
        name="ada_mod",
    )(c.reshape(d, 1), w_ada, b_ada.reshape(depth, 1, n))


def _dot_nt(a, b):
    return lax.dot_general(a, b, (((1,), (1,)), ((), ())), preferred_element_type=F32)


def _inproj_kernel(x_ref, sh_ref, sc_ref, w_hbm, wdt_ref, u_ref, dt_ref, h_scr, wbuf, wsem,
                   *, layer, n_steps):
    i = pl.program_id(0)
    j = pl.program_id(1)
    n_blk = len(U_SRC)
    step = i * n_blk + j
    slot = lax.rem(step, 2)

    def block_copies(blk, s):
        pieces = U_SRC[blk]
        rows_per = U_BLK // len(pieces)
        return [pltpu.make_async_copy(w_hbm.at[layer, pl.ds(src, rows_per)],
                                      wbuf.at[s, pl.ds(p * rows_per, rows_per)], wsem.at[s])
                for p, src in enumerate(pieces)]

    @pl.when(step == 0)
    def _():
        for c in block_copies(0, 0):
            c.start()

    for blk in range(n_blk):
        @pl.when(jnp.logical_and(j == blk, step + 1 < n_steps))
        def _(blk=blk):
            for c in block_copies((blk + 1) % n_blk, 1 - slot):
                c.start()

    tm = x_ref.shape[0]
    sub = min(tm, IN_PROJ_SUB)

    @pl.when(j == 0)
    def _():
        w = wdt_ref[...]
        w_hi = w.astype(BF16)
        w_mid = (w - w_hi.astype(F32)).astype(BF16)
        for r in range(0, tm, sub):
            h = _layer_norm(x_ref[r:r + sub, :]) * (1.0 + sc_ref[...]) + sh_ref[...]
            h_hi = h.astype(BF16)
            h_scr[r:r + sub, :] = h_hi
            h_mid = (h - h_hi.astype(F32)).astype(BF16)
            dt_ref[r:r + sub, :] = (_dot_nt(h_hi, w_hi) + _dot_nt(h_hi, w_mid)
                                    + _dot_nt(h_mid, w_hi))

    for blk in range(n_blk):
        @pl.when(j == blk)
        def _(blk=blk):
            for c in block_copies(blk, slot):
                c.wait()

    for r in range(0, tm, sub):
        u_ref[r:r + sub, :] = _dot_nt(h_scr[r:r + sub, :], wbuf[slot]).astype(BF16)


def _in_proj(x, sh, sc, w_t, w_dt, layer, tm):
    t, d = x.shape
    tn = U_BLK
    assert w_t.shape[2] == d and w_dt.shape == (LANES, d) and len(U_SRC) * tn == U_MAIN
    kern = functools.partial(_inproj_kernel, layer=layer, n_steps=(t // tm) * len(U_SRC))
    return pl.pallas_call(
        kern,
        grid=(t // tm, U_MAIN // tn),
        in_specs=[
            pl.BlockSpec((tm, d), lambda i, j: (i, 0)),
            pl.BlockSpec((1, d), lambda i, j: (0, 0)),
            pl.BlockSpec((1, d), lambda i, j: (0, 0)),
            pl.BlockSpec(memory_space=pl.ANY),
            pl.BlockSpec((LANES, d), lambda i, j: (0, 0)),
        ],
        out_specs=[
            pl.BlockSpec((tm, tn), lambda i, j: (i, j)),
            pl.BlockSpec((tm, LANES), lambda i, j: (i, 0)),
        ],
        out_shape=[
            jax.ShapeDtypeStruct((t, U_MAIN), BF16),
            jax.ShapeDtypeStruct((t, LANES), F32),
        ],
        scratch_shapes=[
            pltpu.VMEM((tm, d), BF16),
            pltpu.VMEM((2, tn, d), BF16),
            pltpu.SemaphoreType.DMA((2,)),
        ],
        compiler_params=pltpu.CompilerParams(
            dimension_semantics=("arbitrary", "arbitrary"), vmem_limit_bytes=VMEM_LIMIT),
        name="in_proj",
    )(x, sh, sc, w_t, w_dt)


def _ssd_kernel(xs_ref, b_ref, c_ref, dt_ref, cwx_ref, cwb_ref, cwc_ref, cbx_ref, cbb_ref,
                cbc_ref, dtb_ref, alog_ref, dskip_ref, y_ref,
                xbuf, bbuf, cbuf, xcs, bcs, ccs, state, *, chunk, rows):
    @pl.when(pl.program_id(0) == 0)
    def _():
        xbuf[...] = jnp.zeros(xbuf.shape, F32)
        bbuf[...] = jnp.zeros(bbuf.shape, F32)
        cbuf[...] = jnp.zeros(cbuf.shape, F32)
        state[...] = jnp.zeros(state.shape, F32)

    def conv_silu(in_ref, tail, w_ref, bias_ref, out_scr):
        x = in_ref[...].astype(F32)
        ext = jnp.concatenate([tail[...], x], axis=0)
        acc = bias_ref[...] + w_ref[SSD_CONV - 1:SSD_CONV, :] * x
        for j in range(1, SSD_CONV):
            shifted = pltpu.roll(ext, j, axis=0)[SUBLANES:, :]
            acc = acc + w_ref[SSD_CONV - 1 - j:SSD_CONV - j, :] * shifted
        out_scr[...] = acc * _sigmoid(acc)
        tail[...] = x[rows - SUBLANES:, :]

    conv_silu(xs_ref, xbuf, cwx_ref, cbx_ref, xcs)
    conv_silu(b_ref, bbuf, cwb_ref, cbb_ref, bcs)
    conv_silu(c_ref, cbuf, cwc_ref, cbc_ref, ccs)

    li = lax.broadcasted_iota(jnp.int32, (chunk, chunk), 0)
    si = lax.broadcasted_iota(jnp.int32, (chunk, chunk), 1)
    causal = li >= si
    tri = jnp.where(causal, 1.0, 0.0).astype(BF16)
    first_half = lax.broadcasted_iota(jnp.int32, (1, LANES), 1) < SSD_HEADDIM
    a_row = -jnp.exp(alog_ref[...])
    heads_per_group = SSD_HEADS // SSD_GROUPS

    def chunk_body(ci, carry):
        r0 = pl.multiple_of(ci * chunk, chunk)
        dt = _softplus(dt_ref[pl.ds(r0, chunk), :] + dtb_ref[...])
        la = dt * a_row
        acum = _dot_exact_lhs(tri, la)
        acum_t = acum.T
        dt_t = dt.T
        last_t = acum_t[:, chunk - 1:chunk]
        w_t = jnp.exp(last_t - acum_t) * dt_t
        cdec_t = jnp.exp(last_t)

        cb = []
        bt = []
        cg = []
        for g in range(SSD_GROUPS):
            bg = bcs[pl.ds(r0, chunk), g * SSD_STATE:(g + 1) * SSD_STATE]
            cgv = ccs[pl.ds(r0, chunk), g * SSD_STATE:(g + 1) * SSD_STATE]
            btg = bg.T
            bt.append(btg)
            cg.append(cgv)
            cb.append(_dot(cgv.astype(BF16), btg.astype(BF16)))

        for pr in range(SSD_HEADS // 2):
            lo = pr * LANES
            xs_pair = xcs[pl.ds(r0, chunk), lo:lo + LANES]
            xs_bf = xs_pair.astype(BF16)
            prev = state[:, lo:lo + LANES]
            lhs_m, lhs_c, lhs_b, cd = [], [], [], []
            for h in (2 * pr, 2 * pr + 1):
                g = h // heads_per_group
                col = jnp.broadcast_to(acum[:, h:h + 1], (chunk, chunk))
                row = acum_t[h:h + 1, :]
                dec = jnp.exp(jnp.where(causal, col - row, NEG_BIG))
                lhs_m.append((cb[g] * dec * dt_t[h:h + 1, :]).astype(BF16))
                lhs_c.append((cg[g] * jnp.exp(col)).astype(BF16))
                lhs_b.append((bt[g] * w_t[h:h + 1, :]).astype(BF16))
                cd.append(cdec_t[h:h + 1, :])
            out = (_dot(jnp.concatenate(lhs_m, axis=0), xs_bf)
                   + _dot(jnp.concatenate(lhs_c, axis=0), prev.astype(BF16)))
            y_pair = jnp.where(first_half, out[0:chunk, :], out[chunk:2 * chunk, :])
            y_ref[pl.ds(r0, chunk), lo:lo + LANES] = (
                y_pair + xs_pair * dskip_ref[:, lo:lo + LANES]).astype(y_ref.dtype)
            st = _dot(jnp.concatenate(lhs_b, axis=0), xs_bf)
            cd_pair = jnp.where(first_half, cd[0], cd[1])
            state[:, lo:lo + LANES] = prev * cd_pair + jnp.where(
                first_half, st[0:SSD_STATE, :], st[SSD_STATE:2 * SSD_STATE, :])
        return carry

    lax.fori_loop(0, rows // chunk, chunk_body, 0)


def _ssd(u, dt_raw, cw, cb, dt_bias, a_log, d_skip, rows, chunk):
    t = u.shape[0]
    gn = SSD_GROUPS * SSD_STATE
    pad = LANES - SSD_HEADS
    assert chunk == SSD_STATE
    kern = functools.partial(_ssd_kernel, chunk=chunk, rows=rows)
    full = lambda shape: pl.BlockSpec(shape, lambda i: (0, 0))
    return pl.pallas_call(
        kern,
        grid=(t // rows,),
        in_specs=[
            pl.BlockSpec((rows, SSD_INNER), lambda i: (i, COL_XS // SSD_INNER)),
            pl.BlockSpec((rows, gn), lambda i: (i, COL_B // gn)),
            pl.BlockSpec((rows, gn), lambda i: (i, COL_C // gn)),
            pl.BlockSpec((rows, LANES), lambda i: (i, 0)),
            full((SSD_CONV, SSD_INNER)), full((SSD_CONV, gn)), full((SSD_CONV, gn)),
            full((1, SSD_INNER)), full((1, gn)), full((1, gn)),
            full((1, LANES)), full((1, LANES)), full((1, SSD_INNER)),
        ],
        out_specs=pl.BlockSpec((rows, SSD_INNER), lambda i: (i, 0)),
        out_shape=jax.ShapeDtypeStruct((t, SSD_INNER), BF16),
        scratch_shapes=[
            pltpu.VMEM((SUBLANES, SSD_INNER), F32),
            pltpu.VMEM((SUBLANES, gn), F32),
            pltpu.VMEM((SUBLANES, gn), F32),
            pltpu.VMEM((rows, SSD_INNER), F32),
            pltpu.VMEM((rows, gn), F32),
            pltpu.VMEM((rows, gn), F32),
            pltpu.VMEM((SSD_STATE, SSD_INNER), F32),
        ],
        compiler_params=pltpu.CompilerParams(
            dimension_semantics=("arbitrary",), vmem_limit_bytes=VMEM_LIMIT),
        name="ssd",
    )(u, u, u, dt_raw,
      cw[:, :SSD_INNER], cw[:, SSD_INNER:SSD_INNER + gn], cw[:, SSD_INNER + gn:],
      cb[None, :SSD_INNER], cb[None, SSD_INNER:SSD_INNER + gn], cb[None, SSD_INNER + gn:],
      jnp.pad(dt_bias, (0, pad))[None, :], jnp.pad(a_log, (0, pad))[None, :],
      jnp.repeat(d_skip, SSD_HEADDIM)[None, :])


def _post_kernel(y_ref, z_ref, scb_ref, scc_ref, scx_ref, gate_ref, x_ref,
                 nw_ref, wa_ref, csw_ref, wb_ref, bg_ref, wo_ref, g1_ref, lng_ref, lnb_ref,
                 sh2_ref, sc2_ref, wr_ref, br_ref,
                 x1_ref, h2_ref, prob_ref, idx_ref, lrank_ref, cst_ref, sidt_ref, runtab_ref, cnt_ref,
                 sbuf, run, *, rows, n_tok):
    @pl.when(pl.program_id(0) == 0)
    def _():
        sbuf[...] = jnp.zeros(sbuf.shape, F32)
        run[...] = jnp.zeros(run.shape, F32)

    z = z_ref[...].astype(F32)
    yg = y_ref[...].astype(F32) * (z * _sigmoid(z))
    ms = jnp.mean(yg * yg, axis=-1, keepdims=True)
    yn = yg * lax.rsqrt(ms + RMS_EPS) * nw_ref[...]
    u_a = _dot(yn.astype(BF16), wa_ref[...])

    cx = scc_ref[...].astype(F32) * scx_ref[...].astype(F32)
    ext = jnp.concatenate([sbuf[...], cx], axis=0)
    v = csw_ref[SC_CONV - 1:SC_CONV, :] * cx
    for j in range(1, SC_CONV):
        v = v + csw_ref[SC_CONV - 1 - j:SC_CONV - j, :] * pltpu.roll(ext, j, axis=0)[SUBLANES:, :]
    sbuf[...] = cx[rows - SUBLANES:, :]
    u_b = _dot((scb_ref[...].astype(F32) * v).astype(BF16), wb_ref[...])

    gl = gate_ref[...].astype(F32) + bg_ref[...]
    merged = _sigmoid(gl[:, :D_MODEL]) * u_a + _sigmoid(gl[:, D_MODEL:]) * u_b
    mix = _dot(merged.astype(BF16), wo_ref[...])
    x1 = _layer_norm(DN_ALPHA * x_ref[...] + g1_ref[...] * mix) * lng_ref[...] + lnb_ref[...]
    x1_ref[...] = x1
    h2 = _layer_norm(x1) * (1.0 + sc2_ref[...]) + sh2_ref[...]
    half = D_MODEL // 2
    h2_ref[...] = _pack_bf16_pair(h2[:, :half], h2[:, half:])

    logits = _dot_split(h2, wr_ref[...]) + br_ref[...]
    lane = lax.broadcasted_iota(jnp.int32, (rows, LANES), 1).astype(F32)
    work = logits
    onehots, vals = [], []
    idx_out = jnp.zeros((rows, LANES), F32)
    for k in range(TOP_K):
        m = jnp.max(work, axis=-1, keepdims=True)
        ik = jnp.min(jnp.where(work == m, lane, float(LANES)), axis=-1, keepdims=True)
        oh = lane == ik
        onehots.append(oh)
        vals.append(m)
        idx_out = jnp.where(lane == float(k), ik, idx_out)
        work = jnp.where(oh, -jnp.inf, work)
    es = [jnp.exp(vk - vals[0]) for vk in vals]
    denom = es[0] + es[1] + es[2] + es[3]
    prob_out = jnp.zeros((rows, LANES), F32)
    for k in range(TOP_K):
        prob_out = jnp.where(lane == float(k), es[k] / denom, prob_out)

    sel = jnp.zeros((rows, LANES), F32)
    kk = jnp.zeros((rows, LANES), F32)
    for k, oh in enumerate(onehots):
        sel = sel + jnp.where(oh, 1.0, 0.0)
        kk = kk + jnp.where(oh, float(k), 0.0)
    ri = lax.broadcasted_iota(jnp.int32, (rows, rows), 0)
    rj = lax.broadcasted_iota(jnp.int32, (rows, rows), 1)
    strict = jnp.where(ri > rj, 1.0, 0.0).astype(BF16)
    in_block = _dot(strict, sel.astype(BF16))
    base = in_block + run[...]
    lrank_out = jnp.zeros((rows, LANES), F32)
    for k in range(TOP_K):
        rk = jnp.sum(jnp.where(onehots[k], in_block, 0.0), axis=-1, keepdims=True)
        lrank_out = jnp.where(lane == float(k), rk, lrank_out)
    idx_ref[...] = idx_out.astype(jnp.int32)
    lrank_ref[...] = lrank_out.astype(jnp.int32)
    tok = (pl.program_id(0) * rows
           + lax.broadcasted_iota(jnp.int32, (rows, LANES), 0)).astype(F32)
    picked = sel > 0.0
    cs = jnp.where(picked, base + 1.0, 0.0)
    sid = jnp.where(picked, kk * float(n_tok) + tok, 0.0)
    cst_ref[...] = cs.T[:N_EXPERTS, :]
    sidt_ref[...] = sid.T[:N_EXPERTS, :]
    runtab_ref[0] = run[...].astype(jnp.int32)
    run[...] = run[...] + jnp.sum(sel, axis=0, keepdims=True)
    prob_ref[...] = prob_out
    cnt_ref[...] = run[...].astype(jnp.int32)


def _post(y, u, x, vecs, mats, rows):
    t = x.shape[0]
    kern = functools.partial(_post_kernel, rows=rows, n_tok=t)
    row = lambda w: pl.BlockSpec((1, w), lambda i: (0, 0))
    mat = lambda a, b: pl.BlockSpec((a, b), lambda i: (0, 0))
    ublk = lambda w, col: pl.BlockSpec((rows, w), lambda i: (i, col // w))
    nw, csw, bg, g1, lng, lnb, sh2, sc2, br = vecs
    wa, wb, wo, wr = mats
    tok = pl.BlockSpec((rows, LANES), lambda i: (i, 0))
    return pl.pallas_call(
        kern,
        grid=(t // rows,),
        in_specs=[
            pl.BlockSpec((rows, SSD_INNER), lambda i: (i, 0)),
            ublk(SSD_INNER, COL_Z), ublk(D_MODEL, COL_SCB), ublk(D_MODEL, COL_SCC),
            ublk(D_MODEL, COL_SCX), ublk(2 * D_MODEL, COL_GATE),
            pl.BlockSpec((rows, D_MODEL), lambda i: (i, 0)),
            row(SSD_INNER), mat(SSD_INNER, D_MODEL), mat(SC_CONV, D_MODEL), mat(D_MODEL, D_MODEL),
            row(2 * D_MODEL), mat(D_MODEL, D_MODEL), row(D_MODEL), row(D_MODEL), row(D_MODEL),
            row(D_MODEL), row(D_MODEL), mat(D_MODEL, LANES), row(LANES),
        ],
        out_specs=[
            pl.BlockSpec((rows, D_MODEL), lambda i: (i, 0)),
            pl.BlockSpec((rows, D_MODEL // 2), lambda i: (i, 0)),
            tok, tok, tok,
            pl.BlockSpec((N_EXPERTS, rows), lambda i: (0, i)),
            pl.BlockSpec((N_EXPERTS, rows), lambda i: (0, i)),
            pl.BlockSpec((1, 1, LANES), lambda i: (i, 0, 0)),
            pl.BlockSpec((1, LANES), lambda i: (0, 0)),
        ],
        out_shape=[
            jax.ShapeDtypeStruct((t, D_MODEL), F32),
            jax.ShapeDtypeStruct((t, D_MODEL // 2), jnp.uint32),
            jax.ShapeDtypeStruct((t, LANES), F32),
            jax.ShapeDtypeStruct((t, LANES), jnp.int32),
            jax.ShapeDtypeStruct((t, LANES), jnp.int32),
            jax.ShapeDtypeStruct((N_EXPERTS, t), F32),
            jax.ShapeDtypeStruct((N_EXPERTS, t), F32),
            jax.ShapeDtypeStruct((t // rows, 1, LANES), jnp.int32),
            jax.ShapeDtypeStruct((1, LANES), jnp.int32),
        ],
        scratch_shapes=[
            pltpu.VMEM((SUBLANES, D_MODEL), F32),
            pltpu.VMEM((1, LANES), F32),
        ],
        compiler_params=pltpu.CompilerParams(
            dimension_semantics=("arbitrary",), vmem_limit_bytes=VMEM_LIMIT),
        name="post_mix",
    )(y, u, u, u, u, u, x, nw, wa, csw, wb, bg, wo, g1, lng, lnb, sh2, sc2, wr, br)


def _invert_kernel(vt_ref, ve_ref, nv_ref, gs_ref, run_ref, cst_ref, sidt_ref, o_ref, ptr,
                   *, tile, tok_tile, n_tok_tiles):
    v = pl.program_id(0)
    i = vt_ref[v]
    e = ve_ref[v]
    vp = jnp.maximum(v - 1, 0)
    new_tile = jnp.logical_or(v == 0, vt_ref[vp] != i)
    new_exp = jnp.logical_or(v == 0, ve_ref[vp] != e)

    @pl.when(new_exp)
    def _():
        ptr[0] = 0

    @pl.when(new_tile)
    def _():
        o_ref[...] = jnp.zeros(o_ref.shape, jnp.int32)

    @pl.when(v < nv_ref[0])
    def _():
        g0 = gs_ref[e]
        row0 = i * tile
        ra = jnp.maximum(g0, row0) - g0
        rb = jnp.minimum(gs_ref[e + 1], row0 + tile) - g0
        b_lo = lax.while_loop(lambda b: run_ref[(b + 1) * N_EXPERTS + e] <= ra,
                              lambda b: b + 1, ptr[0])
        ptr[0] = b_lo
        b_hi = lax.while_loop(
            lambda b: jnp.logical_and(b < n_tok_tiles, run_ref[b * N_EXPERTS + e] < rb),
            lambda b: b + 1, b_lo)
        n_blk = tile // LANES
        row_in_blk = lax.broadcasted_iota(jnp.int32, (LANES, LANES), 0).astype(F32)
        firsts = [(row0 - g0 + 1 + k * LANES).astype(F32) for k in range(n_blk)]

        def body(b, accs):
            c0 = pl.multiple_of(b * tok_tile, tok_tile)
            cs_row = cst_ref[pl.ds(e, 1), pl.ds(c0, tok_tile)]
            sid_row = sidt_ref[pl.ds(e, 1), pl.ds(c0, tok_tile)]
            out = []
            for k in range(n_blk):
                rel = cs_row - firsts[k]
                acc = accs[k]
                for j in range(tok_tile // LANES):
                    lanes = slice(j * LANES, (j + 1) * LANES)
                    acc = acc + jnp.where(rel[:, lanes] == row_in_blk, sid_row[:, lanes], 0.0)
                out.append(acc)
            return tuple(out)

        accs = lax.fori_loop(b_lo, b_hi, body,
                             tuple(jnp.zeros((LANES, LANES), F32) for _ in range(n_blk)))
        for k in range(n_blk):
            contrib = jnp.sum(accs[k].T, axis=0, keepdims=True)
            lanes = slice(k * LANES, (k + 1) * LANES)
            o_ref[0, :, lanes] = o_ref[0, :, lanes] + contrib.astype(jnp.int32)


def _invert(v_tile, v_exp, n_visits, gs, runflat, cst, sidt, n_tiles, tile, tok_tile):
    t = cst.shape[1]
    n_vis = v_tile.shape[0]
    kern = functools.partial(_invert_kernel, tile=tile, tok_tile=tok_tile,
                             n_tok_tiles=t // tok_tile)
    return pl.pallas_call(
        kern,
        grid_spec=pltpu.PrefetchScalarGridSpec(
            num_scalar_prefetch=5,
            grid=(n_vis,),
            in_specs=[
                pl.BlockSpec((N_EXPERTS, t), lambda v, *_: (0, 0)),
                pl.BlockSpec((N_EXPERTS, t), lambda v, *_: (0, 0)),
            ],
            out_specs=pl.BlockSpec((1, 1, tile), lambda v, vt, *_: (vt[v], 0, 0)),
            scratch_shapes=[pltpu.SMEM((1,), jnp.int32)],
        ),
        out_shape=jax.ShapeDtypeStruct((n_tiles, 1, tile), jnp.int32),
        compiler_params=pltpu.CompilerParams(
            dimension_semantics=("arbitrary",), vmem_limit_bytes=VMEM_LIMIT),
        name="moe_invert",
    )(v_tile, v_exp, n_visits, gs, runflat, cst, sidt)


N_FF_BLK = 4
FF_BLK = EXPERT_FF // N_FF_BLK
N_STAGES = 2 * N_FF_BLK


def _experts_kernel(vt_ref, ve_ref, vn_ref, nv_ref, gs_ref,
                    ord_cur, ord_next, h_hbm, wgu_hbm, bgu_ref, wd_hbm, bd_ref,
                    o_ref, h_vmem, x_a, x_b, act, wgu_st, wd_st, wgu_bf, wd_bf,
                    hsem, wsem, *, tile, n_tok, layer):
    v = pl.program_id(0)
    nv = nv_ref[0]
    i = vt_ref[v]
    e = ve_ref[v]
    vp = jnp.maximum(v - 1, 0)
    active = v < nv
    new_tile = jnp.logical_or(v == 0, vt_ref[vp] != i)
    new_exp = jnp.logical_or(v == 0, ve_ref[vp] != e)
    even = lax.rem(i, 2) == 0
    half = D_MODEL // 2
    xs = (x_a, x_b)

    def weight_copies(expert):
        return (pltpu.make_async_copy(wgu_hbm.at[layer, expert], wgu_st, wsem.at[0]),
                pltpu.make_async_copy(wd_hbm.at[layer, expert], wd_st, wsem.at[1]))

    def fetch_row(order_ref, r, dst):
        sid = order_ref[r]
        tok = sid & (n_tok - 1) if n_tok & (n_tok - 1) == 0 else lax.rem(sid, n_tok)
        dst[pl.ds(r, 1), :] = h_vmem[pl.ds(tok, 1), :]

    @pl.when(jnp.logical_not(active))
    def _():
        o_ref[...] = jnp.zeros(o_ref.shape, jnp.uint32)

    @pl.when(v == 0)
    def _():
        tokens = pltpu.make_async_copy(h_hbm, h_vmem, hsem)
        tokens.start()
        for c in weight_copies(e):
            c.start()
        tokens.wait()
        for r in range(tile):
            fetch_row(ord_cur, r, x_a)

    @pl.when(jnp.logical_and(active, new_exp))
    def _():
        for c in weight_copies(e):
            c.wait()
        wgu_bf[...] = wgu_st[...].astype(BF16)
        wd_bf[...] = wd_st[...].astype(BF16)

        @pl.when(vn_ref[v] >= 0)
        def _():
            for c in weight_copies(vn_ref[v]):
                c.start()

    rowpos = i * tile + lax.broadcasted_iota(jnp.int32, (tile, 1), 0)
    mine = jnp.logical_and(rowpos >= gs_ref[e], rowpos < gs_ref[e + 1])

    def compute(x_ref, first, between):
        x_lo, x_hi = _unpack_bf16_pair(x_ref[...])
        xb = jnp.concatenate([x_lo.astype(BF16), x_hi.astype(BF16)], axis=1)
        for cb in range(N_FF_BLK):
            c0 = cb * FF_BLK
            g = _dot(xb, wgu_bf[:, c0:c0 + FF_BLK]) + bgu_ref[:, c0:c0 + FF_BLK]
            u = (_dot(xb, wgu_bf[:, EXPERT_FF + c0:EXPERT_FF + c0 + FF_BLK])
                 + bgu_ref[:, EXPERT_FF + c0:EXPERT_FF + c0 + FF_BLK])
            g = jnp.minimum(g, SWIGLU_LIMIT)
            u = jnp.clip(u, -SWIGLU_LIMIT, SWIGLU_LIMIT)
            act[:, c0:c0 + FF_BLK] = ((u + 1.0) * g * _sigmoid(SWIGLU_ALPHA * g)).astype(BF16)
            between(cb)
        for pb in range(N_FF_BLK // 2):
            c0 = pb * FF_BLK
            o_lo = _dot(act[...], wd_bf[:, c0:c0 + FF_BLK]) + bd_ref[:, c0:c0 + FF_BLK]
            between(N_FF_BLK + 2 * pb)
            o_hi = (_dot(act[...], wd_bf[:, half + c0:half + c0 + FF_BLK])
                    + bd_ref[:, half + c0:half + c0 + FF_BLK])
            packed = _pack_bf16_pair(o_lo, o_hi)
            keep = jnp.uint32(0) if first else o_ref[:, c0:c0 + FF_BLK]
            o_ref[:, c0:c0 + FF_BLK] = jnp.where(mine, packed, keep)
            between(N_FF_BLK + 2 * pb + 1)

    per_stage = tile // N_STAGES

    for p in range(2):
        x_cur, x_nxt = xs[p], xs[1 - p]
        on_parity = even if p == 0 else jnp.logical_not(even)

        @pl.when(jnp.logical_and(jnp.logical_and(active, new_tile), on_parity))
        def _(x_cur=x_cur, x_nxt=x_nxt):
            def between(j):
                for r in range(j * per_stage, (j + 1) * per_stage):
                    fetch_row(ord_next, r, x_nxt)

            compute(x_cur, True, between)

        @pl.when(jnp.logical_and(jnp.logical_and(active, jnp.logical_not(new_tile)), on_parity))
        def _(x_cur=x_cur):
            compute(x_cur, False, lambda j: None)


def _experts(v_tile, v_exp, v_next, n_visits, gs, order_ext, h2, w_gu, b_gu, w_down, b_down,
             layer, n_tiles, tile):
    n_vis = v_tile.shape[0]
    n_tok = h2.shape[0]
    ff2 = 2 * EXPERT_FF
    half = D_MODEL // 2
    kern = functools.partial(_experts_kernel, tile=tile, n_tok=n_tok, layer=layer)
    smem_blk = lambda off: pl.BlockSpec(
        (tile,), lambda v, vt, *_: (vt[v] + off,), memory_space=pltpu.SMEM)
    bias = lambda n: pl.BlockSpec(
        (None, None, 1, n), lambda v, vt, ve, *_: (layer, ve[v], 0, 0))
    hbm = pl.BlockSpec(memory_space=pl.ANY)
    return pl.pallas_call(
        kern,
        grid_spec=pltpu.PrefetchScalarGridSpec(
            num_scalar_prefetch=5,
            grid=(n_vis,),
            in_specs=[smem_blk(0), smem_blk(1), hbm, hbm, bias(ff2), hbm, bias(D_MODEL)],
            out_specs=pl.BlockSpec(
                (tile, half),
                lambda v, vt, ve, vn, nv, *_: (jnp.where(v < nv[0], vt[v], n_tiles), 0)),
            scratch_shapes=[
                pltpu.VMEM((n_tok, half), jnp.uint32),
                pltpu.VMEM((tile, half), jnp.uint32),
                pltpu.VMEM((tile, half), jnp.uint32),
                pltpu.VMEM((tile, EXPERT_FF), BF16),
                pltpu.VMEM((D_MODEL, ff2), F32),
                pltpu.VMEM((EXPERT_FF, D_MODEL), F32),
                pltpu.VMEM((D_MODEL, ff2), BF16),
                pltpu.VMEM((EXPERT_FF, D_MODEL), BF16),
                pltpu.SemaphoreType.DMA(()),
                pltpu.SemaphoreType.DMA((2,)),
            ],
        ),
        out_shape=jax.ShapeDtypeStruct(((n_tiles + 1) * tile, half), jnp.uint32),
        compiler_params=pltpu.CompilerParams(
            dimension_semantics=("arbitrary",), vmem_limit_bytes=EXPERTS_VMEM_LIMIT),
        name="moe_experts",
    )(v_tile, v_exp, v_next, n_visits, gs, order_ext, order_ext, h2,
      w_gu, b_gu[:, :, None, :], w_down, b_down[:, :, None, :])


SEG_CHUNK = 64


def _segment_layout(gs, run_lo, run_hi):
    n = run_hi - run_lo
    lead = (gs + run_lo) & (SUBLANES - 1)
    span = jnp.where(n > 0, n + lead, 0)
    log_chunk = SEG_CHUNK.bit_length() - 1
    padded = lax.shift_left(lax.shift_right_logical(span + (SEG_CHUNK - 1), log_chunk), log_chunk)
    return lead, span, padded


def _combine_kernel(gs_ref, run_ref, row_ref, os_hbm, prob_ref, x1_ref, g2_ref, lng_ref,
                    lnb_ref, x2_ref, seg, g_0, g_1, g_2, g_3, n_started, sem, *, rows, n_blocks):
    b = pl.program_id(0)
    slot = lax.rem(b, 2)
    max_chunks = rows // SEG_CHUNK + 1

    def chunk_copy(src, dst, s):
        return pltpu.make_async_copy(os_hbm.at[pl.ds(src, SEG_CHUNK)],
                                     seg.at[s, pl.ds(dst, SEG_CHUNK)], sem.at[s])

    def start_segments(blk, s):
        off = jnp.int32(0)
        started = jnp.int32(0)
        for e in range(N_EXPERTS):
            first = run_ref[blk * N_EXPERTS + e]
            lead, span, padded = _segment_layout(gs_ref[e], first,
                                                 run_ref[(blk + 1) * N_EXPERTS + e])
            src0 = gs_ref[e] + first - lead
            for c in range(max_chunks):
                @pl.when(c * SEG_CHUNK < span)
                def _(off=off, src0=src0, c=c):
                    chunk_copy(pl.multiple_of(src0 + c * SEG_CHUNK, SUBLANES),
                               pl.multiple_of(off + c * SEG_CHUNK, SEG_CHUNK), s).start()
            off = off + padded
            started = started + lax.shift_right_logical(padded, SEG_CHUNK.bit_length() - 1)
        n_started[s] = started

    @pl.when(b == 0)
    def _():
        start_segments(0, 0)

    @pl.when(b + 1 < n_blocks)
    def _():
        start_segments(b + 1, 1 - slot)

    def wait_one(_, carry):
        chunk_copy(0, 0, slot).wait()
        return carry

    lax.fori_loop(0, n_started[slot], wait_one, 0)

    bufs = (g_0, g_1, g_2, g_3)
    for t in range(rows):
        for k in range(TOP_K):
            bufs[k][pl.ds(t, 1), :] = seg[slot, pl.ds(row_ref[t * TOP_K + k], 1), :]

    prob = prob_ref[...]
    ffn_lo = jnp.zeros((rows, D_MODEL // 2), F32)
    ffn_hi = jnp.zeros((rows, D_MODEL // 2), F32)
    for k in range(TOP_K):
        lo, hi = _unpack_bf16_pair(bufs[k][...])
        ffn_lo = ffn_lo + prob[:, k:k + 1] * lo
        ffn_hi = ffn_hi + prob[:, k:k + 1] * hi
    ffn = jnp.concatenate([ffn_lo, ffn_hi], axis=1)
    x2_ref[...] = (_layer_norm(DN_ALPHA * x1_ref[...] + g2_ref[...] * ffn) * lng_ref[...]
                   + lnb_ref[...])


def _combine(gs, runflat, idx, lrank, out_sorted, probs, x1, g2, lng, lnb, rows):
    t = x1.shape[0]
    half = D_MODEL // 2
    n_blocks = t // rows
    run = runflat.reshape(n_blocks + 1, N_EXPERTS)
    lead, _, padded = _segment_layout(gs[None, :N_EXPERTS], run[:-1], run[1:])
    seg_start = jnp.cumsum(padded, axis=1) - padded + lead
    onehot = idx[:, :TOP_K, None] == jnp.arange(N_EXPERTS, dtype=jnp.int32)[None, None, :]
    start_tk = jnp.sum(jnp.where(onehot, jnp.repeat(seg_start, rows, axis=0)[:, None, :], 0),
                       axis=-1)
    row_flat = (start_tk + lrank[:, :TOP_K]).reshape(-1).astype(jnp.int32)
    seg_rows = rows * TOP_K + N_EXPERTS * (SEG_CHUNK + SUBLANES)
    kern = functools.partial(_combine_kernel, rows=rows, n_blocks=n_blocks)
    row = pl.BlockSpec((1, D_MODEL), lambda i, *_: (0, 0))
    smem = pl.BlockSpec((rows * TOP_K,), lambda i, *_: (i,), memory_space=pltpu.SMEM)
    return pl.pallas_call(
        kern,
        grid_spec=pltpu.PrefetchScalarGridSpec(
            num_scalar_prefetch=2,
            grid=(n_blocks,),
            in_specs=[
                smem,
                pl.BlockSpec(memory_space=pl.ANY),
                pl.BlockSpec((rows, LANES), lambda i, *_: (i, 0)),
                pl.BlockSpec((rows, D_MODEL), lambda i, *_: (i, 0)),
                row, row, row,
            ],
            out_specs=pl.BlockSpec((rows, D_MODEL), lambda i, *_: (i, 0)),
            scratch_shapes=[
                pltpu.VMEM((2, seg_rows, half), jnp.uint32),
                pltpu.VMEM((rows, half), jnp.uint32),
                pltpu.VMEM((rows, half), jnp.uint32),
                pltpu.VMEM((rows, half), jnp.uint32),
                pltpu.VMEM((rows, half), jnp.uint32),
                pltpu.SMEM((2,), jnp.int32),
                pltpu.SemaphoreType.DMA((2,)),
            ],
        ),
        out_shape=jax.ShapeDtypeStruct((t, D_MODEL), F32),
        compiler_params=pltpu.CompilerParams(
            dimension_semantics=("arbitrary",), vmem_limit_bytes=VMEM_LIMIT),
        name="moe_combine",
    )(gs, runflat, row_flat, out_sorted, probs, x1, g2, lng, lnb)


def _tile(t, pref):
    return pref if t % pref == 0 else t


def _visit_plan(counts, tile, n_vis):
    ge = jnp.cumsum(counts)
    gs = ge - counts
    t_lo = gs // tile
    nt = jnp.where(counts > 0, (ge - 1) // tile - t_lo + 1, 0)
    vend = jnp.cumsum(nt)
    vbase = vend - nt
    n_visits = vend[-1:]
    vc = jnp.minimum(jnp.arange(n_vis, dtype=jnp.int32), n_visits[0] - 1)
    v_exp = jnp.sum((vend[None, :] <= vc[:, None]).astype(jnp.int32), axis=1)
    onehot = (v_exp[:, None] == jnp.arange(N_EXPERTS, dtype=jnp.int32)[None, :]).astype(jnp.int32)
    v_tile = jnp.sum(onehot * (t_lo - vbase)[None, :], axis=1) + vc
    gs33 = jnp.concatenate([gs, ge[-1:]])
    after = jnp.sum(onehot * vend[None, :], axis=1)
    e_after = jnp.sum((vend[None, :] <= after[:, None]).astype(jnp.int32), axis=1)
    v_next = jnp.where(after < n_visits[0], e_after, -1)
    return (v_tile.astype(jnp.int32), v_exp.astype(jnp.int32), v_next.astype(jnp.int32),
            n_visits.astype(jnp.int32), gs33.astype(jnp.int32))


def kernel(x, c, w_ada, b_ada, w_in, conv_ssd_w, conv_ssd_b, dt_bias, a_log, d_skip, ssd_norm_w,
           w_ssd_out, conv_short_w, w_short_out, b_gate, w_o, ln1_g, ln1_b, w_router, b_router,
           w_gu, b_gu, w_down, b_down, ln2_g, ln2_b):
    batch, seq, d = x.shape
    assert batch == 1 and d == D_MODEL
    depth = w_in.shape[0]
    t = seq
    xt = x.reshape(t, d)

    tm_in = _tile(t, 2048)
    rows_ssd = _tile(t, 1024)
    chunk = 128
    rows_post = 256
    tile_e = 256
    assert t % rows_post == 0
    n_tiles = t * TOP_K // tile_e
    n_vis = n_tiles + N_EXPERTS

    mods = _ada_mod(c, w_ada, b_ada)

    w_t = jnp.swapaxes(w_in, 1, 2)
    w_t_bf = w_t.astype(BF16)
    c_dt = _C_SCB - SSD_HEADS
    w_dt = jnp.pad(w_t[:, c_dt:_C_SCB], ((0, 0), (0, LANES - SSD_HEADS), (0, 0)))
    w_r = jnp.pad(w_router, ((0, 0), (0, 0), (0, LANES - N_EXPERTS)))
    b_r = jnp.pad(b_router, ((0, 0), (0, LANES - N_EXPERTS)), constant_values=NEG_BIG)
    wa_bf = w_ssd_out.astype(BF16)
    wb_bf = w_short_out.astype(BF16)
    wo_bf = w_o.astype(BF16)
    cool = jnp.zeros((tile_e,), jnp.int32)

    for l in range(depth):
        m = mods[l]
        sh1, sc1, g1, sh2, sc2, g2 = [m[:, k * d:(k + 1) * d] for k in range(N_ADA)]
        u, dt_raw = _in_proj(xt, sh1, sc1, w_t_bf, w_dt[l], l, tm_in)
        y = _ssd(u, dt_raw, conv_ssd_w[l], conv_ssd_b[l], dt_bias[l], a_log[l], d_skip[l],
                 rows_ssd, chunk)
        vecs = (ssd_norm_w[l][None, :], conv_short_w[l], b_gate[l][None, :], g1,
                ln1_g[l][None, :], ln1_b[l][None, :], sh2, sc2, b_r[l][None, :])
        x1, h2, probs, idx, lrank, cst, sidt, runtab, cnt = _post(
            y, u, xt, vecs, (wa_bf[l], wb_bf[l], wo_bf[l], w_r[l]), rows_post)

        v_tile, v_exp, v_next, n_visits, gs = _visit_plan(cnt[0, :N_EXPERTS], tile_e, n_vis)
        runflat = jnp.concatenate([runtab[:, 0, :N_EXPERTS], cnt[:, :N_EXPERTS]]).reshape(-1)
        order = _invert(v_tile, v_exp, n_visits, gs, runflat, cst, sidt, n_tiles, tile_e,
                        rows_post)
        order_ext = jnp.concatenate([order.reshape(-1), cool])
        out_sorted = _experts(v_tile, v_exp, v_next, n_visits, gs, order_ext, h2, w_gu, b_gu,
                              w_down, b_down, l, n_tiles, tile_e)
        xt = _combine(gs, runflat, idx, lrank, out_sorted, probs, x1, g2, ln2_g[l][None, :],
                      ln2_b[l][None, :], rows_post)

    return xt.reshape(batch, seq, d)
```
